```python
import math
import jax, jax.numpy as jnp
from jax import lax
import numpy as np

D_MODEL = 4096
BATCH = 2
SEQ = 8192
DEPTH = 4

CTX_LEN = 256
GRID_W = 64
HEAD_DIM = 128
ROPE_AXIS_DIM = HEAD_DIM // 2
ROPE_THETA = 10000.0
N_HEADS_A = D_MODEL // 256
N_KV_HEADS_A = N_HEADS_A // 4
GQA_GROUP = N_HEADS_A // N_KV_HEADS_A
WINDOW = 128
BLOCK = 128
N_HEADS_B = D_MODEL // 512
DIFF_V_DIM = 2 * HEAD_DIM
BRANCH_W_A = N_HEADS_A * HEAD_DIM
BRANCH_W_B = N_HEADS_B * DIFF_V_DIM
KA_W = N_KV_HEADS_A * HEAD_DIM
VA_W = N_KV_HEADS_A * HEAD_DIM
KB_W = N_HEADS_B * 2 * HEAD_DIM
VB_W = BRANCH_W_B
QA_W = BRANCH_W_A
QB_W = N_HEADS_B * 2 * HEAD_DIM
GATE_W = 2 * D_MODEL
KV_COLS = KA_W + VA_W + KB_W + VB_W
IN_COLS = KV_COLS + QA_W + QB_W + GATE_W
ADA_RANK = 256
N_MOD = 6
N_EXPERTS = 16
N_GROUPS = 4
EXPERTS_PER_GROUP = N_EXPERTS // N_GROUPS
TOPK_GROUPS = 1
TOP_K = 2
D_EXPERT = 512
ROUTED_SCALE = 1.0
NORM_EPS = 1e-6
NEG_INF = -1e30

kernel_name = "hybrid_dit_window_gqa_diffattn_grouped_moe"


def rms_norm(x, gain):
    xf = x.astype(jnp.float32)
    y = xf * lax.rsqrt(jnp.mean(xf * xf, axis=-1, keepdims=True) + NORM_EPS)
    return y.astype(x.dtype) * gain


def modulate(h, shift, scale):
    return h * (1 + scale[:, None]) + shift[:, None]


def adaln(silu_cond, down, up, bias):
    return (silu_cond @ down) @ up + bias


def axial_rope_tables(rows, dtype):
    row = jnp.repeat(jnp.arange(rows), GRID_W).astype(jnp.float32)
    col = jnp.tile(jnp.arange(GRID_W), rows).astype(jnp.float32)
    inv = ROPE_THETA ** (-(jnp.arange(0, ROPE_AXIS_DIM, 2, dtype=jnp.float32) / ROPE_AXIS_DIM))
    ang_r = row[:, None] * inv
    ang_c = col[:, None] * inv
    ang = jnp.concatenate([ang_r, ang_r, ang_c, ang_c], axis=-1)
    return jnp.cos(ang)[:, None, :].astype(dtype), jnp.sin(ang)[:, None, :].astype(dtype)


def apply_rope(x, cos, sin):
    xr = x.reshape(*x.shape[:-1], 2, 2, ROPE_AXIS_DIM // 2)
    rot = jnp.concatenate([-xr[..., 1:, :], xr[..., :1, :]], axis=-2).reshape(x.shape)
    return x * cos + rot * sin


def split_kv(p):
    o1 = KA_W
    o2 = o1 + VA_W
    o3 = o2 + KB_W
    return p[..., :o1], p[..., o1:o2], p[..., o2:o3], p[..., o3:KV_COLS]


def split_q_gate(p):
    o1 = KV_COLS + QA_W
    o2 = o1 + QB_W
    return p[..., KV_COLS:o1], p[..., o1:o2], p[..., o2:]


def softmax_with_sink(s, sink):
    m = jnp.maximum(jnp.max(s, axis=-1, keepdims=True), sink)
    e = jnp.exp(s - m)
    return e / (jnp.sum(e, axis=-1, keepdims=True) + jnp.exp(sink - m))


def window_attention(q, k, v, kc, vc, sink):
    B, S = q.shape[:2]
    nb = S // BLOCK
    q = q.reshape(B, nb, BLOCK, N_KV_HEADS_A, GQA_GROUP, HEAD_DIM) * (HEAD_DIM ** -0.5)

    def band(t):
        tb = t.reshape(B, nb, BLOCK, N_KV_HEADS_A, HEAD_DIM)
        tp = jnp.pad(tb, ((0, 0), (1, 1), (0, 0), (0, 0), (0, 0)))
        return jnp.concatenate([tp[:, :-2], tp[:, 1:-1], tp[:, 2:]], axis=2)

    kb, vb = band(k), band(v)
    s_band = jnp.einsum('bnqhgd,bnkhd->bnhgqk', q, kb, preferred_element_type=jnp.float32)
    s_ctx = jnp.einsum('bnqhgd,bkhd->bnhgqk', q, kc, preferred_element_type=jnp.float32)
    qi = jnp.arange(BLOCK)[:, None]
    kj = jnp.arange(3 * BLOCK)[None, :] - BLOCK
    in_win = jnp.abs(kj - qi) <= WINDOW
    abs_k = jnp.arange(nb)[:, None, None] * BLOCK + kj[None]
    mask = in_win[None] & (abs_k >= 0) & (abs_k < S)
    s_band = jnp.where(mask[None, :, None, None], s_band, NEG_INF)
    s = jnp.concatenate([s_band, s_ctx], axis=-1)
    sk = sink.reshape(N_KV_HEADS_A, GQA_GROUP).astype(jnp.float32)[None, None, :, :, None, None]
    p = softmax_with_sink(s, sk)
    p_band = p[..., :3 * BLOCK].astype(v.dtype)
    p_ctx = p[..., 3 * BLOCK:].astype(v.dtype)
    out = (jnp.einsum('bnhgqk,bnkhd->bnqhgd', p_band, vb)
           + jnp.einsum('bnhgqk,bkhd->bnqhgd', p_ctx, vc))
    return out.reshape(B, S, BRANCH_W_A)


def context_sink_attention(q, k, v, sink):
    B, C = q.shape[:2]
    q = q.reshape(B, C, N_KV_HEADS_A, GQA_GROUP, HEAD_DIM) * (HEAD_DIM ** -0.5)
    s = jnp.einsum('bqhgd,bkhd->bhgqk', q, k, preferred_element_type=jnp.float32)
    sk = sink.reshape(N_KV_HEADS_A, GQA_GROUP).astype(jnp.float32)[None, :, :, None, None]
    p = softmax_with_sink(s, sk).astype(v.dtype)
    return jnp.einsum('bhgqk,bkhd->bqhgd', p, v).reshape(B, C, BRANCH_W_A)


def diff_core(q, k, v, lam):
    s = jnp.einsum('bqhmd,bkhmd->bhmqk', q * (HEAD_DIM ** -0.5), k,
                   preferred_element_type=jnp.float32)
    p = jax.nn.softmax(s, axis=-1)
    a = p[:, :, 0] - lam * p[:, :, 1]
    return jnp.einsum('bhqk,bkhe->bqhe', a.astype(v.dtype), v)


def diff_attention_latent(q, k, v, kc, vc, lam):
    B, S = q.shape[:2]
    nb = S // BLOCK
    k_all = jnp.concatenate([kc, k], axis=1)
    v_all = jnp.concatenate([vc, v], axis=1)
    qb = q.reshape(B, nb, BLOCK, N_HEADS_B, 2, HEAD_DIM).swapaxes(0, 1)
    out = lax.map(lambda qblk: diff_core(qblk, k_all, v_all, lam), qb)
    return out.swapaxes(0, 1).reshape(B, S, N_HEADS_B, DIFF_V_DIM)


def diff_output(o, gain, lam_init):
    B, Q = o.shape[:2]
    return (rms_norm(o, gain) * (1.0 - lam_init)).reshape(B, Q, BRANCH_W_B)


def merge_branches(ya, yb, gate_pre, wa, wb, wo):
    ga, gb = jnp.split(gate_pre, 2, axis=-1)
    m = jax.nn.sigmoid(ga) * (ya @ wa) + jax.nn.sigmoid(gb) * (yb @ wb)
    return m @ wo


def token_mixers(hx, hc, w_in, sink, lam, lam_init, subln, wa, wb, wo, cos, sin, context_output):
    B, S, _ = hx.shape
    C = hc.shape[1]
    px = hx @ w_in
    pc = hc @ w_in if context_output else hc @ w_in[:, :KV_COLS]
    ka_x, va_x, kb_x, vb_x = split_kv(px)
    qa_x, qb_x, g_x = split_q_gate(px)
    ka_c, va_c, kb_c, vb_c = split_kv(pc)
    qa_x = apply_rope(qa_x.reshape(B, S, N_HEADS_A, HEAD_DIM), cos, sin)
    ka_x = apply_rope(ka_x.reshape(B, S, N_KV_HEADS_A, HEAD_DIM), cos, sin)
    va_x = va_x.reshape(B, S, N_KV_HEADS_A, HEAD_DIM)
    qb_x = apply_rope(qb_x.reshape(B, S, 2 * N_HEADS_B, HEAD_DIM), cos, sin).reshape(B, S, N_HEADS_B, 2, HEAD_DIM)
    kb_x = apply_rope(kb_x.reshape(B, S, 2 * N_HEADS_B, HEAD_DIM), cos, sin).reshape(B, S, N_HEADS_B, 2, HEAD_DIM)
    vb_x = vb_x.reshape(B, S, N_HEADS_B, DIFF_V_DIM)
    ka_c = ka_c.reshape(B, C, N_KV_HEADS_A, HEAD_DIM)
    va_c = va_c.reshape(B, C, N_KV_HEADS_A, HEAD_DIM)
    kb_c = kb_c.reshape(B, C, N_HEADS_B, 2, HEAD_DIM)
    vb_c = vb_c.reshape(B, C, N_HEADS_B, DIFF_V_DIM)
    ya_x = window_attention(qa_x, ka_x, va_x, ka_c, va_c, sink)
    yb_x = diff_output(diff_attention_latent(qb_x, kb_x, vb_x, kb_c, vb_c, lam), subln, lam_init)
    out_x = merge_branches(ya_x, yb_x, g_x, wa, wb, wo)
    if not context_output:
        return out_x, None
    qa_c, qb_c, g_c = split_q_gate(pc)
    ya_c = context_sink_attention(qa_c.reshape(B, C, N_HEADS_A, HEAD_DIM), ka_c, va_c, sink)
    yb_c = diff_output(diff_core(qb_c.reshape(B, C, N_HEADS_B, 2, HEAD_DIM), kb_c, vb_c, lam), subln, lam_init)
    out_c = merge_branches(ya_c, yb_c, g_c, wa, wb, wo)
    return out_x, out_c


def route(h, w_router, b_router):
    logits = jnp.einsum('bld,de->ble', h, w_router, preferred_element_type=jnp.float32)
    scores = jax.nn.sigmoid(logits)
    sel = scores + b_router.astype(jnp.float32)
    grp = sel.reshape(*sel.shape[:-1], N_GROUPS, EXPERTS_PER_GROUP)
    grp_score = jnp.sum(lax.top_k(grp, 2)[0], axis=-1)
    _, gidx = lax.top_k(grp_score, TOPK_GROUPS)
    gmask = jnp.any(gidx[..., None] == jnp.arange(N_GROUPS), axis=-2)
    emask = jnp.repeat(gmask, EXPERTS_PER_GROUP, axis=-1)
    _, eidx = lax.top_k(jnp.where(emask, sel, NEG_INF), TOP_K)
    w = jnp.take_along_axis(scores, eidx, axis=-1)
    w = w / jnp.sum(w, axis=-1, keepdims=True) * ROUTED_SCALE
    return jnp.sum(jax.nn.one_hot(eidx, N_EXPERTS, dtype=jnp.float32) * w[..., None], axis=-2)


def moe_ffn(h, comb, wg, wu, wd):
    g = jnp.einsum('bld,edf->blef', h, wg)
    u = jnp.einsum('bld,edf->blef', h, wu)
    a = jax.nn.silu(g) * u * comb[..., None].astype(h.dtype)
    return jnp.einsum('blef,efd->bld', a, wd)


def setup_inputs(seed: int = 0) -> dict:
    key = jax.random.key(seed)
    ks = jax.random.split(key, 25)
    D = D_MODEL

    def nrm(k, shape, scale):
        return jax.random.normal(k, shape, jnp.float32) * scale

    return {
        "x": nrm(ks[0], (BATCH, SEQ, D), 1.0),
        "c": nrm(ks[1], (BATCH, D), 1.0),
        "ctx": nrm(ks[2], (BATCH, CTX_LEN, D), 1.0),
        "c_ctx": nrm(ks[3], (D,), 1.0),
        "ada_down": nrm(ks[4], (DEPTH, D, ADA_RANK), D ** -0.5),
        "ada_up": nrm(ks[5], (DEPTH, ADA_RANK, N_MOD * D), 0.5 * ADA_RANK ** -0.5),
        "ada_bias": nrm(ks[6], (DEPTH, N_MOD * D), 0.02),
        "norm_mix": 1.0 + nrm(ks[7], (DEPTH, D), 0.02),
        "norm_ffn": 1.0 + nrm(ks[8], (DEPTH, D), 0.02),
        "w_in": nrm(ks[9], (DEPTH, D, IN_COLS), D ** -0.5),
        "sink_logit": nrm(ks[10], (DEPTH, N_HEADS_A), 0.5),
        "lam_q1": nrm(ks[11], (DEPTH, HEAD_DIM), 0.1),
        "lam_k1": nrm(ks[12], (DEPTH, HEAD_DIM), 0.1),
        "lam_q2": nrm(ks[13], (DEPTH, HEAD_DIM), 0.1),
        "lam_k2": nrm(ks[14], (DEPTH, HEAD_DIM), 0.1),
        "subln_gain": 1.0 + nrm(ks[15], (DEPTH, DIFF_V_DIM), 0.02),
        "w_branch_a": nrm(ks[16], (DEPTH, BRANCH_W_A, D), BRANCH_W_A ** -0.5),
        "w_branch_b": nrm(ks[17], (DEPTH, BRANCH_W_B, D), BRANCH_W_B ** -0.5),
        "w_out": nrm(ks[18], (DEPTH, D, D), D ** -0.5),
        "w_router": nrm(ks[19], (D, N_EXPERTS), D ** -0.5),
        "b_router": nrm(ks[20], (N_EXPERTS,), 0.01),
        "w_exp_gate": nrm(ks[21], (DEPTH, N_EXPERTS, D, D_EXPERT), D ** -0.5),
        "w_exp_up": nrm(ks[22], (DEPTH, N_EXPERTS, D, D_EXPERT), D ** -0.5),
        "w_exp_down": nrm(ks[23], (DEPTH, N_EXPERTS, D_EXPERT, D), D_EXPERT ** -0.5),
        "norm_final": 1.0 + nrm(ks[24], (D,), 0.02),
    }


def reference(x, c, ctx, c_ctx, ada_down, ada_up, ada_bias, norm_mix, norm_ffn, w_in, sink_logit,
              lam_q1, lam_k1, lam_q2, lam_k2, subln_gain, w_branch_a, w_branch_b, w_out,
              w_router, b_router, w_exp_gate, w_exp_up, w_exp_down, norm_final):
    B, S, _ = x.shape
    C = ctx.shape[1]
    rows = S // GRID_W
    cos, sin = axial_rope_tables(rows, x.dtype)
    silu_c = jax.nn.silu(c)
    silu_cc = jax.nn.silu(c_ctx)[None]
    xc = ctx
    for l in range(DEPTH):
        last = l == DEPTH - 1
        lam_init = 0.8 - 0.6 * math.exp(-0.3 * l)
        lam = (jnp.exp(jnp.sum((lam_q1[l] * lam_k1[l]).astype(jnp.float32)))
               - jnp.exp(jnp.sum((lam_q2[l] * lam_k2[l]).astype(jnp.float32))) + lam_init)
        mx = jnp.split(adaln(silu_c, ada_down[l], ada_up[l], ada_bias[l]), N_MOD, axis=-1)
        mc = jnp.split(adaln(silu_cc, ada_down[l], ada_up[l], ada_bias[l]), N_MOD, axis=-1)
        hx = modulate(rms_norm(x, norm_mix[l]), mx[0], mx[1])
        hc = modulate(rms_norm(xc, norm_mix[l]), mc[0], mc[1])
        out_x, out_c = token_mixers(hx, hc, w_in[l], sink_logit[l], lam, lam_init, subln_gain[l],
                                    w_branch_a[l], w_branch_b[l], w_out[l], cos, sin,
                                    context_output=not last)
        x = x + mx[2][:, None] * out_x
        hx = modulate(rms_norm(x, norm_ffn[l]), mx[3], mx[4])
        if last:
            y = moe_ffn(hx, route(hx, w_router, b_router), w_exp_gate[l], w_exp_up[l], w_exp_down[l])
            x = x + mx[5][:, None] * y
        else:
            xc = xc + mc[2][:, None] * out_c
            hc = modulate(rms_norm(xc, norm_ffn[l]), mc[3], mc[4])
            h = jnp.concatenate([hc, hx], axis=1)
            y = moe_ffn(h, route(h, w_router, b_router), w_exp_gate[l], w_exp_up[l], w_exp_down[l])
            xc = xc + mc[5][:, None] * y[:, :C]
            x = x + mx[5][:, None] * y[:, C:]
    return rms_norm(x, norm_final)
```

```python
import functools
import math

import jax
import jax.numpy as jnp
from jax import lax
from jax.experimental import pallas as pl
from jax.experimental.pallas import tpu as pltpu

HEAD_DIM = 128
GRID_W = 64
ROPE_AXIS_DIM = HEAD_DIM // 2
ROPE_THETA = 10000.0
BLOCK = 128
N_GROUPS = 4
ROUTED_SCALE = 1.0
NORM_EPS = 1e-6
NEG_INF = -1e30
N_MOD = 6
Q_SCALE = HEAD_DIM ** -0.5

F32 = jnp.float32
BF16 = jnp.bfloat16

VMEM_LIMIT_BYTES = 56 * 1024 * 1024


def _cparams(n_axes):
    return pltpu.CompilerParams(dimension_semantics=("arbitrary",) * n_axes,
                                vmem_limit_bytes=VMEM_LIMIT_BYTES)


def _pick_tile(total, candidates):
    for t in candidates:
        if total % t == 0:
            return t
    raise ValueError(f"no tile in {candidates} divides {total}")


def _dot(a, b):
    return jnp.dot(a, b, preferred_element_type=F32)


def _dot_nt(a, b):
    return lax.dot_general(a, b, (((1,), (1,)), ((), ())), preferred_element_type=F32)


def _sigmoid(x):
    return 1.0 / (1.0 + jnp.exp(-x))


def _adaln_kernel(cond_ref, down_ref, up_ref, bias_ref, o_ref):
    cond = cond_ref[...]
    s = (cond * _sigmoid(cond)).astype(BF16)
    t = _dot(s, down_ref[0].astype(BF16)).astype(BF16)
    o_ref[0] = _dot(t, up_ref[0].astype(BF16)) + bias_ref[0]


def _adaln(cond, down, up, bias):
    L, D, rank = down.shape
    n_out = up.shape[-1]
    rows = cond.shape[0]
    tn = _pick_tile(n_out, (2048, 1024, 512, 256, 128))
    return pl.pallas_call(
        _adaln_kernel,
        out_shape=jax.ShapeDtypeStruct((L, rows, n_out), F32),
        grid=(L, n_out // tn),
        in_specs=[
            pl.BlockSpec((rows, D), lambda l, j: (0, 0)),
            pl.BlockSpec((1, D, rank), lambda l, j: (l, 0, 0)),
            pl.BlockSpec((1, rank, tn), lambda l, j: (l, 0, j)),
            pl.BlockSpec((1, 1, tn), lambda l, j: (l, 0, j)),
        ],
        out_specs=pl.BlockSpec((1, rows, tn), lambda l, j: (l, 0, j)),
        compiler_params=_cparams(2),
        name="adaln",
    )(cond, down, up, bias.reshape(L, 1, n_out))


def _norm_mod_kernel(x_ref, gain_ref, shift_ref, scale_ref, o_ref, *, tr, n_ctx):
    x = x_ref[0]
    var = jnp.mean(x * x, axis=-1, keepdims=True)
    y = x * lax.rsqrt(var + NORM_EPS) * gain_ref[...]
    row = pl.program_id(1) * tr + lax.broadcasted_iota(jnp.int32, (tr, 1), 0)
    is_ctx = row < n_ctx
    shift = jnp.where(is_ctx, shift_ref[0, 0:1, :], shift_ref[0, 1:2, :])
    scale = jnp.where(is_ctx, scale_ref[0, 0:1, :], scale_ref[0, 1:2, :])
    o_ref[0] = (y * (1.0 + scale) + shift).astype(o_ref.dtype)


def _norm_mod(xa, gain, shift, scale, n_ctx):
    B, R, D = xa.shape
    tr = _pick_tile(R, (256, 128))
    return pl.pallas_call(
        functools.partial(_norm_mod_kernel, tr=tr, n_ctx=n_ctx),
        out_shape=jax.ShapeDtypeStruct((B, R, D), BF16),
        grid=(B, R // tr),
        in_specs=[
            pl.BlockSpec((1, tr, D), lambda b, i: (b, i, 0)),
            pl.BlockSpec((1, D), lambda b, i: (0, 0)),
            pl.BlockSpec((1, 2, D), lambda b, i: (b, 0, 0)),
            pl.BlockSpec((1, 2, D), lambda b, i: (b, 0, 0)),
        ],
        out_specs=pl.BlockSpec((1, tr, D), lambda b, i: (b, i, 0)),
        compiler_params=_cparams(2),
        name="norm_mod",
    )(xa, gain.reshape(1, D), shift, scale)


def _final_norm_kernel(x_ref, gain_ref, o_ref):
    x = x_ref[0]
    var = jnp.mean(x * x, axis=-1, keepdims=True)
    o_ref[0] = x * lax.rsqrt(var + NORM_EPS) * gain_ref[...]


def _final_norm(xa, gain, n_ctx):
    B, R, D = xa.shape
    S = R - n_ctx
    tr = _pick_tile(math.gcd(S, n_ctx), (256, 128))
    off = n_ctx // tr
    return pl.pallas_call(
        _final_norm_kernel,
        out_shape=jax.ShapeDtypeStruct((B, S, D), F32),
        grid=(B, S // tr),
        in_specs=[
            pl.BlockSpec((1, tr, D), lambda b, i: (b, i + off, 0)),
            pl.BlockSpec((1, D), lambda b, i: (0, 0)),
        ],
        out_specs=pl.BlockSpec((1, tr, D), lambda b, i: (b, i, 0)),
        compiler_params=_cparams(2),
        name="final_norm",
    )(xa, gain.reshape(1, D))


def _rope_cols(a, cos, sin_signed, lane_lo):
    rot = jnp.where(lane_lo, pltpu.roll(a, HEAD_DIM - 32, 1), pltpu.roll(a, 32, 1))
    return a * cos + rot * sin_signed


def _inproj_kernel(h_ref, w_ref, cos_ref, sin_ref, o_ref, *, tn, rope_ranges, q_range):
    j = pl.program_id(1)
    acc = _dot(h_ref[...], w_ref[...])
    roped = functools.reduce(jnp.logical_or, [(j >= lo) & (j < hi) for lo, hi in rope_ranges])

    @pl.when(roped)
    def _():
        cos = cos_ref[...]
        sin_signed = sin_ref[...]
        mult = jnp.where((j >= q_range[0]) & (j < q_range[1]), Q_SCALE, 1.0).astype(F32)
        lane = lax.broadcasted_iota(jnp.int32, (1, HEAD_DIM), 1)
        lane_lo = (lane % ROPE_AXIS_DIM) < (ROPE_AXIS_DIM // 2)
        for c in range(tn // HEAD_DIM):
            sl = slice(c * HEAD_DIM, (c + 1) * HEAD_DIM)
            o_ref[:, sl] = (_rope_cols(acc[:, sl], cos, sin_signed, lane_lo) * mult).astype(o_ref.dtype)

    @pl.when(jnp.logical_not(roped))
    def _():
        o_ref[...] = acc.astype(o_ref.dtype)


def _inproj(h2d, w, cos_tab, sin_tab, dims):
    M, D = h2d.shape
    N = w.shape[1]
    R = cos_tab.shape[0]
    tm = dims["tm"]
    tn = dims["tn_in"]
    tiles_per_batch = R // tm
    seg = dims["seg"]
    rope_ranges = tuple((seg[k][0] // tn, seg[k][1] // tn) for k in ("ka", "kb", "qa", "qb"))
    q_range = (seg["qa"][0] // tn, seg["qb"][1] // tn)
    return pl.pallas_call(
        functools.partial(_inproj_kernel, tn=tn, rope_ranges=rope_ranges, q_range=q_range),
        out_shape=jax.ShapeDtypeStruct((M, N), BF16),
        grid=(M // tm, N // tn),
        in_specs=[
            pl.BlockSpec((tm, D), lambda i, j: (i, 0)),
            pl.BlockSpec((D, tn), lambda i, j: (0, j)),
            pl.BlockSpec((tm, HEAD_DIM), lambda i, j: (i % tiles_per_batch, 0)),
            pl.BlockSpec((tm, HEAD_DIM), lambda i, j: (i % tiles_per_batch, 0)),
        ],
        out_specs=pl.BlockSpec((tm, tn), lambda i, j: (i, j)),
        compiler_params=_cparams(2),
        name="inproj_rope",
    )(h2d, w, cos_tab, sin_tab)


def _win_attn_kernel(sink_ref, q_ref, kc_ref, vc_ref, kp_ref, ko_ref, kn_ref, vp_ref, vo_ref, vn_ref,
                     o_ref, *, group, n_ctx_blocks, n_blocks):
    kvh = pl.program_id(1)
    n = pl.program_id(2)
    qi = lax.broadcasted_iota(jnp.int32, (BLOCK, BLOCK), 0)
    kj = lax.broadcasted_iota(jnp.int32, (BLOCK, BLOCK), 1)
    is_lat = n >= n_ctx_blocks
    valid_p = jnp.logical_and(is_lat, n - 1 >= n_ctx_blocks)
    valid_n = jnp.logical_and(is_lat, n + 1 < n_blocks)
    mask = jnp.concatenate([
        jnp.logical_and(kj >= qi, valid_p),
        jnp.logical_and(kj >= 0, is_lat),
        jnp.logical_and(kj <= qi, valid_n)], axis=1)
    k_band = jnp.concatenate([kp_ref[0], ko_ref[0], kn_ref[0]], axis=0)
    v_band = jnp.concatenate([vp_ref[0], vo_ref[0], vn_ref[0]], axis=0)
    kc = kc_ref[0]
    vc = vc_ref[0]
    for g in range(group):
        sl = slice(g * HEAD_DIM, (g + 1) * HEAD_DIM)
        q = q_ref[0, :, sl]
        sink = sink_ref[kvh * group + g]
        s_c = _dot_nt(q, kc)
        s_b = jnp.where(mask, _dot_nt(q, k_band), NEG_INF)
        m = jnp.maximum(jnp.maximum(jnp.max(s_c, axis=-1, keepdims=True),
                                    jnp.max(s_b, axis=-1, keepdims=True)), sink)
        e_c = jnp.exp(s_c - m)
        e_b = jnp.exp(s_b - m)
        denom = (jnp.sum(e_c, axis=-1, keepdims=True) + jnp.sum(e_b, axis=-1, keepdims=True)
                 + jnp.exp(sink - m))
        out = _dot(e_b.astype(BF16), v_band) + _dot(e_c.astype(BF16), vc)
        o_ref[0, :, sl] = (out / denom).astype(o_ref.dtype)


def _win_attn(px, sink, dims):
    B, R, _ = px.shape
    seg = dims["seg"]
    hkv, group, n_ctx = dims["hkv"], dims["group"], dims["n_ctx"]
    nb = R // BLOCK
    ncb = n_ctx // BLOCK
    qw = group * HEAD_DIM
    k0 = seg["ka"][0] // HEAD_DIM
    v0 = seg["va"][0] // HEAD_DIM
    q0 = seg["qa"][0] // qw
    assert seg["qa"][0] % qw == 0

    def band_spec(col0, shift):
        return pl.BlockSpec(
            (1, BLOCK, HEAD_DIM),
            lambda b, h, n: (b, jnp.clip(n + shift, 0, nb - 1), col0 + h))

    return pl.pallas_call(
        functools.partial(_win_attn_kernel, group=group, n_ctx_blocks=ncb, n_blocks=nb),
        out_shape=jax.ShapeDtypeStruct((B, R, hkv * qw), BF16),
        grid=(B, hkv, nb),
        in_specs=[
            pl.BlockSpec(memory_space=pltpu.SMEM),
            pl.BlockSpec((1, BLOCK, qw), lambda b, h, n: (b, n, q0 + h)),
            pl.BlockSpec((1, n_ctx, HEAD_DIM), lambda b, h, n: (b, 0, k0 + h)),
            pl.BlockSpec((1, n_ctx, HEAD_DIM), lambda b, h, n: (b, 0, v0 + h)),
            band_spec(k0, -1), band_spec(k0, 0), band_spec(k0, 1),
            band_spec(v0, -1), band_spec(v0, 0), band_spec(v0, 1),
        ],
        out_specs=pl.BlockSpec((1, BLOCK, qw), lambda b, h, n: (b, n, h)),
        compiler_params=_cparams(3),
        name="window_attn",
    )(sink, px, px, px, px, px, px, px, px, px)


def _diff_attn_kernel(lq1_ref, lk1_ref, lq2_ref, lk2_ref, gain_ref, q_ref, k_ref, v_ref, o_ref,
                      acc1_ref, acc2_ref, *, tq, tk, n_ctx, n_rows, lam_init):
    qi = pl.program_id(2)
    q1 = q_ref[0, :, :HEAD_DIM]
    q2 = q_ref[0, :, HEAD_DIM:]

    def chunk(start, size, carry):
        m1, l1, m2, l2 = carry
        k = k_ref[0, pl.ds(start, size), :]
        v = v_ref[0, pl.ds(start, size), :]
        s1 = _dot_nt(q1, k[:, :HEAD_DIM])
        s2 = _dot_nt(q2, k[:, HEAD_DIM:])
        n1 = jnp.maximum(m1, jnp.max(s1, axis=-1, keepdims=True))
        n2 = jnp.maximum(m2, jnp.max(s2, axis=-1, keepdims=True))
        a1 = jnp.exp(m1 - n1)
        a2 = jnp.exp(m2 - n2)
        p1 = jnp.exp(s1 - n1)
        p2 = jnp.exp(s2 - n2)
        l1 = a1 * l1 + jnp.sum(p1, axis=-1, keepdims=True)
        l2 = a2 * l2 + jnp.sum(p2, axis=-1, keepdims=True)
        acc1_ref[...] = a1 * acc1_ref[...] + _dot(p1.astype(BF16), v)
        acc2_ref[...] = a2 * acc2_ref[...] + _dot(p2.astype(BF16), v)
        return n1, l1, n2, l2

    acc1_ref[...] = jnp.zeros_like(acc1_ref)
    acc2_ref[...] = jnp.zeros_like(acc2_ref)
    init = (jnp.full((tq, 1), -jnp.inf, F32), jnp.zeros((tq, 1), F32),
            jnp.full((tq, 1), -jnp.inf, F32), jnp.zeros((tq, 1), F32))

    def finish(carry):
        _, l1, _, l2 = carry
        lam = (jnp.exp(jnp.sum(lq1_ref[...] * lk1_ref[...], axis=-1, keepdims=True))
               - jnp.exp(jnp.sum(lq2_ref[...] * lk2_ref[...], axis=-1, keepdims=True)) + lam_init)
        o = acc1_ref[...] / l1 - lam * (acc2_ref[...] / l2)
        var = jnp.mean(o * o, axis=-1, keepdims=True)
        y = o * lax.rsqrt(var + NORM_EPS) * gain_ref[...] * (1.0 - lam_init)
        o_ref[0] = y.astype(o_ref.dtype)

    @pl.when(qi * tq < n_ctx)
    def _():
        finish(chunk(0, n_ctx, init))

    @pl.when(qi * tq >= n_ctx)
    def _():
        carry = lax.fori_loop(
            0, n_rows // tk,
            lambda t, c: chunk(pl.multiple_of(t * tk, tk), tk, c), init)
        finish(carry)


def _diff_attn(px, lq1, lk1, lq2, lk2, gain, lam_init, dims):
    B, R, _ = px.shape
    seg = dims["seg"]
    hb, n_ctx = dims["hb"], dims["n_ctx"]
    w = 2 * HEAD_DIM
    tq = _pick_tile(math.gcd(R, n_ctx), (256, 128))
    tk = _pick_tile(R, (768, 512, 384, 256, 128))
    q0, k0, v0 = seg["qb"][0] // w, seg["kb"][0] // w, seg["vb"][0] // w
    vec = lambda a: a.reshape(1, -1)
    small = lambda n: pl.BlockSpec((1, n), lambda b, h, i: (0, 0))
    return pl.pallas_call(
        functools.partial(_diff_attn_kernel, tq=tq, tk=tk, n_ctx=n_ctx, n_rows=R, lam_init=lam_init),
        out_shape=jax.ShapeDtypeStruct((B, R, hb * w), BF16),
        grid=(B, hb, R // tq),
        in_specs=[
            small(HEAD_DIM), small(HEAD_DIM), small(HEAD_DIM), small(HEAD_DIM), small(w),
            pl.BlockSpec((1, tq, w), lambda b, h, i: (b, i, q0 + h)),
            pl.BlockSpec((1, R, w), lambda b, h, i: (b, 0, k0 + h)),
            pl.BlockSpec((1, R, w), lambda b, h, i: (b, 0, v0 + h)),
        ],
        out_specs=pl.BlockSpec((1, tq, w), lambda b, h, i: (b, i, h)),
        scratch_shapes=[pltpu.VMEM((tq, w), F32), pltpu.VMEM((tq, w), F32)],
        compiler_params=_cparams(3),
        name="diff_attn",
    )(vec(lq1), vec(lk1), vec(lq2), vec(lk2), vec(gain), px, px, px)


def _merge_kernel(ya_ref, yb_ref, wa_ref, wb_ref, ga_ref, gb_ref, o_ref):
    pa = _dot(ya_ref[...], wa_ref[...])
    pb = _dot(yb_ref[...], wb_ref[...])
    m = _sigmoid(ga_ref[...].astype(F32)) * pa + _sigmoid(gb_ref[...].astype(F32)) * pb
    o_ref[...] = m.astype(o_ref.dtype)


def _merge(ya, yb, wa, wb, px2d, dims):
    M = ya.shape[0]
    D = wa.shape[1]
    tm, tn = dims["tm"], dims["tn"]
    g0 = dims["seg"]["gate"][0] // tn
    g1 = g0 + D // tn
    return pl.pallas_call(
        _merge_kernel,
        out_shape=jax.ShapeDtypeStruct((M, D), BF16),
        grid=(M // tm, D // tn),
        in_specs=[
            pl.BlockSpec((tm, ya.shape[1]), lambda i, j: (i, 0)),
            pl.BlockSpec((tm, yb.shape[1]), lambda i, j: (i, 0)),
            pl.BlockSpec((wa.shape[0], tn), lambda i, j: (0, j)),
            pl.BlockSpec((wb.shape[0], tn), lambda i, j: (0, j)),
            pl.BlockSpec((tm, tn), lambda i, j: (i, g0 + j)),
            pl.BlockSpec((tm, tn), lambda i, j: (i, g1 + j)),
        ],
        out_specs=pl.BlockSpec((tm, tn), lambda i, j: (i, j)),
        compiler_params=_cparams(2),
        name="merge_branches",
    )(ya, yb, wa, wb, px2d, px2d)


def _row_gate(gate_ref, tile_in_batch, tm, n_ctx):
    row = tile_in_batch * tm + lax.broadcasted_iota(jnp.int32, (tm, 1), 0)
    return jnp.where(row < n_ctx, gate_ref[0, 0:1, :], gate_ref[0, 1:2, :])


def _outproj_kernel(m_ref, w_ref, x_ref, gate_ref, o_ref, *, tm, tiles_per_batch, n_ctx):
    gate = _row_gate(gate_ref, pl.program_id(0) % tiles_per_batch, tm, n_ctx)
    o_ref[...] = x_ref[...] + gate * _dot(m_ref[...], w_ref[...])


def _outproj(m, w, x2d, gate, dims):
    M, D = x2d.shape
    tm, tn = dims["tm"], dims["tn"]
    tpb = dims["rows"] // tm
    return pl.pallas_call(
        functools.partial(_outproj_kernel, tm=tm, tiles_per_batch=tpb, n_ctx=dims["n_ctx"]),
        out_shape=jax.ShapeDtypeStruct((M, D), F32),
        grid=(M // tm, D // tn),
        in_specs=[
            pl.BlockSpec((tm, D), lambda i, j: (i, 0)),
            pl.BlockSpec((D, tn), lambda i, j: (0, j)),
            pl.BlockSpec((tm, tn), lambda i, j: (i, j)),
            pl.BlockSpec((1, 2, tn), lambda i, j: (i // tpb, 0, j)),
        ],
        out_specs=pl.BlockSpec((tm, tn), lambda i, j: (i, j)),
        input_output_aliases={2: 0},
        compiler_params=_cparams(2),
        name="outproj_residual",
    )(m, w, x2d, gate)


def _router_kernel(h_ref, wr_ref, bias_ref, o_ref, *, n_experts):
    per_group = n_experts // N_GROUPS
    logits = _dot_nt(wr_ref[...], h_ref[...])
    scores = _sigmoid(logits)
    sel = scores + bias_ref[...]
    srow = [scores[e:e + 1, :] for e in range(n_experts)]
    row = [sel[e:e + 1, :] for e in range(n_experts)]
    best, gidx = None, None
    for g in range(N_GROUPS):
        mem = row[g * per_group:(g + 1) * per_group]
        gs = None
        for a in range(per_group):
            for b in range(a + 1, per_group):
                s = mem[a] + mem[b]
                gs = s if gs is None else jnp.maximum(gs, s)
        if best is None:
            best, gidx = gs, jnp.zeros_like(gs, dtype=jnp.int32)
        else:
            better = gs > best
            gidx = jnp.where(better, g, gidx)
            best = jnp.where(better, gs, best)
    masked = [jnp.where(gidx == (e // per_group), row[e], NEG_INF) for e in range(n_experts)]
    m1, i1 = masked[0], jnp.zeros_like(gidx)
    for e in range(1, n_experts):
        better = masked[e] > m1
        i1 = jnp.where(better, e, i1)
        m1 = jnp.where(better, masked[e], m1)
    m2 = jnp.full_like(m1, -jnp.inf)
    i2 = jnp.zeros_like(gidx)
    for e in range(n_experts):
        cand = jnp.where(i1 == e, -jnp.inf, masked[e])
        better = cand > m2
        i2 = jnp.where(better, e, i2)
        m2 = jnp.where(better, cand, m2)
    w1 = functools.reduce(jnp.add, [jnp.where(i1 == e, srow[e], 0.0) for e in range(n_experts)])
    w2 = functools.reduce(jnp.add, [jnp.where(i2 == e, srow[e], 0.0) for e in range(n_experts)])
    tot = w1 + w2
    w1 = w1 / tot * ROUTED_SCALE
    w2 = w2 / tot * ROUTED_SCALE
    o_ref[...] = jnp.concatenate(
        [jnp.where(i1 == e, w1, 0.0) + jnp.where(i2 == e, w2, 0.0) for e in range(n_experts)], axis=0)


def _router(h2d, wr_t, bias, dims):
    M, D = h2d.shape
    E = wr_t.shape[0]
    tm = dims["tm"]
    return pl.pallas_call(
        functools.partial(_router_kernel, n_experts=E),
        out_shape=jax.ShapeDtypeStruct((E, M), F32),
        grid=(M // tm,),
        in_specs=[
            pl.BlockSpec((tm, D), lambda i: (i, 0)),
            pl.BlockSpec((E, D), lambda i: (0, 0)),
            pl.BlockSpec((E, 1), lambda i: (0, 0)),
        ],
        out_specs=pl.BlockSpec((E, tm), lambda i: (0, i)),
        compiler_params=_cparams(1),
        name="router",
    )(h2d, wr_t, bias.reshape(E, 1).astype(F32))


def _moe_kernel(h_ref, comb_ref, wg_ref, wu_ref, wd_ref, o_ref):
    e = pl.program_id(1)
    h = h_ref[...]
    g = _dot(h, wg_ref[0])
    u = _dot(h, wu_ref[0])
    comb = comb_ref[...]
    lane = lax.broadcasted_iota(jnp.int32, comb.shape, 1)
    c = jnp.sum(jnp.where(lane == e, comb, 0.0), axis=-1, keepdims=True)
    a = (g * _sigmoid(g) * u * c).astype(BF16)
    y = _dot(a, wd_ref[0])

    @pl.when(e == 0)
    def _():
        o_ref[...] = y

    @pl.when(e > 0)
    def _():
        o_ref[...] += y


def _moe(h2d, comb, wg, wu, wd, dims):
    M, D = h2d.shape
    E, _, F = wg.shape
    tm = dims["tm_moe"]
    return pl.pallas_call(
        _moe_kernel,
        out_shape=jax.ShapeDtypeStruct((M, D), F32),
        grid=(M // tm, E),
        in_specs=[
            pl.BlockSpec((tm, D), lambda i, e: (i, 0)),
            pl.BlockSpec((tm, E), lambda i, e: (i, 0)),
            pl.BlockSpec((1, D, F), lambda i, e: (e, 0, 0)),
            pl.BlockSpec((1, D, F), lambda i, e: (e, 0, 0)),
            pl.BlockSpec((1, F, D), lambda i, e: (e, 0, 0)),
        ],
        out_specs=pl.BlockSpec((tm, D), lambda i, e: (i, 0)),
        compiler_params=_cparams(2),
        name="moe_experts",
    )(h2d, comb, wg, wu, wd)


def _residual_kernel(x_ref, y_ref, gate_ref, o_ref, *, tr, n_ctx):
    gate = _row_gate(gate_ref, pl.program_id(1), tr, n_ctx)
    o_ref[0] = x_ref[0] + gate * y_ref[0]


def _residual(xa, y, gate, n_ctx):
    B, R, D = xa.shape
    tr = _pick_tile(R, (256, 128))
    blk = pl.BlockSpec((1, tr, D), lambda b, i: (b, i, 0))
    return pl.pallas_call(
        functools.partial(_residual_kernel, tr=tr, n_ctx=n_ctx),
        out_shape=jax.ShapeDtypeStruct((B, R, D), F32),
        grid=(B, R // tr),
        in_specs=[blk, blk, pl.BlockSpec((1, 2, D), lambda b, i: (b, 0, 0))],
        out_specs=blk,
        input_output_aliases={0: 0},
        compiler_params=_cparams(2),
        name="gated_residual",
    )(xa, y, gate)


def _rope_tables(seq, n_ctx):
    rows = seq // GRID_W
    row = jnp.repeat(jnp.arange(rows), GRID_W).astype(F32)
    col = jnp.tile(jnp.arange(GRID_W), rows).astype(F32)
    inv = ROPE_THETA ** (-(jnp.arange(0, ROPE_AXIS_DIM, 2, dtype=F32) / ROPE_AXIS_DIM))
    ang_r = row[:, None] * inv
    ang_c = col[:, None] * inv
    ang = jnp.concatenate([ang_r, ang_r, ang_c, ang_c], axis=-1)
    lane = jnp.arange(HEAD_DIM)
    sign = jnp.where((lane % ROPE_AXIS_DIM) < (ROPE_AXIS_DIM // 2), -1.0, 1.0).astype(F32)
    cos = jnp.concatenate([jnp.ones((n_ctx, HEAD_DIM), F32), jnp.cos(ang)], axis=0)
    sin = jnp.concatenate([jnp.zeros((n_ctx, HEAD_DIM), F32), jnp.sin(ang) * sign], axis=0)
    return cos, sin


def _dims(D, S, C, d_expert):
    ha = D // 256
    hkv = ha // 4
    hb = D // 512
    widths = [("ka", hkv * HEAD_DIM), ("va", hkv * HEAD_DIM), ("kb", hb * 2 * HEAD_DIM),
              ("vb", hb * 2 * HEAD_DIM), ("qa", ha * HEAD_DIM), ("qb", hb * 2 * HEAD_DIM),
              ("gate", 2 * D)]
    seg, off = {}, 0
    for name, wdt in widths:
        seg[name] = (off, off + wdt)
        off += wdt
    R = C + S
    tm = _pick_tile(R, (768, 512, 384, 256, 128))
    tn = _pick_tile(D, (512, 256, 128))
    tn_in = _pick_tile(hkv * HEAD_DIM, (512, 256, 128))
    tm_moe = _pick_tile(R, (384, 256, 128))
    return dict(ha=ha, hkv=hkv, group=ha // hkv, hb=hb, seg=seg, n_cols=off, rows=R, n_ctx=C,
                tm=tm, tn=tn, tn_in=tn_in, tm_moe=tm_moe)


def kernel(x, c, ctx, c_ctx, ada_down, ada_up, ada_bias, norm_mix, norm_ffn, w_in, sink_logit,
           lam_q1, lam_k1, lam_q2, lam_k2, subln_gain, w_branch_a, w_branch_b, w_out,
           w_router, b_router, w_exp_gate, w_exp_up, w_exp_down, norm_final):
    B, S, D = x.shape
    C = ctx.shape[1]
    depth = w_in.shape[0]
    dims = _dims(D, S, C, w_exp_gate.shape[-1])
    R = dims["rows"]
    M = B * R

    cond = jnp.concatenate([c, c_ctx[None], jnp.zeros((8 - (B + 1) % 8 if (B + 1) % 8 else 0, D), F32)])
    mods = _adaln(cond, ada_down, ada_up, ada_bias)
    mods = mods.reshape(depth, mods.shape[1], N_MOD, D)
    lat = mods[:, :B]
    cx = jnp.broadcast_to(mods[:, B:B + 1], lat.shape)
    mod = jnp.stack([cx, lat], axis=2)

    cos_tab, sin_tab = _rope_tables(S, C)
    wr_t = w_router.T.astype(BF16)
    xa = jnp.concatenate([ctx, x], axis=1)

    for l in range(depth):
        lam_init = 0.8 - 0.6 * math.exp(-0.3 * l)
        m = [mod[l, :, :, k] for k in range(N_MOD)]
        h = _norm_mod(xa, norm_mix[l], m[0], m[1], C)
        px = _inproj(h.reshape(M, D), w_in[l].astype(BF16), cos_tab, sin_tab, dims)
        px3 = px.reshape(B, R, -1)
        ya = _win_attn(px3, sink_logit[l], dims)
        yb = _diff_attn(px3, lam_q1[l], lam_k1[l], lam_q2[l], lam_k2[l], subln_gain[l], lam_init, dims)
        mg = _merge(ya.reshape(M, -1), yb.reshape(M, -1), w_branch_a[l].astype(BF16),
                    w_branch_b[l].astype(BF16), px, dims)
        x2d = _outproj(mg, w_out[l].astype(BF16), xa.reshape(M, D), m[2], dims)
        xa = x2d.reshape(B, R, D)
        h = _norm_mod(xa, norm_ffn[l], m[3], m[4], C).reshape(M, D)
        comb = _router(h, wr_t, b_router, dims).T
        y = _moe(h, comb, w_exp_gate[l].astype(BF16), w_exp_up[l].astype(BF16),
                 w_exp_down[l].astype(BF16), dims)
        xa = _residual(xa, y.reshape(B, R, D), m[5], C)
    return _final_norm(xa, norm_final, C)
```

```python
import functools
import math

import jax
import jax.numpy as jnp
from jax import lax
from jax.experimental import pallas as pl
from jax.experimental.pallas import tpu as pltpu

HEAD_DIM = 128
GRID_W = 64
ROPE_AXIS_DIM = HEAD_DIM // 2
ROPE_THETA = 10000.0
BLOCK = 128
N_GROUPS = 4
ROUTED_SCALE = 1.0
NORM_EPS = 1e-6
NEG_INF = -1e30
N_MOD = 6
Q_SCALE = HEAD_DIM ** -0.5
LOG2E = math.log2(math.e)

F32 = jnp.float32
BF16 = jnp.bfloat16

VMEM_LIMIT_BYTES = 56 * 1024 * 1024


def _cparams(n_axes):
    return pltpu.CompilerParams(dimension_semantics=("arbitrary",) * n_axes,
                                vmem_limit_bytes=VMEM_LIMIT_BYTES)


def _pick_tile(total, candidates):
    for t in candidates:
        if total % t == 0:
            return t
    raise ValueError(f"no tile in {candidates} divides {total}")


def _dot(a, b):
    return jnp.dot(a, b, preferred_element_type=F32)


def _dot_nt(a, b):
    return lax.dot_general(a, b, (((1,), (1,)), ((), ())), preferred_element_type=F32)


def _sigmoid(x):
    return 1.0 / (1.0 + jnp.exp(-x))


def _adaln_kernel(cond_ref, down_ref, up_ref, bias_ref, o_ref):
    cond = cond_ref[...]
    s = (cond * _sigmoid(cond)).astype(BF16)
    t = _dot(s, down_ref[0].astype(BF16)).astype(BF16)
    o_ref[0] = _dot(t, up_ref[0].astype(BF16)) + bias_ref[0]


def _adaln(cond, down, up, bias):
    L, D, rank = down.shape
    n_out = up.shape[-1]
    rows = cond.shape[0]
    tn = _pick_tile(n_out, (2048, 1024, 512, 256, 128))
    return pl.pallas_call(
        _adaln_kernel,
        out_shape=jax.ShapeDtypeStruct((L, rows, n_out), F32),
        grid=(L, n_out // tn),
        in_specs=[
            pl.BlockSpec((rows, D), lambda l, j: (0, 0)),
            pl.BlockSpec((1, D, rank), lambda l, j: (l, 0, 0)),
            pl.BlockSpec((1, rank, tn), lambda l, j: (l, 0, j)),
            pl.BlockSpec((1, 1, tn), lambda l, j: (l, 0, j)),
        ],
        out_specs=pl.BlockSpec((1, rows, tn), lambda l, j: (l, 0, j)),
        compiler_params=_cparams(2),
        name="adaln",
    )(cond, down, up, bias.reshape(L, 1, n_out))


def _norm_mod_kernel(x_ref, gain_ref, shift_ref, scale_ref, o_ref, *, tr, n_ctx):
    x = x_ref[0]
    var = jnp.mean(x * x, axis=-1, keepdims=True)
    y = x * lax.rsqrt(var + NORM_EPS) * gain_ref[...]
    row = pl.program_id(1) * tr + lax.broadcasted_iota(jnp.int32, (tr, 1), 0)
    is_ctx = row < n_ctx
    shift = jnp.where(is_ctx, shift_ref[0, 0:1, :], shift_ref[0, 1:2, :])
    scale = jnp.where(is_ctx, scale_ref[0, 0:1, :], scale_ref[0, 1:2, :])
    o_ref[0] = (y * (1.0 + scale) + shift).astype(o_ref.dtype)


def _norm_mod(xa, gain, shift, scale, n_ctx):
    B, R, D = xa.shape
    tr = _pick_tile(R, (256, 128))
    return pl.pallas_call(
        functools.partial(_norm_mod_kernel, tr=tr, n_ctx=n_ctx),
        out_shape=jax.ShapeDtypeStruct((B, R, D), BF16),
        grid=(B, R // tr),
        in_specs=[
            pl.BlockSpec((1, tr, D), lambda b, i: (b, i, 0)),
            pl.BlockSpec((1, D), lambda b, i: (0, 0)),
            pl.BlockSpec((1, 2, D), lambda b, i: (b, 0, 0)),
            pl.BlockSpec((1, 2, D), lambda b, i: (b, 0, 0)),
        ],
        out_specs=pl.BlockSpec((1, tr, D), lambda b, i: (b, i, 0)),
        compiler_params=_cparams(2),
        name="norm_mod",
    )(xa, gain.reshape(1, D), shift, scale)


def _final_norm_kernel(x_ref, gain_ref, o_ref):
    x = x_ref[0]
    var = jnp.mean(x * x, axis=-1, keepdims=True)
    o_ref[0] = x * lax.rsqrt(var + NORM_EPS) * gain_ref[...]


def _final_norm(xa, gain, n_ctx):
    B, R, D = xa.shape
    S = R - n_ctx
    tr = _pick_tile(math.gcd(S, n_ctx), (256, 128))
    off = n_ctx // tr
    return pl.pallas_call(
        _final_norm_kernel,
        out_shape=jax.ShapeDtypeStruct((B, S, D), F32),
        grid=(B, S // tr),
        in_specs=[
            pl.BlockSpec((1, tr, D), lambda b, i: (b, i + off, 0)),
            pl.BlockSpec((1, D), lambda b, i: (0, 0)),
        ],
        out_specs=pl.BlockSpec((1, tr, D), lambda b, i: (b, i, 0)),
        compiler_params=_cparams(2),
        name="final_norm",
    )(xa, gain.reshape(1, D))


def _rope_cols(a, cos, sin_signed, lane_lo):
    rot = jnp.where(lane_lo, pltpu.roll(a, HEAD_DIM - 32, 1), pltpu.roll(a, 32, 1))
    return a * cos + rot * sin_signed


def _inproj_kernel(h_ref, w_ref, cos_ref, sin_ref, o_ref, *, tn, rope_ranges, qa_range, qb_range):
    j = pl.program_id(1)
    acc = _dot(h_ref[...], w_ref[0])
    roped = functools.reduce(jnp.logical_or, [(j >= lo) & (j < hi) for lo, hi in rope_ranges])

    @pl.when(roped)
    def _():
        cos = cos_ref[...]
        sin_signed = sin_ref[...]
        mult = jnp.where((j >= qa_range[0]) & (j < qa_range[1]), Q_SCALE,
                         jnp.where((j >= qb_range[0]) & (j < qb_range[1]), Q_SCALE * LOG2E, 1.0)
                         ).astype(F32)
        lane = lax.broadcasted_iota(jnp.int32, (1, HEAD_DIM), 1)
        lane_lo = (lane % ROPE_AXIS_DIM) < (ROPE_AXIS_DIM // 2)
        for c in range(tn // HEAD_DIM):
            sl = slice(c * HEAD_DIM, (c + 1) * HEAD_DIM)
            o_ref[:, sl] = (_rope_cols(acc[:, sl], cos, sin_signed, lane_lo) * mult).astype(o_ref.dtype)

    @pl.when(jnp.logical_not(roped))
    def _():
        o_ref[...] = acc.astype(o_ref.dtype)


def _inproj(h2d, w_all, layer, cos_tab, sin_tab, dims):
    M, D = h2d.shape
    N = w_all.shape[2]
    R = cos_tab.shape[0]
    tm = dims["tm_in"]
    tn = dims["tn_in"]
    tiles_per_batch = R // tm
    seg = dims["seg"]
    rng = lambda k: (seg[k][0] // tn, seg[k][1] // tn)
    rope_ranges = tuple(rng(k) for k in ("ka", "kb", "qa", "qb"))
    return pl.pallas_call(
        functools.partial(_inproj_kernel, tn=tn, rope_ranges=rope_ranges,
                          qa_range=rng("qa"), qb_range=rng("qb")),
        out_shape=jax.ShapeDtypeStruct((M, N), BF16),
        grid=(M // tm, N // tn),
        in_specs=[
            pl.BlockSpec((tm, D), lambda i, j: (i, 0)),
            pl.BlockSpec((1, D, tn), lambda i, j: (layer, 0, j)),
            pl.BlockSpec((tm, HEAD_DIM), lambda i, j: (i % tiles_per_batch, 0)),
            pl.BlockSpec((tm, HEAD_DIM), lambda i, j: (i % tiles_per_batch, 0)),
        ],
        out_specs=pl.BlockSpec((tm, tn), lambda i, j: (i, j)),
        compiler_params=_cparams(2),
        name="inproj_rope",
    )(h2d, w_all, cos_tab, sin_tab)


def _win_attn_kernel(sink_ref, q_ref, kc_ref, vc_ref, kp_ref, ko_ref, kn_ref, vp_ref, vo_ref, vn_ref,
                     o_ref, *, group, n_ctx_blocks, n_blocks):
    kvh = pl.program_id(1)
    n = pl.program_id(2)
    qi = lax.broadcasted_iota(jnp.int32, (BLOCK, BLOCK), 0)
    kj = lax.broadcasted_iota(jnp.int32, (BLOCK, BLOCK), 1)
    is_lat = n >= n_ctx_blocks
    valid_p = jnp.logical_and(is_lat, n - 1 >= n_ctx_blocks)
    valid_n = jnp.logical_and(is_lat, n + 1 < n_blocks)
    mask = jnp.concatenate([
        jnp.logical_and(kj >= qi, valid_p),
        jnp.logical_and(kj >= 0, is_lat),
        jnp.logical_and(kj <= qi, valid_n)], axis=1)
    k_band = jnp.concatenate([kp_ref[0], ko_ref[0], kn_ref[0]], axis=0)
    v_band = jnp.concatenate([vp_ref[0], vo_ref[0], vn_ref[0]], axis=0)
    kc = kc_ref[0]
    vc = vc_ref[0]
    for g in range(group):
        sl = slice(g * HEAD_DIM, (g + 1) * HEAD_DIM)
        q = q_ref[0, :, sl]
        sink = sink_ref[kvh * group + g]
        s_c = _dot_nt(q, kc)
        s_b = jnp.where(mask, _dot_nt(q, k_band), NEG_INF)
        m = jnp.maximum(jnp.maximum(jnp.max(s_c, axis=-1, keepdims=True),
                                    jnp.max(s_b, axis=-1, keepdims=True)), sink)
        e_c = jnp.exp(s_c - m)
        e_b = jnp.exp(s_b - m)
        denom = (jnp.sum(e_c, axis=-1, keepdims=True) + jnp.sum(e_b, axis=-1, keepdims=True)
                 + jnp.exp(sink - m))
        out = _dot(e_b.astype(BF16), v_band) + _dot(e_c.astype(BF16), vc)
        o_ref[0, :, sl] = (out / denom).astype(o_ref.dtype)


def _win_attn(px, sink, dims):
    B, R, _ = px.shape
    seg = dims["seg"]
    hkv, group, n_ctx = dims["hkv"], dims["group"], dims["n_ctx"]
    nb = R // BLOCK
    ncb = n_ctx // BLOCK
    qw = group * HEAD_DIM
    k0 = seg["ka"][0] // HEAD_DIM
    v0 = seg["va"][0] // HEAD_DIM
    q0 = seg["qa"][0] // qw
    assert seg["qa"][0] % qw == 0

    def band_spec(col0, shift):
        return pl.BlockSpec(
            (1, BLOCK, HEAD_DIM),
            lambda b, h, n: (b, jnp.clip(n + shift, 0, nb - 1), col0 + h))

    return pl.pallas_call(
        functools.partial(_win_attn_kernel, group=group, n_ctx_blocks=ncb, n_blocks=nb),
        out_shape=jax.ShapeDtypeStruct((B, R, hkv * qw), BF16),
        grid=(B, hkv, nb),
        in_specs=[
            pl.BlockSpec(memory_space=pltpu.SMEM),
            pl.BlockSpec((1, BLOCK, qw), lambda b, h, n: (b, n, q0 + h)),
            pl.BlockSpec((1, n_ctx, HEAD_DIM), lambda b, h, n: (b, 0, k0 + h)),
            pl.BlockSpec((1, n_ctx, HEAD_DIM), lambda b, h, n: (b, 0, v0 + h)),
            band_spec(k0, -1), band_spec(k0, 0), band_spec(k0, 1),
            band_spec(v0, -1), band_spec(v0, 0), band_spec(v0, 1),
        ],
        out_specs=pl.BlockSpec((1, BLOCK, qw), lambda b, h, n: (b, n, h)),
        compiler_params=_cparams(3),
        name="window_attn",
    )(sink, px, px, px, px, px, px, px, px, px)


def _diff_attn_kernel(lq1_ref, lk1_ref, lq2_ref, lk2_ref, gain_ref, q_ref, k_ref, v_ref, o_ref,
                      acc1_ref, acc2_ref, *, tq, tk, n_ctx, n_rows, lam_init):
    lam = (jnp.exp(jnp.sum(lq1_ref[...] * lk1_ref[...], axis=-1, keepdims=True))
           - jnp.exp(jnp.sum(lq2_ref[...] * lk2_ref[...], axis=-1, keepdims=True)) + lam_init)
    gain = gain_ref[...] * (1.0 - lam_init)

    def attend(q_start, nq, kv_len, chunk):
        q1 = q_ref[0, pl.ds(q_start, nq), :HEAD_DIM]
        q2 = q_ref[0, pl.ds(q_start, nq), HEAD_DIM:]
        a1_ref = acc1_ref.at[pl.ds(0, nq), :]
        a2_ref = acc2_ref.at[pl.ds(0, nq), :]

        def step(start, carry):
            m1, l1, m2, l2 = carry
            k = k_ref[0, pl.ds(start, chunk), :]
            v = v_ref[0, pl.ds(start, chunk), :]
            s1 = _dot_nt(q1, k[:, :HEAD_DIM])
            s2 = _dot_nt(q2, k[:, HEAD_DIM:])
            n1 = jnp.maximum(m1, jnp.max(s1, axis=-1, keepdims=True))
            n2 = jnp.maximum(m2, jnp.max(s2, axis=-1, keepdims=True))
            a1 = jnp.exp2(m1 - n1)
            a2 = jnp.exp2(m2 - n2)
            p1 = jnp.exp2(s1 - n1)
            p2 = jnp.exp2(s2 - n2)
            l1 = a1 * l1 + jnp.sum(p1, axis=-1, keepdims=True)
            l2 = a2 * l2 + jnp.sum(p2, axis=-1, keepdims=True)
            a1_ref[...] = a1 * a1_ref[...] + _dot(p1.astype(BF16), v)
            a2_ref[...] = a2 * a2_ref[...] + _dot(p2.astype(BF16), v)
            return n1, l1, n2, l2

        a1_ref[...] = jnp.zeros((nq, 2 * HEAD_DIM), F32)
        a2_ref[...] = jnp.zeros((nq, 2 * HEAD_DIM), F32)
        carry = (jnp.full((nq, 1), -jnp.inf, F32), jnp.zeros((nq, 1), F32),
                 jnp.full((nq, 1), -jnp.inf, F32), jnp.zeros((nq, 1), F32))
        if kv_len == chunk:
            carry = step(0, carry)
        else:
            carry = lax.fori_loop(
                0, kv_len // chunk, lambda t, c: step(pl.multiple_of(t * chunk, chunk), c), carry)
        _, l1, _, l2 = carry
        o = a1_ref[...] / l1 - lam * (a2_ref[...] / l2)
        var = jnp.mean(o * o, axis=-1, keepdims=True)
        y = o * lax.rsqrt(var + NORM_EPS) * gain
        o_ref[0, pl.ds(q_start, nq), :] = y.astype(o_ref.dtype)

    attend(0, n_ctx, n_ctx, n_ctx)
    align = math.gcd(n_ctx, tq)

    def latent_tile(i, _):
        attend(pl.multiple_of(n_ctx + i * tq, align), tq, n_rows, tk)
        return 0

    lax.fori_loop(0, (n_rows - n_ctx) // tq, latent_tile, 0)


def _diff_attn(px, lq1, lk1, lq2, lk2, gain, lam_init, dims):
    B, R, _ = px.shape
    seg = dims["seg"]
    hb, n_ctx = dims["hb"], dims["n_ctx"]
    w = 2 * HEAD_DIM
    tq = _pick_tile(R - n_ctx, (1024, 512, 256, 128))
    tk = _pick_tile(R, (768, 512, 384, 256, 128))
    assert n_ctx <= tq and n_ctx % 16 == 0
    q0, k0, v0 = seg["qb"][0] // w, seg["kb"][0] // w, seg["vb"][0] // w
    vec = lambda a: a.reshape(1, -1)
    small = lambda n: pl.BlockSpec((1, n), lambda b, h: (0, 0))
    return pl.pallas_call(
        functools.partial(_diff_attn_kernel, tq=tq, tk=tk, n_ctx=n_ctx, n_rows=R, lam_init=lam_init),
        out_shape=jax.ShapeDtypeStruct((B, R, hb * w), BF16),
        grid=(B, hb),
        in_specs=[
            small(HEAD_DIM), small(HEAD_DIM), small(HEAD_DIM), small(HEAD_DIM), small(w),
            pl.BlockSpec((1, R, w), lambda b, h: (b, 0, q0 + h)),
            pl.BlockSpec((1, R, w), lambda b, h: (b, 0, k0 + h)),
            pl.BlockSpec((1, R, w), lambda b, h: (b, 0, v0 + h)),
        ],
        out_specs=pl.BlockSpec((1, R, w), lambda b, h: (b, 0, h)),
        scratch_shapes=[pltpu.VMEM((tq, w), F32), pltpu.VMEM((tq, w), F32)],
        compiler_params=_cparams(2),
        name="diff_attn",
    )(vec(lq1), vec(lk1), vec(lq2), vec(lk2), vec(gain), px, px, px)


def _merge_kernel(ya_ref, yb_ref, wa_ref, wb_ref, ga_ref, gb_ref, o_ref):
    pa = _dot(ya_ref[...], wa_ref[0])
    pb = _dot(yb_ref[...], wb_ref[0])
    m = _sigmoid(ga_ref[...].astype(F32)) * pa + _sigmoid(gb_ref[...].astype(F32)) * pb
    o_ref[...] = m.astype(o_ref.dtype)


def _merge(ya, yb, wa, wb, layer, px2d, dims):
    M = ya.shape[0]
    D = wa.shape[2]
    tm, tn = dims["tm"], dims["tn"]
    g0 = dims["seg"]["gate"][0] // tn
    g1 = g0 + D // tn
    return pl.pallas_call(
        _merge_kernel,
        out_shape=jax.ShapeDtypeStruct((M, D), BF16),
        grid=(M // tm, D // tn),
        in_specs=[
            pl.BlockSpec((tm, ya.shape[1]), lambda i, j: (i, 0)),
            pl.BlockSpec((tm, yb.shape[1]), lambda i, j: (i, 0)),
            pl.BlockSpec((1, wa.shape[1], tn), lambda i, j: (layer, 0, j)),
            pl.BlockSpec((1, wb.shape[1], tn), lambda i, j: (layer, 0, j)),
            pl.BlockSpec((tm, tn), lambda i, j: (i, g0 + j)),
            pl.BlockSpec((tm, tn), lambda i, j: (i, g1 + j)),
        ],
        out_specs=pl.BlockSpec((tm, tn), lambda i, j: (i, j)),
        compiler_params=_cparams(2),
        name="merge_branches",
    )(ya, yb, wa, wb, px2d, px2d)


def _row_gate(gate_ref, tile_in_batch, tm, n_ctx):
    row = tile_in_batch * tm + lax.broadcasted_iota(jnp.int32, (tm, 1), 0)
    return jnp.where(row < n_ctx, gate_ref[0, 0:1, :], gate_ref[0, 1:2, :])


def _outproj_kernel(m_ref, w_ref, x_ref, gate_ref, o_ref, *, tm, tiles_per_batch, n_ctx):
    gate = _row_gate(gate_ref, pl.program_id(0) % tiles_per_batch, tm, n_ctx)
    o_ref[...] = x_ref[...] + gate * _dot(m_ref[...], w_ref[0])


def _outproj(m, w, layer, x2d, gate, dims):
    M, D = x2d.shape
    tm, tn = dims["tm"], dims["tn"]
    tpb = dims["rows"] // tm
    return pl.pallas_call(
        functools.partial(_outproj_kernel, tm=tm, tiles_per_batch=tpb, n_ctx=dims["n_ctx"]),
        out_shape=jax.ShapeDtypeStruct((M, D), F32),
        grid=(M // tm, D // tn),
        in_specs=[
            pl.BlockSpec((tm, D), lambda i, j: (i, 0)),
            pl.BlockSpec((1, D, tn), lambda i, j: (layer, 0, j)),
            pl.BlockSpec((tm, tn), lambda i, j: (i, j)),
            pl.BlockSpec((1, 2, tn), lambda i, j: (i // tpb, 0, j)),
        ],
        out_specs=pl.BlockSpec((tm, tn), lambda i, j: (i, j)),
        input_output_aliases={2: 0},
        compiler_params=_cparams(2),
        name="outproj_residual",
    )(m, w, x2d, gate)


def _route(logits, bias, n_experts):
    per_group = n_experts // N_GROUPS
    scores = _sigmoid(logits)
    sel = scores + bias
    srow = [scores[e:e + 1, :] for e in range(n_experts)]
    row = [sel[e:e + 1, :] for e in range(n_experts)]
    best, gidx = None, None
    for g in range(N_GROUPS):
        mem = row[g * per_group:(g + 1) * per_group]
        gs = None
        for a in range(per_group):
            for b in range(a + 1, per_group):
                s = mem[a] + mem[b]
                gs = s if gs is None else jnp.maximum(gs, s)
        if best is None:
            best, gidx = gs, jnp.zeros_like(gs, dtype=jnp.int32)
        else:
            better = gs > best
            gidx = jnp.where(better, g, gidx)
            best = jnp.where(better, gs, best)
    masked = [jnp.where(gidx == (e // per_group), row[e], NEG_INF) for e in range(n_experts)]
    m1, i1 = masked[0], jnp.zeros_like(gidx)
    for e in range(1, n_experts):
        better = masked[e] > m1
        i1 = jnp.where(better, e, i1)
        m1 = jnp.where(better, masked[e], m1)
    m2 = jnp.full_like(m1, -jnp.inf)
    i2 = jnp.zeros_like(gidx)
    for e in range(n_experts):
        cand = jnp.where(i1 == e, -jnp.inf, masked[e])
        better = cand > m2
        i2 = jnp.where(better, e, i2)
        m2 = jnp.where(better, cand, m2)
    w1 = functools.reduce(jnp.add, [jnp.where(i1 == e, srow[e], 0.0) for e in range(n_experts)])
    w2 = functools.reduce(jnp.add, [jnp.where(i2 == e, srow[e], 0.0) for e in range(n_experts)])
    tot = w1 + w2
    return i1, i2, w1 / tot * ROUTED_SCALE, w2 / tot * ROUTED_SCALE


META_ROWS = 8


def _norm_router_kernel(x_ref, gain_ref, shift_ref, scale_ref, wr_ref, bias_ref,
                        h_ref, meta_ref, cnt_ref, carry_ref, *, tr, n_ctx, n_experts):
    first = jnp.logical_and(pl.program_id(0) == 0, pl.program_id(1) == 0)

    @pl.when(first)
    def _():
        carry_ref[...] = jnp.zeros_like(carry_ref)

    x = x_ref[0]
    var = jnp.mean(x * x, axis=-1, keepdims=True)
    y = x * lax.rsqrt(var + NORM_EPS) * gain_ref[...]
    row = pl.program_id(1) * tr + lax.broadcasted_iota(jnp.int32, (tr, 1), 0)
    is_ctx = row < n_ctx
    shift = jnp.where(is_ctx, shift_ref[0, 0:1, :], shift_ref[0, 1:2, :])
    scale = jnp.where(is_ctx, scale_ref[0, 0:1, :], scale_ref[0, 1:2, :])
    h = y * (1.0 + scale) + shift
    h_ref[0] = h

    logits = _dot_nt(wr_ref[...], h.astype(BF16))
    i1, i2, w1, w2 = _route(logits, bias_ref[...], n_experts)
    erow = lax.broadcasted_iota(jnp.int32, (n_experts, tr), 0)
    hit1 = erow == i1
    hit2 = erow == i2
    onehot = jnp.where(jnp.logical_or(hit1, hit2), 1.0, 0.0)
    before = (lax.broadcasted_iota(jnp.int32, (tr, tr), 0)
              < lax.broadcasted_iota(jnp.int32, (tr, tr), 1))
    prefix = _dot(onehot.astype(BF16), jnp.where(before, 1.0, 0.0).astype(BF16))
    seen = prefix + carry_ref[:, 0:1]
    r1 = jnp.sum(jnp.where(hit1, seen, 0.0), axis=0, keepdims=True)
    r2 = jnp.sum(jnp.where(hit2, seen, 0.0), axis=0, keepdims=True)
    carry_ref[...] = carry_ref[...] + jnp.sum(onehot, axis=1, keepdims=True)
    zero = jnp.zeros_like(w1)
    meta_ref[...] = jnp.concatenate(
        [i1.astype(F32), i2.astype(F32), w1, w2, r1, r2, zero, zero], axis=0)
    cnt_ref[...] = carry_ref[...]


def _norm_router(xa, gain, shift, scale, wr_t, bias, n_ctx):
    B, R, D = xa.shape
    E = wr_t.shape[0]
    tr = _pick_tile(R, (256, 128))
    nt = R // tr
    return pl.pallas_call(
        functools.partial(_norm_router_kernel, tr=tr, n_ctx=n_ctx, n_experts=E),
        out_shape=(jax.ShapeDtypeStruct((B, R, D), F32),
                   jax.ShapeDtypeStruct((META_ROWS, B * R), F32),
                   jax.ShapeDtypeStruct((E, HEAD_DIM), F32)),
        grid=(B, nt),
        in_specs=[
            pl.BlockSpec((1, tr, D), lambda b, i: (b, i, 0)),
            pl.BlockSpec((1, D), lambda b, i: (0, 0)),
            pl.BlockSpec((1, 2, D), lambda b, i: (b, 0, 0)),
            pl.BlockSpec((1, 2, D), lambda b, i: (b, 0, 0)),
            pl.BlockSpec((E, D), lambda b, i: (0, 0)),
            pl.BlockSpec((E, 1), lambda b, i: (0, 0)),
        ],
        out_specs=(pl.BlockSpec((1, tr, D), lambda b, i: (b, i, 0)),
                   pl.BlockSpec((META_ROWS, tr), lambda b, i: (0, b * nt + i)),
                   pl.BlockSpec((E, HEAD_DIM), lambda b, i: (0, 0))),
        scratch_shapes=[pltpu.VMEM((E, HEAD_DIM), F32)],
        compiler_params=_cparams(2),
        name="norm_router",
    )(xa, gain.reshape(1, D), shift, scale, wr_t, bias.reshape(E, 1).astype(F32))


def _row_gather(idx_vmem_ref, idx_smem, isem, src_hbm, buf, sem, slot, n_rows):
    cp = pltpu.make_async_copy(idx_vmem_ref.at[0, 0], idx_smem.at[slot], isem)
    cp.start()
    cp.wait()

    def body(r, carry):
        t = idx_smem[slot, r]
        pltpu.make_async_copy(src_hbm.at[pl.ds(t, 1), :], buf.at[slot, pl.ds(r, 1), :],
                              sem.at[slot]).start()
        return carry

    lax.fori_loop(0, n_rows, body, 0, unroll=8)


def _row_gather_wait(src_hbm, buf, sem, slot, n_rows):
    pltpu.make_async_copy(src_hbm.at[pl.ds(0, n_rows), :], buf.at[slot], sem.at[slot]).wait()


def _expert_kernel(te_ref, nv_ref, tok_ref, tok_next_ref, w_ref, h_hbm, wg_ref, wu_ref, wd_ref,
                   o_ref, buf, idx_smem, sem, isem, *, tm):
    j = pl.program_id(0)
    n_valid = nv_ref[0]
    slot = j % 2

    @pl.when(j == 0)
    def _():
        _row_gather(tok_ref, idx_smem, isem, h_hbm, buf, sem, 0, tm)

    @pl.when(j + 1 < n_valid)
    def _():
        _row_gather(tok_next_ref, idx_smem, isem, h_hbm, buf, sem, 1 - slot, tm)

    @pl.when(j < n_valid)
    def _():
        _row_gather_wait(h_hbm, buf, sem, slot, tm)
        h = buf[slot].astype(BF16)
        g = _dot(h, wg_ref[0, 0])
        u = _dot(h, wu_ref[0, 0])
        a = (g * _sigmoid(g) * u * w_ref[...]).astype(BF16)
        o_ref[...] = _dot(a, wd_ref[0, 0])

    @pl.when(j >= n_valid)
    def _():
        o_ref[...] = jnp.zeros_like(o_ref)


def _experts(h2d, tok3, w_sorted, tile_expert, n_valid, wg, wu, wd, layer, tm):
    M, D = h2d.shape
    F = wg.shape[-1]
    n_tiles = tok3.shape[0]
    P = n_tiles * tm
    grid_spec = pltpu.PrefetchScalarGridSpec(
        num_scalar_prefetch=2,
        grid=(n_tiles,),
        in_specs=[
            pl.BlockSpec((1, 1, tm), lambda j, te, nv: (j, 0, 0)),
            pl.BlockSpec((1, 1, tm), lambda j, te, nv: (jnp.minimum(j + 1, n_tiles - 1), 0, 0)),
            pl.BlockSpec((tm, 1), lambda j, te, nv: (j, 0)),
            pl.BlockSpec(memory_space=pl.ANY),
            pl.BlockSpec((1, 1, D, F), lambda j, te, nv: (layer, te[j], 0, 0)),
            pl.BlockSpec((1, 1, D, F), lambda j, te, nv: (layer, te[j], 0, 0)),
            pl.BlockSpec((1, 1, F, D), lambda j, te, nv: (layer, te[j], 0, 0)),
        ],
        out_specs=pl.BlockSpec((tm, D), lambda j, te, nv: (j, 0)),
        scratch_shapes=[pltpu.VMEM((2, tm, D), F32), pltpu.SMEM((2, tm), jnp.int32),
                        pltpu.SemaphoreType.DMA((2,)), pltpu.SemaphoreType.DMA],
    )
    return pl.pallas_call(
        functools.partial(_expert_kernel, tm=tm),
        out_shape=jax.ShapeDtypeStruct((P, D), F32),
        grid_spec=grid_spec,
        compiler_params=_cparams(1),
        name="moe_experts",
    )(tile_expert, n_valid, tok3, tok3, w_sorted, h2d, wg, wu, wd)


def _combine_kernel(p1_ref, p1n_ref, p2_ref, p2n_ref, y_hbm, x_ref, gate_ref, o_ref,
                    buf1, buf2, idx1, idx2, sem1, sem2, isem, *, tm, tiles_per_batch, n_ctx):
    j = pl.program_id(0)
    slot = j % 2

    @pl.when(j == 0)
    def _():
        _row_gather(p1_ref, idx1, isem, y_hbm, buf1, sem1, 0, tm)
        _row_gather(p2_ref, idx2, isem, y_hbm, buf2, sem2, 0, tm)

    @pl.when(j + 1 < pl.num_programs(0))
    def _():
        _row_gather(p1n_ref, idx1, isem, y_hbm, buf1, sem1, 1 - slot, tm)
        _row_gather(p2n_ref, idx2, isem, y_hbm, buf2, sem2, 1 - slot, tm)

    _row_gather_wait(y_hbm, buf1, sem1, slot, tm)
    _row_gather_wait(y_hbm, buf2, sem2, slot, tm)
    gate = _row_gate(gate_ref, j % tiles_per_batch, tm, n_ctx)
    o_ref[...] = x_ref[...] + gate * (buf1[slot] + buf2[slot])


def _combine(x2d, y_sorted, pos1, pos2, gate, dims):
    M, D = x2d.shape
    tm = dims["tm_comb"]
    n_tiles = M // tm
    tpb = dims["rows"] // tm
    p1 = pos1.reshape(n_tiles, 1, tm)
    p2 = pos2.reshape(n_tiles, 1, tm)
    cur = pl.BlockSpec((1, 1, tm), lambda j: (j, 0, 0))
    nxt = pl.BlockSpec((1, 1, tm), lambda j: (jnp.minimum(j + 1, n_tiles - 1), 0, 0))
    return pl.pallas_call(
        functools.partial(_combine_kernel, tm=tm, tiles_per_batch=tpb, n_ctx=dims["n_ctx"]),
        out_shape=jax.ShapeDtypeStruct((M, D), F32),
        grid=(n_tiles,),
        in_specs=[cur, nxt, cur, nxt,
                  pl.BlockSpec(memory_space=pl.ANY),
                  pl.BlockSpec((tm, D), lambda j: (j, 0)),
                  pl.BlockSpec((1, 2, D), lambda j: (j // tpb, 0, 0))],
        out_specs=pl.BlockSpec((tm, D), lambda j: (j, 0)),
        scratch_shapes=[pltpu.VMEM((2, tm, D), F32), pltpu.VMEM((2, tm, D), F32),
                        pltpu.SMEM((2, tm), jnp.int32), pltpu.SMEM((2, tm), jnp.int32),
                        pltpu.SemaphoreType.DMA((2,)), pltpu.SemaphoreType.DMA((2,)),
                        pltpu.SemaphoreType.DMA],
        input_output_aliases={5: 0},
        compiler_params=_cparams(1),
        name="moe_combine",
    )(p1, p1, p2, p2, y_sorted, x2d, gate)


def _dispatch_plan(meta, counts, tm, n_tiles):
    M = meta.shape[1]
    E = counts.shape[0]
    i1 = meta[0].astype(jnp.int32)
    i2 = meta[1].astype(jnp.int32)
    w1, w2 = meta[2], meta[3]
    r1 = meta[4].astype(jnp.int32)
    r2 = meta[5].astype(jnp.int32)
    cnt = counts[:, 0].astype(jnp.int32)
    padded = ((cnt + tm - 1) // tm) * tm
    seg_end = jnp.cumsum(padded)
    seg_start = seg_end - padded
    pos1 = seg_start[i1] + r1
    pos2 = seg_start[i2] + r2
    n_valid = seg_end[-1] // tm
    tile_start = jnp.arange(n_tiles, dtype=jnp.int32) * tm
    tile_expert = jnp.minimum(jnp.searchsorted(seg_end, tile_start, side="right"), E - 1)
    last = jnp.maximum(n_valid - 1, 0)
    tile_expert = jnp.where(tile_start < seg_end[-1], tile_expert, tile_expert[last]).astype(jnp.int32)
    tok = jnp.arange(M, dtype=jnp.int32)
    P = n_tiles * tm
    tok_sorted = jnp.zeros((P,), jnp.int32).at[pos1].set(tok).at[pos2].set(tok)
    w_sorted = jnp.zeros((P,), F32).at[pos1].set(w1).at[pos2].set(w2)
    return (pos1, pos2, tok_sorted.reshape(n_tiles, 1, tm), w_sorted.reshape(P, 1),
            tile_expert, n_valid.reshape(1).astype(jnp.int32))


def _rope_tables(seq, n_ctx):
    rows = seq // GRID_W
    row = jnp.repeat(jnp.arange(rows), GRID_W).astype(F32)
    col = jnp.tile(jnp.arange(GRID_W), rows).astype(F32)
    inv = ROPE_THETA ** (-(jnp.arange(0, ROPE_AXIS_DIM, 2, dtype=F32) / ROPE_AXIS_DIM))
    ang_r = row[:, None] * inv
    ang_c = col[:, None] * inv
    ang = jnp.concatenate([ang_r, ang_r, ang_c, ang_c], axis=-1)
    lane = jnp.arange(HEAD_DIM)
    sign = jnp.where((lane % ROPE_AXIS_DIM) < (ROPE_AXIS_DIM // 2), -1.0, 1.0).astype(F32)
    cos = jnp.concatenate([jnp.ones((n_ctx, HEAD_DIM), F32), jnp.cos(ang)], axis=0)
    sin = jnp.concatenate([jnp.zeros((n_ctx, HEAD_DIM), F32), jnp.sin(ang) * sign], axis=0)
    return cos, sin


def _dims(D, S, C, d_expert):
    ha = D // 256
    hkv = ha // 4
    hb = D // 512
    widths = [("ka", hkv * HEAD_DIM), ("va", hkv * HEAD_DIM), ("kb", hb * 2 * HEAD_DIM),
              ("vb", hb * 2 * HEAD_DIM), ("qa", ha * HEAD_DIM), ("qb", hb * 2 * HEAD_DIM),
              ("gate", 2 * D)]
    seg, off = {}, 0
    for name, wdt in widths:
        seg[name] = (off, off + wdt)
        off += wdt
    R = C + S
    tm = _pick_tile(R, (768, 512, 384, 256, 128))
    tn = _pick_tile(D, (512, 256, 128))
    tn_in = _pick_tile(hkv * HEAD_DIM, (512, 256, 128))
    tm_in = _pick_tile(R, (1408, 768, 512, 384, 256, 128))
    tm_comb = _pick_tile(R, (256, 128))
    return dict(ha=ha, hkv=hkv, group=ha // hkv, hb=hb, seg=seg, n_cols=off, rows=R, n_ctx=C,
                tm=tm, tn=tn, tn_in=tn_in, tm_in=tm_in, tm_exp=256, tm_comb=tm_comb)


def kernel(x, c, ctx, c_ctx, ada_down, ada_up, ada_bias, norm_mix, norm_ffn, w_in, sink_logit,
           lam_q1, lam_k1, lam_q2, lam_k2, subln_gain, w_branch_a, w_branch_b, w_out,
           w_router, b_router, w_exp_gate, w_exp_up, w_exp_down, norm_final):
    B, S, D = x.shape
    C = ctx.shape[1]
    depth = w_in.shape[0]
    dims = _dims(D, S, C, w_exp_gate.shape[-1])
    R = dims["rows"]
    M = B * R

    cond = jnp.concatenate([c, c_ctx[None], jnp.zeros((8 - (B + 1) % 8 if (B + 1) % 8 else 0, D), F32)])
    mods = _adaln(cond, ada_down, ada_up, ada_bias)
    mods = mods.reshape(depth, mods.shape[1], N_MOD, D)
    lat = mods[:, :B]
    cx = jnp.broadcast_to(mods[:, B:B + 1], lat.shape)
    mod = jnp.stack([cx, lat], axis=2)

    cos_tab, sin_tab = _rope_tables(S, C)
    wr_t = w_router.T.astype(BF16)
    xa = jnp.concatenate([ctx, x], axis=1)
    w_in_b, wa_b, wb_b, wo_b = (w.astype(BF16) for w in (w_in, w_branch_a, w_branch_b, w_out))
    wg_b, wu_b, wd_b = (w.astype(BF16) for w in (w_exp_gate, w_exp_up, w_exp_down))
    n_experts = w_router.shape[1]
    tm_exp = dims["tm_exp"]
    n_exp_tiles = -(-(2 * M + n_experts * (tm_exp - 1)) // tm_exp)

    for l in range(depth):
        lam_init = 0.8 - 0.6 * math.exp(-0.3 * l)
        m = [mod[l, :, :, k] for k in range(N_MOD)]
        h = _norm_mod(xa, norm_mix[l], m[0], m[1], C)
        px = _inproj(h.reshape(M, D), w_in_b, l, cos_tab, sin_tab, dims)
        px3 = px.reshape(B, R, -1)
        ya = _win_attn(px3, sink_logit[l], dims)
        yb = _diff_attn(px3, lam_q1[l], lam_k1[l], lam_q2[l], lam_k2[l], subln_gain[l], lam_init, dims)
        mg = _merge(ya.reshape(M, -1), yb.reshape(M, -1), wa_b, wb_b, l, px, dims)
        x2d = _outproj(mg, wo_b, l, xa.reshape(M, D), m[2], dims)
        xa = x2d.reshape(B, R, D)
        h, meta, counts = _norm_router(xa, norm_ffn[l], m[3], m[4], wr_t, b_router, C)
        pos1, pos2, tok3, w_sorted, tile_expert, n_valid = _dispatch_plan(meta, counts, tm_exp, n_exp_tiles)
        y_sorted = _experts(h.reshape(M, D), tok3, w_sorted, tile_expert, n_valid,
                            wg_b, wu_b, wd_b, l, tm_exp)
        xa = _combine(x2d, y_sorted, pos1, pos2, m[5], dims).reshape(B, R, D)
    return _final_norm(xa, norm_final, C)
```

```python
import functools
import math

import jax
import jax.numpy as jnp
from jax import lax
from jax.experimental import pallas as pl
from jax.experimental.pallas import tpu as pltpu

HEAD_DIM = 128
GRID_W = 64
ROPE_AXIS_DIM = HEAD_DIM // 2
ROPE_THETA = 10000.0
BLOCK = 128
N_GROUPS = 4
ROUTED_SCALE = 1.0
NORM_EPS = 1e-6
NEG_INF = -1e30
N_MOD = 6
Q_SCALE = HEAD_DIM ** -0.5
LOG2E = math.log2(math.e)
MIN_SOFTMAX_DENOM = 2.0 ** -80

F32 = jnp.float32
BF16 = jnp.bfloat16

VMEM_LIMIT_BYTES = 56 * 1024 * 1024


def _cparams(n_axes):
    return pltpu.CompilerParams(dimension_semantics=("arbitrary",) * n_axes,
                                vmem_limit_bytes=VMEM_LIMIT_BYTES)


def _pick_tile(total, candidates):
    for t in candidates:
        if total % t == 0:
            return t
    raise ValueError(f"no tile in {candidates} divides {total}")


def _dot(a, b):
    return jnp.dot(a, b, preferred_element_type=F32)


def _dot_nt(a, b):
    return lax.dot_general(a, b, (((1,), (1,)), ((), ())), preferred_element_type=F32)


def _sigmoid(x):
    return 1.0 / (1.0 + jnp.exp(-x))


def _adaln_kernel(cond_ref, down_ref, up_ref, bias_ref, o_ref):
    cond = cond_ref[...]
    s = (cond * _sigmoid(cond)).astype(BF16)
    t = _dot(s, down_ref[0].astype(BF16)).astype(BF16)
    o_ref[0] = _dot(t, up_ref[0].astype(BF16)) + bias_ref[0]


def _adaln(cond, down, up, bias):
    L, D, rank = down.shape
    n_out = up.shape[-1]
    rows = cond.shape[0]
    tn = _pick_tile(n_out, (2048, 1024, 512, 256, 128))
    return pl.pallas_call(
        _adaln_kernel,
        out_shape=jax.ShapeDtypeStruct((L, rows, n_out), F32),
        grid=(L, n_out // tn),
        in_specs=[
            pl.BlockSpec((rows, D), lambda l, j: (0, 0)),
            pl.BlockSpec((1, D, rank), lambda l, j: (l, 0, 0)),
            pl.BlockSpec((1, rank, tn), lambda l, j: (l, 0, j)),
            pl.BlockSpec((1, 1, tn), lambda l, j: (l, 0, j)),
        ],
        out_specs=pl.BlockSpec((1, rows, tn), lambda l, j: (l, 0, j)),
        compiler_params=_cparams(2),
        name="adaln",
    )(cond, down, up, bias.reshape(L, 1, n_out))


def _norm_mod_kernel(x_ref, gain_ref, shift_ref, scale_ref, o_ref, *, tr, n_ctx):
    x = x_ref[0]
    var = jnp.mean(x * x, axis=-1, keepdims=True)
    y = x * lax.rsqrt(var + NORM_EPS) * gain_ref[...]
    row = pl.program_id(1) * tr + lax.broadcasted_iota(jnp.int32, (tr, 1), 0)
    is_ctx = row < n_ctx
    shift = jnp.where(is_ctx, shift_ref[0, 0:1, :], shift_ref[0, 1:2, :])
    scale = jnp.where(is_ctx, scale_ref[0, 0:1, :], scale_ref[0, 1:2, :])
    o_ref[0] = (y * (1.0 + scale) + shift).astype(o_ref.dtype)


def _norm_mod(xa, gain, shift, scale, n_ctx):
    B, R, D = xa.shape
    tr = _pick_tile(R, (256, 128))
    return pl.pallas_call(
        functools.partial(_norm_mod_kernel, tr=tr, n_ctx=n_ctx),
        out_shape=jax.ShapeDtypeStruct((B, R, D), BF16),
        grid=(B, R // tr),
        in_specs=[
            pl.BlockSpec((1, tr, D), lambda b, i: (b, i, 0)),
            pl.BlockSpec((1, D), lambda b, i: (0, 0)),
            pl.BlockSpec((1, 2, D), lambda b, i: (b, 0, 0)),
            pl.BlockSpec((1, 2, D), lambda b, i: (b, 0, 0)),
        ],
        out_specs=pl.BlockSpec((1, tr, D), lambda b, i: (b, i, 0)),
        compiler_params=_cparams(2),
        name="norm_mod",
    )(xa, gain.reshape(1, D), shift, scale)


def _final_norm_kernel(x_ref, gain_ref, o_ref):
    x = x_ref[0]
    var = jnp.mean(x * x, axis=-1, keepdims=True)
    o_ref[0] = x * lax.rsqrt(var + NORM_EPS) * gain_ref[...]


def _final_norm(xa, gain, n_ctx):
    B, R, D = xa.shape
    S = R - n_ctx
    tr = _pick_tile(math.gcd(S, n_ctx), (256, 128))
    off = n_ctx // tr
    return pl.pallas_call(
        _final_norm_kernel,
        out_shape=jax.ShapeDtypeStruct((B, S, D), F32),
        grid=(B, S // tr),
        in_specs=[
            pl.BlockSpec((1, tr, D), lambda b, i: (b, i + off, 0)),
            pl.BlockSpec((1, D), lambda b, i: (0, 0)),
        ],
        out_specs=pl.BlockSpec((1, tr, D), lambda b, i: (b, i, 0)),
        compiler_params=_cparams(2),
        name="final_norm",
    )(xa, gain.reshape(1, D))


def _rope_cols(a, cos, sin_signed, lane_lo):
    rot = jnp.where(lane_lo, pltpu.roll(a, HEAD_DIM - 32, 1), pltpu.roll(a, 32, 1))
    return a * cos + rot * sin_signed


def _inproj_kernel(h_ref, w_ref, cos_ref, sin_ref, o_ref, *, tn, rope_ranges, qa_range, qb_range):
    j = pl.program_id(1)
    acc = _dot(h_ref[...], w_ref[0])
    roped = functools.reduce(jnp.logical_or, [(j >= lo) & (j < hi) for lo, hi in rope_ranges])

    @pl.when(roped)
    def _():
        cos = cos_ref[...]
        sin_signed = sin_ref[...]
        mult = jnp.where((j >= qa_range[0]) & (j < qa_range[1]), Q_SCALE,
                         jnp.where((j >= qb_range[0]) & (j < qb_range[1]), Q_SCALE * LOG2E, 1.0)
                         ).astype(F32)
        lane = lax.broadcasted_iota(jnp.int32, (1, HEAD_DIM), 1)
        lane_lo = (lane % ROPE_AXIS_DIM) < (ROPE_AXIS_DIM // 2)
        for c in range(tn // HEAD_DIM):
            sl = slice(c * HEAD_DIM, (c + 1) * HEAD_DIM)
            o_ref[:, sl] = (_rope_cols(acc[:, sl], cos, sin_signed, lane_lo) * mult).astype(o_ref.dtype)

    @pl.when(jnp.logical_not(roped))
    def _():
        o_ref[...] = acc.astype(o_ref.dtype)


def _inproj(h2d, w_all, layer, cos_tab, sin_tab, dims):
    M, D = h2d.shape
    N = w_all.shape[2]
    R = cos_tab.shape[0]
    tm = dims["tm_in"]
    tn = dims["tn_in"]
    tiles_per_batch = R // tm
    seg = dims["seg"]
    rng = lambda k: (seg[k][0] // tn, seg[k][1] // tn)
    rope_ranges = tuple(rng(k) for k in ("ka", "kb", "qa", "qb"))
    return pl.pallas_call(
        functools.partial(_inproj_kernel, tn=tn, rope_ranges=rope_ranges,
                          qa_range=rng("qa"), qb_range=rng("qb")),
        out_shape=jax.ShapeDtypeStruct((M, N), BF16),
        grid=(M // tm, N // tn),
        in_specs=[
            pl.BlockSpec((tm, D), lambda i, j: (i, 0)),
            pl.BlockSpec((1, D, tn), lambda i, j: (layer, 0, j)),
            pl.BlockSpec((tm, HEAD_DIM), lambda i, j: (i % tiles_per_batch, 0)),
            pl.BlockSpec((tm, HEAD_DIM), lambda i, j: (i % tiles_per_batch, 0)),
        ],
        out_specs=pl.BlockSpec((tm, tn), lambda i, j: (i, j)),
        compiler_params=_cparams(2),
        name="inproj_rope",
    )(h2d, w_all, cos_tab, sin_tab)


def _win_attn_kernel(sink_ref, *refs, hkv, group, n_ctx_blocks, n_blocks):
    q_refs = refs[:hkv]
    kc_ref, vc_ref, kp_ref, ko_ref, kn_ref, vp_ref, vo_ref, vn_ref, o_ref = refs[hkv:]
    n = pl.program_id(1)
    rows = group * BLOCK
    qi = lax.broadcasted_iota(jnp.int32, (rows, BLOCK), 0) % BLOCK
    kj = lax.broadcasted_iota(jnp.int32, (rows, BLOCK), 1)
    is_lat = n >= n_ctx_blocks
    valid_p = jnp.logical_and(is_lat, n - 1 >= n_ctx_blocks)
    valid_n = jnp.logical_and(is_lat, n + 1 < n_blocks)
    mask = jnp.concatenate([
        jnp.logical_and(kj >= qi, valid_p),
        jnp.logical_and(kj >= 0, is_lat),
        jnp.logical_and(kj <= qi, valid_n)], axis=1)
    for h in range(hkv):
        cs = slice(h * HEAD_DIM, (h + 1) * HEAD_DIM)
        q = jnp.concatenate(
            [q_refs[h][0, :, g * HEAD_DIM:(g + 1) * HEAD_DIM] for g in range(group)], axis=0)
        sink = jnp.concatenate(
            [jnp.full((BLOCK, 1), sink_ref[h * group + g], F32) for g in range(group)], axis=0)
        k_band = jnp.concatenate([kp_ref[0, :, cs], ko_ref[0, :, cs], kn_ref[0, :, cs]], axis=0)
        v_band = jnp.concatenate([vp_ref[0, :, cs], vo_ref[0, :, cs], vn_ref[0, :, cs]], axis=0)
        s_c = _dot_nt(q, kc_ref[0, :, cs])
        s_b = jnp.where(mask, _dot_nt(q, k_band), NEG_INF)
        m = jnp.maximum(jnp.maximum(jnp.max(s_c, axis=-1, keepdims=True),
                                    jnp.max(s_b, axis=-1, keepdims=True)), sink)
        e_c = jnp.exp(s_c - m)
        e_b = jnp.exp(s_b - m)
        denom = (jnp.sum(e_c, axis=-1, keepdims=True) + jnp.sum(e_b, axis=-1, keepdims=True)
                 + jnp.exp(sink - m))
        out = (_dot(e_b.astype(BF16), v_band) + _dot(e_c.astype(BF16), vc_ref[0, :, cs])) / denom
        for g in range(group):
            col = (h * group + g) * HEAD_DIM
            o_ref[0, :, col:col + HEAD_DIM] = out[g * BLOCK:(g + 1) * BLOCK].astype(o_ref.dtype)


def _win_attn(px, sink, dims):
    B, R, _ = px.shape
    seg = dims["seg"]
    hkv, group, n_ctx = dims["hkv"], dims["group"], dims["n_ctx"]
    nb = R // BLOCK
    ncb = n_ctx // BLOCK
    qw = group * HEAD_DIM
    kw = hkv * HEAD_DIM
    k0 = seg["ka"][0] // kw
    v0 = seg["va"][0] // kw
    q0 = seg["qa"][0] // qw
    assert seg["qa"][0] % qw == 0 and seg["ka"][0] % kw == 0 and seg["va"][0] % kw == 0

    def q_spec(h):
        return pl.BlockSpec((1, BLOCK, qw), lambda b, n: (b, n, q0 + h))

    def band_spec(col, shift):
        return pl.BlockSpec((1, BLOCK, kw), lambda b, n: (b, jnp.clip(n + shift, 0, nb - 1), col))

    return pl.pallas_call(
        functools.partial(_win_attn_kernel, hkv=hkv, group=group, n_ctx_blocks=ncb, n_blocks=nb),
        out_shape=jax.ShapeDtypeStruct((B, R, hkv * qw), BF16),
        grid=(B, nb),
        in_specs=[pl.BlockSpec(memory_space=pltpu.SMEM)]
        + [q_spec(h) for h in range(hkv)]
        + [pl.BlockSpec((1, n_ctx, kw), lambda b, n: (b, 0, k0)),
           pl.BlockSpec((1, n_ctx, kw), lambda b, n: (b, 0, v0)),
           band_spec(k0, -1), band_spec(k0, 0), band_spec(k0, 1),
           band_spec(v0, -1), band_spec(v0, 0), band_spec(v0, 1)],
        out_specs=pl.BlockSpec((1, BLOCK, hkv * qw), lambda b, n: (b, n, 0)),
        compiler_params=_cparams(2),
        name="window_attn",
    )(sink, *([px] * (hkv + 8)))


def _diff_attn_kernel(lq1_ref, lk1_ref, lq2_ref, lk2_ref, gain_ref, q_ref, k_ref, v_ref, o_ref,
                      acc1_ref, acc2_ref, s_ref, *, tq, tk, n_ctx, n_rows, lam_init):
    lam = (jnp.exp(jnp.sum(lq1_ref[...] * lk1_ref[...], axis=-1, keepdims=True))
           - jnp.exp(jnp.sum(lq2_ref[...] * lk2_ref[...], axis=-1, keepdims=True)) + lam_init)
    gain = gain_ref[...] * (1.0 - lam_init)

    def finish(q_start, nq, l1, l2):
        o = acc1_ref[pl.ds(0, nq), :] / l1 - lam * (acc2_ref[pl.ds(0, nq), :] / l2)
        var = jnp.mean(o * o, axis=-1, keepdims=True)
        y = o * lax.rsqrt(var + NORM_EPS) * gain
        o_ref[0, pl.ds(q_start, nq), :] = y.astype(o_ref.dtype)

    def attend_online(q_start, nq, kv_len, chunk):
        q1 = q_ref[0, pl.ds(q_start, nq), :HEAD_DIM]
        q2 = q_ref[0, pl.ds(q_start, nq), HEAD_DIM:]
        accs = (acc1_ref.at[pl.ds(0, nq), :], acc2_ref.at[pl.ds(0, nq), :])

        def step(t, carry):
            start = pl.multiple_of(t * chunk, chunk)
            k = k_ref[0, pl.ds(start, chunk), :]
            v = v_ref[0, pl.ds(start, chunk), :]
            out = []
            for which, (q, acc) in enumerate(zip((q1, q2), accs)):
                m, l = carry[2 * which], carry[2 * which + 1]
                s = _dot_nt(q, k[:, which * HEAD_DIM:(which + 1) * HEAD_DIM])
                n = jnp.maximum(m, jnp.max(s, axis=-1, keepdims=True))
                alpha = jnp.exp2(m - n)
                p = jnp.exp2(s - n)
                acc[...] = alpha * acc[...] + _dot(p.astype(BF16), v)
                out += [n, alpha * l + jnp.sum(p, axis=-1, keepdims=True)]
            return tuple(out)

        for acc in accs:
            acc[...] = jnp.zeros((nq, 2 * HEAD_DIM), F32)
        carry = (jnp.full((nq, 1), -jnp.inf, F32), jnp.zeros((nq, 1), F32)) * 2
        if kv_len == chunk:
            carry = step(0, carry)
        else:
            carry = lax.fori_loop(0, kv_len // chunk, step, carry)
        finish(q_start, nq, carry[1], carry[3])

    def attend_bounded(q_start, nq, kv_len, chunk, key_norm):
        q1 = q_ref[0, pl.ds(q_start, nq), :HEAD_DIM]
        q2 = q_ref[0, pl.ds(q_start, nq), HEAD_DIM:]
        qs = (q1, q2)
        accs = (acc1_ref.at[pl.ds(0, nq), :], acc2_ref.at[pl.ds(0, nq), :])
        bound = []
        for q, kn in zip(qs, key_norm):
            qf = q.astype(F32)
            bound.append(jnp.sqrt(jnp.sum(qf * qf, axis=-1, keepdims=True)) * kn)

        def scores(t, slot):
            k = k_ref[0, pl.ds(pl.multiple_of(t * chunk, chunk), chunk), :]
            for which in range(2):
                s_ref[slot, which, pl.ds(0, nq), pl.ds(0, chunk)] = _dot_nt(
                    qs[which], k[:, which * HEAD_DIM:(which + 1) * HEAD_DIM])

        def softmax_pv(t, slot, ls):
            v = v_ref[0, pl.ds(pl.multiple_of(t * chunk, chunk), chunk), :]
            out = []
            for which in range(2):
                p = jnp.exp2(s_ref[slot, which, pl.ds(0, nq), pl.ds(0, chunk)] - bound[which])
                accs[which][...] += _dot(p.astype(BF16), v)
                out.append(ls[which] + jnp.sum(p, axis=-1, keepdims=True))
            return tuple(out)

        for acc in accs:
            acc[...] = jnp.zeros((nq, 2 * HEAD_DIM), F32)
        ls = (jnp.zeros((nq, 1), F32),) * 2
        n_chunks = kv_len // chunk

        def pair(u, c):
            scores(2 * u + 1, 1)
            c = softmax_pv(2 * u, 0, c)
            scores(2 * u + 2, 0)
            return softmax_pv(2 * u + 1, 1, c)

        scores(0, 0)
        n_pairs = (n_chunks - 1) // 2
        if n_pairs > 0:
            ls = lax.fori_loop(0, n_pairs, pair, ls)
        if n_chunks % 2 == 0:
            scores(n_chunks - 1, 1)
            ls = softmax_pv(n_chunks - 2, 0, ls)
            ls = softmax_pv(n_chunks - 1, 1, ls)
        else:
            ls = softmax_pv(n_chunks - 1, 0, ls)
        return ls

    attend_online(0, n_ctx, n_ctx, n_ctx)

    def max_norm(t, mx):
        k = k_ref[0, pl.ds(pl.multiple_of(t * tk, tk), tk), :].astype(F32)
        sq = k * k
        return tuple(
            jnp.maximum(mx[w], jnp.max(jnp.sum(sq[:, w * HEAD_DIM:(w + 1) * HEAD_DIM], axis=-1, keepdims=True),
                                       axis=0, keepdims=True))
            for w in range(2))

    key_sq = lax.fori_loop(0, n_rows // tk, max_norm, (jnp.zeros((1, 1), F32),) * 2)
    key_norm = tuple(jnp.sqrt(x) for x in key_sq)
    align = math.gcd(n_ctx, tq)

    def latent_tile(i, _):
        q_start = pl.multiple_of(n_ctx + i * tq, align)
        l1, l2 = attend_bounded(q_start, tq, n_rows, tk, key_norm)
        smallest = jnp.min(jnp.minimum(l1, l2), axis=0, keepdims=True)[0, 0]
        safe = smallest >= MIN_SOFTMAX_DENOM

        @pl.when(safe)
        def _():
            finish(q_start, tq, l1, l2)

        @pl.when(jnp.logical_not(safe))
        def _():
            attend_online(q_start, tq, n_rows, tk)

        return 0

    lax.fori_loop(0, (n_rows - n_ctx) // tq, latent_tile, 0)


def _diff_attn(px, lq1, lk1, lq2, lk2, gain, lam_init, dims):
    B, R, _ = px.shape
    seg = dims["seg"]
    hb, n_ctx = dims["hb"], dims["n_ctx"]
    w = 2 * HEAD_DIM
    tq = _pick_tile(R - n_ctx, (512, 256, 128))
    tk = _pick_tile(R, (768, 512, 384, 256, 128))
    assert n_ctx <= tq and n_ctx % 16 == 0
    q0, k0, v0 = seg["qb"][0] // w, seg["kb"][0] // w, seg["vb"][0] // w
    vec = lambda a: a.reshape(1, -1)
    small = lambda n: pl.BlockSpec((1, n), lambda b, h: (0, 0))
    return pl.pallas_call(
        functools.partial(_diff_attn_kernel, tq=tq, tk=tk, n_ctx=n_ctx, n_rows=R, lam_init=lam_init),
        out_shape=jax.ShapeDtypeStruct((B, R, hb * w), BF16),
        grid=(B, hb),
        in_specs=[
            small(HEAD_DIM), small(HEAD_DIM), small(HEAD_DIM), small(HEAD_DIM), small(w),
            pl.BlockSpec((1, R, w), lambda b, h: (b, 0, q0 + h)),
            pl.BlockSpec((1, R, w), lambda b, h: (b, 0, k0 + h)),
            pl.BlockSpec((1, R, w), lambda b, h: (b, 0, v0 + h)),
        ],
        out_specs=pl.BlockSpec((1, R, w), lambda b, h: (b, 0, h)),
        scratch_shapes=[pltpu.VMEM((tq, w), F32), pltpu.VMEM((tq, w), F32),
                        pltpu.VMEM((2, 2, tq, tk), F32)],
        compiler_params=_cparams(2),
        name="diff_attn",
    )(vec(lq1), vec(lk1), vec(lq2), vec(lk2), vec(gain), px, px, px)


def _merge_kernel(ya_ref, yb_ref, wa_ref, wb_ref, ga_ref, gb_ref, o_ref):
    pa = _dot(ya_ref[...], wa_ref[0])
    pb = _dot(yb_ref[...], wb_ref[0])
    m = _sigmoid(ga_ref[...].astype(F32)) * pa + _sigmoid(gb_ref[...].astype(F32)) * pb
    o_ref[...] = m.astype(o_ref.dtype)


def _merge(ya, yb, wa, wb, layer, px2d, dims):
    M = ya.shape[0]
    D = wa.shape[2]
    tm, tn = dims["tm"], dims["tn"]
    g0 = dims["seg"]["gate"][0] // tn
    g1 = g0 + D // tn
    return pl.pallas_call(
        _merge_kernel,
        out_shape=jax.ShapeDtypeStruct((M, D), BF16),
        grid=(M // tm, D // tn),
        in_specs=[
            pl.BlockSpec((tm, ya.shape[1]), lambda i, j: (i, 0)),
            pl.BlockSpec((tm, yb.shape[1]), lambda i, j: (i, 0)),
            pl.BlockSpec((1, wa.shape[1], tn), lambda i, j: (layer, 0, j)),
            pl.BlockSpec((1, wb.shape[1], tn), lambda i, j: (layer, 0, j)),
            pl.BlockSpec((tm, tn), lambda i, j: (i, g0 + j)),
            pl.BlockSpec((tm, tn), lambda i, j: (i, g1 + j)),
        ],
        out_specs=pl.BlockSpec((tm, tn), lambda i, j: (i, j)),
        compiler_params=_cparams(2),
        name="merge_branches",
    )(ya, yb, wa, wb, px2d, px2d)


def _row_gate(gate_ref, tile_in_batch, tm, n_ctx):
    row = tile_in_batch * tm + lax.broadcasted_iota(jnp.int32, (tm, 1), 0)
    return jnp.where(row < n_ctx, gate_ref[0, 0:1, :], gate_ref[0, 1:2, :])


def _outproj_kernel(m_ref, w_ref, x_ref, gate_ref, o_ref, *, tm, tiles_per_batch, n_ctx):
    gate = _row_gate(gate_ref, pl.program_id(0) % tiles_per_batch, tm, n_ctx)
    o_ref[...] = x_ref[...] + gate * _dot(m_ref[...], w_ref[0])


def _outproj(m, w, layer, x2d, gate, dims):
    M, D = x2d.shape
    tm, tn = dims["tm"], dims["tn"]
    tpb = dims["rows"] // tm
    return pl.pallas_call(
        functools.partial(_outproj_kernel, tm=tm, tiles_per_batch=tpb, n_ctx=dims["n_ctx"]),
        out_shape=jax.ShapeDtypeStruct((M, D), F32),
        grid=(M // tm, D // tn),
        in_specs=[
            pl.BlockSpec((tm, D), lambda i, j: (i, 0)),
            pl.BlockSpec((1, D, tn), lambda i, j: (layer, 0, j)),
            pl.BlockSpec((tm, tn), lambda i, j: (i, j)),
            pl.BlockSpec((1, 2, tn), lambda i, j: (i // tpb, 0, j)),
        ],
        out_specs=pl.BlockSpec((tm, tn), lambda i, j: (i, j)),
        input_output_aliases={2: 0},
        compiler_params=_cparams(2),
        name="outproj_residual",
    )(m, w, x2d, gate)


def _route(logits, bias, n_experts):
    per_group = n_experts // N_GROUPS
    scores = _sigmoid(logits)
    sel = scores + bias
    srow = [scores[e:e + 1, :] for e in range(n_experts)]
    row = [sel[e:e + 1, :] for e in range(n_experts)]
    best, gidx = None, None
    for g in range(N_GROUPS):
        mem = row[g * per_group:(g + 1) * per_group]
        gs = None
        for a in range(per_group):
            for b in range(a + 1, per_group):
                s = mem[a] + mem[b]
                gs = s if gs is None else jnp.maximum(gs, s)
        if best is None:
            best, gidx = gs, jnp.zeros_like(gs, dtype=jnp.int32)
        else:
            better = gs > best
            gidx = jnp.where(better, g, gidx)
            best = jnp.where(better, gs, best)
    masked = [jnp.where(gidx == (e // per_group), row[e], NEG_INF) for e in range(n_experts)]
    m1, i1 = masked[0], jnp.zeros_like(gidx)
    for e in range(1, n_experts):
        better = masked[e] > m1
        i1 = jnp.where(better, e, i1)
        m1 = jnp.where(better, masked[e], m1)
    m2 = jnp.full_like(m1, -jnp.inf)
    i2 = jnp.zeros_like(gidx)
    for e in range(n_experts):
        cand = jnp.where(i1 == e, -jnp.inf, masked[e])
        better = cand > m2
        i2 = jnp.where(better, e, i2)
        m2 = jnp.where(better, cand, m2)
    w1 = functools.reduce(jnp.add, [jnp.where(i1 == e, srow[e], 0.0) for e in range(n_experts)])
    w2 = functools.reduce(jnp.add, [jnp.where(i2 == e, srow[e], 0.0) for e in range(n_experts)])
    tot = w1 + w2
    return i1, i2, w1 / tot * ROUTED_SCALE, w2 / tot * ROUTED_SCALE


META_ROWS = 8


def _norm_router_kernel(x_ref, gain_ref, shift_ref, scale_ref, wr_ref, bias_ref,
                        h_ref, meta_ref, cnt_ref, carry_ref, *, tr, n_ctx, n_experts):
    first = jnp.logical_and(pl.program_id(0) == 0, pl.program_id(1) == 0)

    @pl.when(first)
    def _():
        carry_ref[...] = jnp.zeros_like(carry_ref)

    x = x_ref[0]
    var = jnp.mean(x * x, axis=-1, keepdims=True)
    y = x * lax.rsqrt(var + NORM_EPS) * gain_ref[...]
    row = pl.program_id(1) * tr + lax.broadcasted_iota(jnp.int32, (tr, 1), 0)
    is_ctx = row < n_ctx
    shift = jnp.where(is_ctx, shift_ref[0, 0:1, :], shift_ref[0, 1:2, :])
    scale = jnp.where(is_ctx, scale_ref[0, 0:1, :], scale_ref[0, 1:2, :])
    h = y * (1.0 + scale) + shift
    h_ref[0] = h

    logits = _dot_nt(wr_ref[...], h.astype(BF16))
    i1, i2, w1, w2 = _route(logits, bias_ref[...], n_experts)
    erow = lax.broadcasted_iota(jnp.int32, (n_experts, tr), 0)
    hit1 = erow == i1
    hit2 = erow == i2
    onehot = jnp.where(jnp.logical_or(hit1, hit2), 1.0, 0.0)
    before = (lax.broadcasted_iota(jnp.int32, (tr, tr), 0)
              < lax.broadcasted_iota(jnp.int32, (tr, tr), 1))
    prefix = _dot(onehot.astype(BF16), jnp.where(before, 1.0, 0.0).astype(BF16))
    seen = prefix + carry_ref[:, 0:1]
    r1 = jnp.sum(jnp.where(hit1, seen, 0.0), axis=0, keepdims=True)
    r2 = jnp.sum(jnp.where(hit2, seen, 0.0), axis=0, keepdims=True)
    carry_ref[...] = carry_ref[...] + jnp.sum(onehot, axis=1, keepdims=True)
    zero = jnp.zeros_like(w1)
    meta_ref[...] = jnp.concatenate(
        [i1.astype(F32), i2.astype(F32), w1, w2, r1, r2, zero, zero], axis=0)
    cnt_ref[...] = carry_ref[...]


def _norm_router(xa, gain, shift, scale, wr_t, bias, n_ctx):
    B, R, D = xa.shape
    E = wr_t.shape[0]
    tr = _pick_tile(R, (256, 128))
    nt = R // tr
    return pl.pallas_call(
        functools.partial(_norm_router_kernel, tr=tr, n_ctx=n_ctx, n_experts=E),
        out_shape=(jax.ShapeDtypeStruct((B, R, D), F32),
                   jax.ShapeDtypeStruct((META_ROWS, B * R), F32),
                   jax.ShapeDtypeStruct((E, HEAD_DIM), F32)),
        grid=(B, nt),
        in_specs=[
            pl.BlockSpec((1, tr, D), lambda b, i: (b, i, 0)),
            pl.BlockSpec((1, D), lambda b, i: (0, 0)),
            pl.BlockSpec((1, 2, D), lambda b, i: (b, 0, 0)),
            pl.BlockSpec((1, 2, D), lambda b, i: (b, 0, 0)),
            pl.BlockSpec((E, D), lambda b, i: (0, 0)),
            pl.BlockSpec((E, 1), lambda b, i: (0, 0)),
        ],
        out_specs=(pl.BlockSpec((1, tr, D), lambda b, i: (b, i, 0)),
                   pl.BlockSpec((META_ROWS, tr), lambda b, i: (0, b * nt + i)),
                   pl.BlockSpec((E, HEAD_DIM), lambda b, i: (0, 0))),
        scratch_shapes=[pltpu.VMEM((E, HEAD_DIM), F32)],
        compiler_params=_cparams(2),
        name="norm_router",
    )(xa, gain.reshape(1, D), shift, scale, wr_t, bias.reshape(E, 1).astype(F32))


def _row_gather(idx_vmem_ref, idx_smem, isem, src_hbm, buf, sem, slot, n_rows):
    cp = pltpu.make_async_copy(idx_vmem_ref.at[0, 0], idx_smem.at[slot], isem)
    cp.start()
    cp.wait()

    def body(r, carry):
        t = idx_smem[slot, r]
        pltpu.make_async_copy(src_hbm.at[pl.ds(t, 1), :], buf.at[slot, pl.ds(r, 1), :],
                              sem.at[slot]).start()
        return carry

    lax.fori_loop(0, n_rows, body, 0, unroll=8)


def _row_gather_wait(src_hbm, buf, sem, slot, n_rows):
    pltpu.make_async_copy(src_hbm.at[pl.ds(0, n_rows), :], buf.at[slot], sem.at[slot]).wait()


def _expert_kernel(te_ref, nv_ref, tok_ref, tok_next_ref, w_ref, h_hbm, wg_ref, wu_ref, wd_ref,
                   o_ref, buf, idx_smem, sem, isem, *, tm):
    j = pl.program_id(0)
    n_valid = nv_ref[0]
    slot = j % 2

    @pl.when(j == 0)
    def _():
        _row_gather(tok_ref, idx_smem, isem, h_hbm, buf, sem, 0, tm)

    @pl.when(j + 1 < n_valid)
    def _():
        _row_gather(tok_next_ref, idx_smem, isem, h_hbm, buf, sem, 1 - slot, tm)

    @pl.when(j < n_valid)
    def _():
        _row_gather_wait(h_hbm, buf, sem, slot, tm)
        h = buf[slot].astype(BF16)
        g = _dot(h, wg_ref[0, 0])
        u = _dot(h, wu_ref[0, 0])
        a = (g * _sigmoid(g) * u * w_ref[...]).astype(BF16)
        o_ref[...] = _dot(a, wd_ref[0, 0])

    @pl.when(j >= n_valid)
    def _():
        o_ref[...] = jnp.zeros_like(o_ref)


def _experts(h2d, tok3, w_sorted, tile_expert, n_valid, wg, wu, wd, layer, tm):
    M, D = h2d.shape
    F = wg.shape[-1]
    n_tiles = tok3.shape[0]
    P = n_tiles * tm
    grid_spec = pltpu.PrefetchScalarGridSpec(
        num_scalar_prefetch=2,
        grid=(n_tiles,),
        in_specs=[
            pl.BlockSpec((1, 1, tm), lambda j, te, nv: (j, 0, 0)),
            pl.BlockSpec((1, 1, tm), lambda j, te, nv: (jnp.minimum(j + 1, n_tiles - 1), 0, 0)),
            pl.BlockSpec((tm, 1), lambda j, te, nv: (j, 0)),
            pl.BlockSpec(memory_space=pl.ANY),
            pl.BlockSpec((1, 1, D, F), lambda j, te, nv: (layer, te[j], 0, 0)),
            pl.BlockSpec((1, 1, D, F), lambda j, te, nv: (layer, te[j], 0, 0)),
            pl.BlockSpec((1, 1, F, D), lambda j, te, nv: (layer, te[j], 0, 0)),
        ],
        out_specs=pl.BlockSpec((tm, D), lambda j, te, nv: (j, 0)),
        scratch_shapes=[pltpu.VMEM((2, tm, D), F32), pltpu.SMEM((2, tm), jnp.int32),
                        pltpu.SemaphoreType.DMA((2,)), pltpu.SemaphoreType.DMA],
    )
    return pl.pallas_call(
        functools.partial(_expert_kernel, tm=tm),
        out_shape=jax.ShapeDtypeStruct((P, D), F32),
        grid_spec=grid_spec,
        compiler_params=_cparams(1),
        name="moe_experts",
    )(tile_expert, n_valid, tok3, tok3, w_sorted, h2d, wg, wu, wd)


def _combine_kernel(p1_ref, p1n_ref, p2_ref, p2n_ref, y_hbm, x_ref, gate_ref, o_ref,
                    buf1, buf2, idx1, idx2, sem1, sem2, isem, *, tm, tiles_per_batch, n_ctx):
    j = pl.program_id(0)
    slot = j % 2

    @pl.when(j == 0)
    def _():
        _row_gather(p1_ref, idx1, isem, y_hbm, buf1, sem1, 0, tm)
        _row_gather(p2_ref, idx2, isem, y_hbm, buf2, sem2, 0, tm)

    @pl.when(j + 1 < pl.num_programs(0))
    def _():
        _row_gather(p1n_ref, idx1, isem, y_hbm, buf1, sem1, 1 - slot, tm)
        _row_gather(p2n_ref, idx2, isem, y_hbm, buf2, sem2, 1 - slot, tm)

    _row_gather_wait(y_hbm, buf1, sem1, slot, tm)
    _row_gather_wait(y_hbm, buf2, sem2, slot, tm)
    gate = _row_gate(gate_ref, j % tiles_per_batch, tm, n_ctx)
    o_ref[...] = x_ref[...] + gate * (buf1[slot] + buf2[slot])


def _combine(x2d, y_sorted, pos1, pos2, gate, dims):
    M, D = x2d.shape
    tm = dims["tm_comb"]
    n_tiles = M // tm
    tpb = dims["rows"] // tm
    p1 = pos1.reshape(n_tiles, 1, tm)
    p2 = pos2.reshape(n_tiles, 1, tm)
    cur = pl.BlockSpec((1, 1, tm), lambda j: (j, 0, 0))
    nxt = pl.BlockSpec((1, 1, tm), lambda j: (jnp.minimum(j + 1, n_tiles - 1), 0, 0))
    return pl.pallas_call(
        functools.partial(_combine_kernel, tm=tm, tiles_per_batch=tpb, n_ctx=dims["n_ctx"]),
        out_shape=jax.ShapeDtypeStruct((M, D), F32),
        grid=(n_tiles,),
        in_specs=[cur, nxt, cur, nxt,
                  pl.BlockSpec(memory_space=pl.ANY),
                  pl.BlockSpec((tm, D), lambda j: (j, 0)),
                  pl.BlockSpec((1, 2, D), lambda j: (j // tpb, 0, 0))],
        out_specs=pl.BlockSpec((tm, D), lambda j: (j, 0)),
        scratch_shapes=[pltpu.VMEM((2, tm, D), F32), pltpu.VMEM((2, tm, D), F32),
                        pltpu.SMEM((2, tm), jnp.int32), pltpu.SMEM((2, tm), jnp.int32),
                        pltpu.SemaphoreType.DMA((2,)), pltpu.SemaphoreType.DMA((2,)),
                        pltpu.SemaphoreType.DMA],
        input_output_aliases={5: 0},
        compiler_params=_cparams(1),
        name="moe_combine",
    )(p1, p1, p2, p2, y_sorted, x2d, gate)


def _dispatch_plan(meta, counts, tm, n_tiles):
    M = meta.shape[1]
    E = counts.shape[0]
    i1 = meta[0].astype(jnp.int32)
    i2 = meta[1].astype(jnp.int32)
    w1, w2 = meta[2], meta[3]
    r1 = meta[4].astype(jnp.int32)
    r2 = meta[5].astype(jnp.int32)
    cnt = counts[:, 0].astype(jnp.int32)
    padded = ((cnt + tm - 1) // tm) * tm
    seg_end = jnp.cumsum(padded)
    seg_start = seg_end - padded
    pos1 = seg_start[i1] + r1
    pos2 = seg_start[i2] + r2
    n_valid = seg_end[-1] // tm
    tile_start = jnp.arange(n_tiles, dtype=jnp.int32) * tm
    tile_expert = jnp.minimum(jnp.searchsorted(seg_end, tile_start, side="right"), E - 1)
    last = jnp.maximum(n_valid - 1, 0)
    tile_expert = jnp.where(tile_start < seg_end[-1], tile_expert, tile_expert[last]).astype(jnp.int32)
    tok = jnp.arange(M, dtype=jnp.int32)
    P = n_tiles * tm
    tok_sorted = jnp.zeros((P,), jnp.int32).at[pos1].set(tok).at[pos2].set(tok)
    w_sorted = jnp.zeros((P,), F32).at[pos1].set(w1).at[pos2].set(w2)
    return (pos1, pos2, tok_sorted.reshape(n_tiles, 1, tm), w_sorted.reshape(P, 1),
            tile_expert, n_valid.reshape(1).astype(jnp.int32))


def _rope_tables(seq, n_ctx):
    rows = seq // GRID_W
    row = jnp.repeat(jnp.arange(rows), GRID_W).astype(F32)
    col = jnp.tile(jnp.arange(GRID_W), rows).astype(F32)
    inv = ROPE_THETA ** (-(jnp.arange(0, ROPE_AXIS_DIM, 2, dtype=F32) / ROPE_AXIS_DIM))
    ang_r = row[:, None] * inv
    ang_c = col[:, None] * inv
    ang = jnp.concatenate([ang_r, ang_r, ang_c, ang_c], axis=-1)
    lane = jnp.arange(HEAD_DIM)
    sign = jnp.where((lane % ROPE_AXIS_DIM) < (ROPE_AXIS_DIM // 2), -1.0, 1.0).astype(F32)
    cos = jnp.concatenate([jnp.ones((n_ctx, HEAD_DIM), F32), jnp.cos(ang)], axis=0)
    sin = jnp.concatenate([jnp.zeros((n_ctx, HEAD_DIM), F32), jnp.sin(ang) * sign], axis=0)
    return cos, sin


def _dims(D, S, C, d_expert):
    ha = D // 256
    hkv = ha // 4
    hb = D // 512
    widths = [("ka", hkv * HEAD_DIM), ("va", hkv * HEAD_DIM), ("kb", hb * 2 * HEAD_DIM),
              ("vb", hb * 2 * HEAD_DIM), ("qa", ha * HEAD_DIM), ("qb", hb * 2 * HEAD_DIM),
              ("gate", 2 * D)]
    seg, off = {}, 0
    for name, wdt in widths:
        seg[name] = (off, off + wdt)
        off += wdt
    R = C + S
    tm = _pick_tile(R, (768, 512, 384, 256, 128))
    tn = _pick_tile(D, (512, 256, 128))
    tn_in = _pick_tile(hkv * HEAD_DIM, (512, 256, 128))
    tm_in = _pick_tile(R, (1408, 768, 512, 384, 256, 128))
    tm_comb = _pick_tile(R, (256, 128))
    return dict(ha=ha, hkv=hkv, group=ha // hkv, hb=hb, seg=seg, n_cols=off, rows=R, n_ctx=C,
                tm=tm, tn=tn, tn_in=tn_in, tm_in=tm_in, tm_exp=256, tm_comb=tm_comb)


def kernel(x, c, ctx, c_ctx, ada_down, ada_up, ada_bias, norm_mix, norm_ffn, w_in, sink_logit,
           lam_q1, lam_k1, lam_q2, lam_k2, subln_gain, w_branch_a, w_branch_b, w_out,
           w_router, b_router, w_exp_gate, w_exp_up, w_exp_down, norm_final):
    B, S, D = x.shape
    C = ctx.shape[1]
    depth = w_in.shape[0]
    dims = _dims(D, S, C, w_exp_gate.shape[-1])
    R = dims["rows"]
    M = B * R

    cond = jnp.concatenate([c, c_ctx[None], jnp.zeros((8 - (B + 1) % 8 if (B + 1) % 8 else 0, D), F32)])
    mods = _adaln(cond, ada_down, ada_up, ada_bias)
    mods = mods.reshape(depth, mods.shape[1], N_MOD, D)
    lat = mods[:, :B]
    cx = jnp.broadcast_to(mods[:, B:B + 1], lat.shape)
    mod = jnp.stack([cx, lat], axis=2)

    cos_tab, sin_tab = _rope_tables(S, C)
    wr_t = w_router.T.astype(BF16)
    xa = jnp.concatenate([ctx, x], axis=1)
    w_in_b, wa_b, wb_b, wo_b = (w.astype(BF16) for w in (w_in, w_branch_a, w_branch_b, w_out))
    wg_b, wu_b, wd_b = (w.astype(BF16) for w in (w_exp_gate, w_exp_up, w_exp_down))
    n_experts = w_router.shape[1]
    tm_exp = dims["tm_exp"]
    n_exp_tiles = -(-(2 * M + n_experts * (tm_exp - 1)) // tm_exp)

    for l in range(depth):
        lam_init = 0.8 - 0.6 * math.exp(-0.3 * l)
        m = [mod[l, :, :, k] for k in range(N_MOD)]
        h = _norm_mod(xa, norm_mix[l], m[0], m[1], C)
        px = _inproj(h.reshape(M, D), w_in_b, l, cos_tab, sin_tab, dims)
        px3 = px.reshape(B, R, -1)
        ya = _win_attn(px3, sink_logit[l], dims)
        yb = _diff_attn(px3, lam_q1[l], lam_k1[l], lam_q2[l], lam_k2[l], subln_gain[l], lam_init, dims)
        mg = _merge(ya.reshape(M, -1), yb.reshape(M, -1), wa_b, wb_b, l, px, dims)
        x2d = _outproj(mg, wo_b, l, xa.reshape(M, D), m[2], dims)
        xa = x2d.reshape(B, R, D)
        h, meta, counts = _norm_router(xa, norm_ffn[l], m[3], m[4], wr_t, b_router, C)
        pos1, pos2, tok3, w_sorted, tile_expert, n_valid = _dispatch_plan(meta, counts, tm_exp, n_exp_tiles)
        y_sorted = _experts(h.reshape(M, D), tok3, w_sorted, tile_expert, n_valid,
                            wg_b, wu_b, wd_b, l, tm_exp)
        xa = _combine(x2d, y_sorted, pos1, pos2, m[5], dims).reshape(B, R, D)
    return _final_norm(xa, norm_final, C)
```

```python
import functools
import math

import jax
import jax.numpy as jnp
from jax import lax
from jax.experimental import pallas as pl
from jax.experimental.pallas import tpu as pltpu

HEAD_DIM = 128
GRID_W = 64
ROPE_AXIS_DIM = HEAD_DIM // 2
ROPE_THETA = 10000.0
BLOCK = 128
N_GROUPS = 4
ROUTED_SCALE = 1.0
NORM_EPS = 1e-6
NEG_INF = -1e30
N_MOD = 6
Q_SCALE = HEAD_DIM ** -0.5
LOG2E = math.log2(math.e)
MIN_SOFTMAX_DENOM = 2.0 ** -80

F32 = jnp.float32
BF16 = jnp.bfloat16

VMEM_LIMIT_BYTES = 56 * 1024 * 1024


def _cparams(n_axes):
    return pltpu.CompilerParams(dimension_semantics=("arbitrary",) * n_axes,
                                vmem_limit_bytes=VMEM_LIMIT_BYTES)


def _pick_tile(total, candidates):
    for t in candidates:
        if total % t == 0:
            return t
    raise ValueError(f"no tile in {candidates} divides {total}")


def _dot(a, b):
    return jnp.dot(a, b, preferred_element_type=F32)


def _dot_nt(a, b):
    return lax.dot_general(a, b, (((1,), (1,)), ((), ())), preferred_element_type=F32)


def _sigmoid(x):
    return 1.0 / (1.0 + jnp.exp(-x))


def _adaln_kernel(cond_ref, down_ref, up_ref, bias_ref, o_ref):
    cond = cond_ref[...]
    s = (cond * _sigmoid(cond)).astype(BF16)
    t = _dot(s, down_ref[0].astype(BF16)).astype(BF16)
    o_ref[0] = _dot(t, up_ref[0].astype(BF16)) + bias_ref[0]


def _adaln(cond, down, up, bias):
    L, D, rank = down.shape
    n_out = up.shape[-1]
    rows = cond.shape[0]
    tn = _pick_tile(n_out, (2048, 1024, 512, 256, 128))
    return pl.pallas_call(
        _adaln_kernel,
        out_shape=jax.ShapeDtypeStruct((L, rows, n_out), F32),
        grid=(L, n_out // tn),
        in_specs=[
            pl.BlockSpec((rows, D), lambda l, j: (0, 0)),
            pl.BlockSpec((1, D, rank), lambda l, j: (l, 0, 0)),
            pl.BlockSpec((1, rank, tn), lambda l, j: (l, 0, j)),
            pl.BlockSpec((1, 1, tn), lambda l, j: (l, 0, j)),
        ],
        out_specs=pl.BlockSpec((1, rows, tn), lambda l, j: (l, 0, j)),
        compiler_params=_cparams(2),
        name="adaln",
    )(cond, down, up, bias.reshape(L, 1, n_out))


def _norm_mod_kernel(x_ref, gain_ref, shift_ref, scale_ref, o_ref, *, tr, n_ctx):
    x = x_ref[0]
    var = jnp.mean(x * x, axis=-1, keepdims=True)
    y = x * lax.rsqrt(var + NORM_EPS) * gain_ref[...]
    row = pl.program_id(1) * tr + lax.broadcasted_iota(jnp.int32, (tr, 1), 0)
    is_ctx = row < n_ctx
    shift = jnp.where(is_ctx, shift_ref[0, 0:1, :], shift_ref[0, 1:2, :])
    scale = jnp.where(is_ctx, scale_ref[0, 0:1, :], scale_ref[0, 1:2, :])
    o_ref[0] = (y * (1.0 + scale) + shift).astype(o_ref.dtype)


def _norm_mod(xa, gain, shift, scale, n_ctx):
    B, R, D = xa.shape
    tr = _pick_tile(R, (256, 128))
    return pl.pallas_call(
        functools.partial(_norm_mod_kernel, tr=tr, n_ctx=n_ctx),
        out_shape=jax.ShapeDtypeStruct((B, R, D), BF16),
        grid=(B, R // tr),
        in_specs=[
            pl.BlockSpec((1, tr, D), lambda b, i: (b, i, 0)),
            pl.BlockSpec((1, D), lambda b, i: (0, 0)),
            pl.BlockSpec((1, 2, D), lambda b, i: (b, 0, 0)),
            pl.BlockSpec((1, 2, D), lambda b, i: (b, 0, 0)),
        ],
        out_specs=pl.BlockSpec((1, tr, D), lambda b, i: (b, i, 0)),
        compiler_params=_cparams(2),
        name="norm_mod",
    )(xa, gain.reshape(1, D), shift, scale)


def _final_norm_kernel(x_ref, gain_ref, o_ref):
    x = x_ref[0]
    var = jnp.mean(x * x, axis=-1, keepdims=True)
    o_ref[0] = x * lax.rsqrt(var + NORM_EPS) * gain_ref[...]


def _final_norm(xa, gain, n_ctx):
    B, R, D = xa.shape
    S = R - n_ctx
    tr = _pick_tile(math.gcd(S, n_ctx), (256, 128))
    off = n_ctx // tr
    return pl.pallas_call(
        _final_norm_kernel,
        out_shape=jax.ShapeDtypeStruct((B, S, D), F32),
        grid=(B, S // tr),
        in_specs=[
            pl.BlockSpec((1, tr, D), lambda b, i: (b, i + off, 0)),
            pl.BlockSpec((1, D), lambda b, i: (0, 0)),
        ],
        out_specs=pl.BlockSpec((1, tr, D), lambda b, i: (b, i, 0)),
        compiler_params=_cparams(2),
        name="final_norm",
    )(xa, gain.reshape(1, D))


def _rope_cols(a, cos, sin_signed, lane_lo):
    rot = jnp.where(lane_lo, pltpu.roll(a, HEAD_DIM - 32, 1), pltpu.roll(a, 32, 1))
    return a * cos + rot * sin_signed


def _in_ranges(x, ranges):
    return functools.reduce(jnp.logical_or, [(x >= lo) & (x < hi) for lo, hi in ranges])


def _inproj_kernel(h_ref, w_ref, cos_ref, sin_ref, o_ref, *, tn, sub, rope_ranges, qa_range, qb_range):
    j = pl.program_id(1)
    chunks = tn // HEAD_DIM
    first = j * chunks
    any_roped = functools.reduce(
        jnp.logical_or, [_in_ranges(first + c, rope_ranges) for c in range(chunks)])

    @pl.when(any_roped)
    def _():
        h = h_ref[...]
        cos = cos_ref[...]
        sin_signed = sin_ref[...]
        lane = lax.broadcasted_iota(jnp.int32, (1, HEAD_DIM), 1)
        lane_lo = (lane % ROPE_AXIS_DIM) < (ROPE_AXIS_DIM // 2)
        for c0 in range(0, tn, sub):
            acc = _dot(h, w_ref[0, :, c0:c0 + sub])
            for c in range(sub // HEAD_DIM):
                g = first + (c0 // HEAD_DIM + c)
                roped = _in_ranges(g, rope_ranges)
                mult = jnp.where(_in_ranges(g, (qa_range,)), Q_SCALE,
                                 jnp.where(_in_ranges(g, (qb_range,)), Q_SCALE * LOG2E, 1.0)).astype(F32)
                cos_g = jnp.where(roped, cos, 1.0) * mult
                sin_g = jnp.where(roped, sin_signed, 0.0) * mult
                a = acc[:, c * HEAD_DIM:(c + 1) * HEAD_DIM]
                col = c0 + c * HEAD_DIM
                o_ref[:, col:col + HEAD_DIM] = _rope_cols(a, cos_g, sin_g, lane_lo).astype(o_ref.dtype)

    @pl.when(jnp.logical_not(any_roped))
    def _():
        h = h_ref[...]
        for c0 in range(0, tn, sub):
            o_ref[:, c0:c0 + sub] = _dot(h, w_ref[0, :, c0:c0 + sub]).astype(o_ref.dtype)


def _inproj(h2d, w_all, layer, cos_tab, sin_tab, dims):
    M, D = h2d.shape
    N = w_all.shape[2]
    R = cos_tab.shape[0]
    tm = dims["tm_in"]
    tn = dims["tn_in"]
    tiles_per_batch = R // tm
    seg = dims["seg"]
    rng = lambda k: (seg[k][0] // HEAD_DIM, seg[k][1] // HEAD_DIM)
    rope_ranges = tuple(rng(k) for k in ("ka", "kb", "qa", "qb"))
    return pl.pallas_call(
        functools.partial(_inproj_kernel, tn=tn, sub=min(tn, 2 * HEAD_DIM), rope_ranges=rope_ranges,
                          qa_range=rng("qa"), qb_range=rng("qb")),
        out_shape=jax.ShapeDtypeStruct((M, N), BF16),
        grid=(M // tm, N // tn),
        in_specs=[
            pl.BlockSpec((tm, D), lambda i, j: (i, 0)),
            pl.BlockSpec((1, D, tn), lambda i, j: (layer, 0, j)),
            pl.BlockSpec((tm, HEAD_DIM), lambda i, j: (i % tiles_per_batch, 0)),
            pl.BlockSpec((tm, HEAD_DIM), lambda i, j: (i % tiles_per_batch, 0)),
        ],
        out_specs=pl.BlockSpec((tm, tn), lambda i, j: (i, j)),
        compiler_params=_cparams(2),
        name="inproj_rope",
    )(h2d, w_all, cos_tab, sin_tab)


def _win_attn_kernel(sink_ref, *refs, hkv, group, n_ctx_blocks, n_blocks):
    q_refs = refs[:hkv]
    kc_ref, vc_ref, kp_ref, ko_ref, kn_ref, vp_ref, vo_ref, vn_ref, o_ref = refs[hkv:]
    n = pl.program_id(1)
    rows = group * BLOCK
    qi = lax.broadcasted_iota(jnp.int32, (rows, BLOCK), 0) % BLOCK
    kj = lax.broadcasted_iota(jnp.int32, (rows, BLOCK), 1)
    is_lat = n >= n_ctx_blocks
    valid_p = jnp.logical_and(is_lat, n - 1 >= n_ctx_blocks)
    valid_n = jnp.logical_and(is_lat, n + 1 < n_blocks)
    mask = jnp.concatenate([
        jnp.logical_and(kj >= qi, valid_p),
        jnp.logical_and(kj >= 0, is_lat),
        jnp.logical_and(kj <= qi, valid_n)], axis=1)
    for h in range(hkv):
        cs = slice(h * HEAD_DIM, (h + 1) * HEAD_DIM)
        q = jnp.concatenate(
            [q_refs[h][0, :, g * HEAD_DIM:(g + 1) * HEAD_DIM] for g in range(group)], axis=0)
        sink = jnp.concatenate(
            [jnp.full((BLOCK, 1), sink_ref[h * group + g], F32) for g in range(group)], axis=0)
        k_band = jnp.concatenate([kp_ref[0, :, cs], ko_ref[0, :, cs], kn_ref[0, :, cs]], axis=0)
        v_band = jnp.concatenate([vp_ref[0, :, cs], vo_ref[0, :, cs], vn_ref[0, :, cs]], axis=0)
        s_c = _dot_nt(q, kc_ref[0, :, cs])
        s_b = jnp.where(mask, _dot_nt(q, k_band), NEG_INF)
        m = jnp.maximum(jnp.maximum(jnp.max(s_c, axis=-1, keepdims=True),
                                    jnp.max(s_b, axis=-1, keepdims=True)), sink)
        e_c = jnp.exp(s_c - m)
        e_b = jnp.exp(s_b - m)
        denom = (jnp.sum(e_c, axis=-1, keepdims=True) + jnp.sum(e_b, axis=-1, keepdims=True)
                 + jnp.exp(sink - m))
        out = (_dot(e_b.astype(BF16), v_band) + _dot(e_c.astype(BF16), vc_ref[0, :, cs])) / denom
        for g in range(group):
            col = (h * group + g) * HEAD_DIM
            o_ref[0, :, col:col + HEAD_DIM] = out[g * BLOCK:(g + 1) * BLOCK].astype(o_ref.dtype)


def _win_attn(px, sink, dims):
    B, R, _ = px.shape
    seg = dims["seg"]
    hkv, group, n_ctx = dims["hkv"], dims["group"], dims["n_ctx"]
    nb = R // BLOCK
    ncb = n_ctx // BLOCK
    qw = group * HEAD_DIM
    kw = hkv * HEAD_DIM
    k0 = seg["ka"][0] // kw
    v0 = seg["va"][0] // kw
    q0 = seg["qa"][0] // qw
    assert seg["qa"][0] % qw == 0 and seg["ka"][0] % kw == 0 and seg["va"][0] % kw == 0

    def q_spec(h):
        return pl.BlockSpec((1, BLOCK, qw), lambda b, n: (b, n, q0 + h))

    def band_spec(col, shift):
        return pl.BlockSpec((1, BLOCK, kw), lambda b, n: (b, jnp.clip(n + shift, 0, nb - 1), col))

    return pl.pallas_call(
        functools.partial(_win_attn_kernel, hkv=hkv, group=group, n_ctx_blocks=ncb, n_blocks=nb),
        out_shape=jax.ShapeDtypeStruct((B, R, hkv * qw), BF16),
        grid=(B, nb),
        in_specs=[pl.BlockSpec(memory_space=pltpu.SMEM)]
        + [q_spec(h) for h in range(hkv)]
        + [pl.BlockSpec((1, n_ctx, kw), lambda b, n: (b, 0, k0)),
           pl.BlockSpec((1, n_ctx, kw), lambda b, n: (b, 0, v0)),
           band_spec(k0, -1), band_spec(k0, 0), band_spec(k0, 1),
           band_spec(v0, -1), band_spec(v0, 0), band_spec(v0, 1)],
        out_specs=pl.BlockSpec((1, BLOCK, hkv * qw), lambda b, n: (b, n, 0)),
        compiler_params=_cparams(2),
        name="window_attn",
    )(sink, *([px] * (hkv + 8)))


def _diff_attn_kernel(lq1_ref, lk1_ref, lq2_ref, lk2_ref, gain_ref, q_ref, k_ref, v_ref, o_ref,
                      acc1_ref, acc2_ref, s_ref, *, tq, tk, n_ctx, n_rows, lam_init):
    lam = (jnp.exp(jnp.sum(lq1_ref[...] * lk1_ref[...], axis=-1, keepdims=True))
           - jnp.exp(jnp.sum(lq2_ref[...] * lk2_ref[...], axis=-1, keepdims=True)) + lam_init)
    gain = gain_ref[...] * (1.0 - lam_init)

    def finish(q_start, nq, l1, l2):
        o = acc1_ref[pl.ds(0, nq), :] / l1 - lam * (acc2_ref[pl.ds(0, nq), :] / l2)
        var = jnp.mean(o * o, axis=-1, keepdims=True)
        y = o * lax.rsqrt(var + NORM_EPS) * gain
        o_ref[0, pl.ds(q_start, nq), :] = y.astype(o_ref.dtype)

    def attend_online(q_start, nq, kv_len, chunk):
        q1 = q_ref[0, pl.ds(q_start, nq), :HEAD_DIM]
        q2 = q_ref[0, pl.ds(q_start, nq), HEAD_DIM:]
        accs = (acc1_ref.at[pl.ds(0, nq), :], acc2_ref.at[pl.ds(0, nq), :])

        def step(t, carry):
            start = pl.multiple_of(t * chunk, chunk)
            k = k_ref[0, pl.ds(start, chunk), :]
            v = v_ref[0, pl.ds(start, chunk), :]
            out = []
            for which, (q, acc) in enumerate(zip((q1, q2), accs)):
                m, l = carry[2 * which], carry[2 * which + 1]
                s = _dot_nt(q, k[:, which * HEAD_DIM:(which + 1) * HEAD_DIM])
                n = jnp.maximum(m, jnp.max(s, axis=-1, keepdims=True))
                alpha = jnp.exp2(m - n)
                p = jnp.exp2(s - n)
                acc[...] = alpha * acc[...] + _dot(p.astype(BF16), v)
                out += [n, alpha * l + jnp.sum(p, axis=-1, keepdims=True)]
            return tuple(out)

        for acc in accs:
            acc[...] = jnp.zeros((nq, 2 * HEAD_DIM), F32)
        carry = (jnp.full((nq, 1), -jnp.inf, F32), jnp.zeros((nq, 1), F32)) * 2
        if kv_len == chunk:
            carry = step(0, carry)
        else:
            carry = lax.fori_loop(0, kv_len // chunk, step, carry)
        finish(q_start, nq, carry[1], carry[3])

    def attend_bounded(q_start, nq, kv_len, chunk, key_norm):
        q1 = q_ref[0, pl.ds(q_start, nq), :HEAD_DIM]
        q2 = q_ref[0, pl.ds(q_start, nq), HEAD_DIM:]
        qs = (q1, q2)
        accs = (acc1_ref.at[pl.ds(0, nq), :], acc2_ref.at[pl.ds(0, nq), :])
        bound = []
        for q, kn in zip(qs, key_norm):
            qf = q.astype(F32)
            bound.append(jnp.sqrt(jnp.sum(qf * qf, axis=-1, keepdims=True)) * kn)

        def scores(t, slot):
            k = k_ref[0, pl.ds(pl.multiple_of(t * chunk, chunk), chunk), :]
            for which in range(2):
                s_ref[slot, which, pl.ds(0, nq), pl.ds(0, chunk)] = _dot_nt(
                    qs[which], k[:, which * HEAD_DIM:(which + 1) * HEAD_DIM])

        def softmax_pv(t, slot, ls):
            v = v_ref[0, pl.ds(pl.multiple_of(t * chunk, chunk), chunk), :]
            out = []
            for which in range(2):
                p = jnp.exp2(s_ref[slot, which, pl.ds(0, nq), pl.ds(0, chunk)] - bound[which])
                accs[which][...] += _dot(p.astype(BF16), v)
                out.append(ls[which] + jnp.sum(p, axis=-1, keepdims=True))
            return tuple(out)

        for acc in accs:
            acc[...] = jnp.zeros((nq, 2 * HEAD_DIM), F32)
        ls = (jnp.zeros((nq, 1), F32),) * 2
        n_chunks = kv_len // chunk

        def pair(u, c):
            scores(2 * u + 1, 1)
            c = softmax_pv(2 * u, 0, c)
            scores(2 * u + 2, 0)
            return softmax_pv(2 * u + 1, 1, c)

        scores(0, 0)
        n_pairs = (n_chunks - 1) // 2
        if n_pairs > 0:
            ls = lax.fori_loop(0, n_pairs, pair, ls)
        if n_chunks % 2 == 0:
            scores(n_chunks - 1, 1)
            ls = softmax_pv(n_chunks - 2, 0, ls)
            ls = softmax_pv(n_chunks - 1, 1, ls)
        else:
            ls = softmax_pv(n_chunks - 1, 0, ls)
        return ls

    attend_online(0, n_ctx, n_ctx, n_ctx)

    def max_norm(t, mx):
        k = k_ref[0, pl.ds(pl.multiple_of(t * tk, tk), tk), :].astype(F32)
        sq = k * k
        return tuple(
            jnp.maximum(mx[w], jnp.max(jnp.sum(sq[:, w * HEAD_DIM:(w + 1) * HEAD_DIM], axis=-1, keepdims=True),
                                       axis=0, keepdims=True))
            for w in range(2))

    key_sq = lax.fori_loop(0, n_rows // tk, max_norm, (jnp.zeros((1, 1), F32),) * 2)
    key_norm = tuple(jnp.sqrt(x) for x in key_sq)
    align = math.gcd(n_ctx, tq)

    def latent_tile(i, _):
        q_start = pl.multiple_of(n_ctx + i * tq, align)
        l1, l2 = attend_bounded(q_start, tq, n_rows, tk, key_norm)
        smallest = jnp.min(jnp.minimum(l1, l2), axis=0, keepdims=True)[0, 0]
        safe = smallest >= MIN_SOFTMAX_DENOM

        @pl.when(safe)
        def _():
            finish(q_start, tq, l1, l2)

        @pl.when(jnp.logical_not(safe))
        def _():
            attend_online(q_start, tq, n_rows, tk)

        return 0

    lax.fori_loop(0, (n_rows - n_ctx) // tq, latent_tile, 0)


def _diff_attn(px, lq1, lk1, lq2, lk2, gain, lam_init, dims):
    B, R, _ = px.shape
    seg = dims["seg"]
    hb, n_ctx = dims["hb"], dims["n_ctx"]
    w = 2 * HEAD_DIM
    tq = _pick_tile(R - n_ctx, (512, 256, 128))
    tk = _pick_tile(R, (768, 512, 384, 256, 128))
    assert n_ctx <= tq and n_ctx % 16 == 0
    q0, k0, v0 = seg["qb"][0] // w, seg["kb"][0] // w, seg["vb"][0] // w
    vec = lambda a: a.reshape(1, -1)
    small = lambda n: pl.BlockSpec((1, n), lambda b, h: (0, 0))
    return pl.pallas_call(
        functools.partial(_diff_attn_kernel, tq=tq, tk=tk, n_ctx=n_ctx, n_rows=R, lam_init=lam_init),
        out_shape=jax.ShapeDtypeStruct((B, R, hb * w), BF16),
        grid=(B, hb),
        in_specs=[
            small(HEAD_DIM), small(HEAD_DIM), small(HEAD_DIM), small(HEAD_DIM), small(w),
            pl.BlockSpec((1, R, w), lambda b, h: (b, 0, q0 + h)),
            pl.BlockSpec((1, R, w), lambda b, h: (b, 0, k0 + h)),
            pl.BlockSpec((1, R, w), lambda b, h: (b, 0, v0 + h)),
        ],
        out_specs=pl.BlockSpec((1, R, w), lambda b, h: (b, 0, h)),
        scratch_shapes=[pltpu.VMEM((tq, w), F32), pltpu.VMEM((tq, w), F32),
                        pltpu.VMEM((2, 2, tq, tk), F32)],
        compiler_params=_cparams(2),
        name="diff_attn",
    )(vec(lq1), vec(lk1), vec(lq2), vec(lk2), vec(gain), px, px, px)


def _merge_kernel(ya_ref, yb_ref, wa_ref, wb_ref, ga_ref, gb_ref, o_ref):
    pa = _dot(ya_ref[...], wa_ref[0])
    pb = _dot(yb_ref[...], wb_ref[0])
    m = _sigmoid(ga_ref[...].astype(F32)) * pa + _sigmoid(gb_ref[...].astype(F32)) * pb
    o_ref[...] = m.astype(o_ref.dtype)


def _merge(ya, yb, wa, wb, layer, px2d, dims):
    M = ya.shape[0]
    D = wa.shape[2]
    tm, tn = dims["tm"], dims["tn"]
    g0 = dims["seg"]["gate"][0] // tn
    g1 = g0 + D // tn
    return pl.pallas_call(
        _merge_kernel,
        out_shape=jax.ShapeDtypeStruct((M, D), BF16),
        grid=(M // tm, D // tn),
        in_specs=[
            pl.BlockSpec((tm, ya.shape[1]), lambda i, j: (i, 0)),
            pl.BlockSpec((tm, yb.shape[1]), lambda i, j: (i, 0)),
            pl.BlockSpec((1, wa.shape[1], tn), lambda i, j: (layer, 0, j)),
            pl.BlockSpec((1, wb.shape[1], tn), lambda i, j: (layer, 0, j)),
            pl.BlockSpec((tm, tn), lambda i, j: (i, g0 + j)),
            pl.BlockSpec((tm, tn), lambda i, j: (i, g1 + j)),
        ],
        out_specs=pl.BlockSpec((tm, tn), lambda i, j: (i, j)),
        compiler_params=_cparams(2),
        name="merge_branches",
    )(ya, yb, wa, wb, px2d, px2d)


def _row_gate(gate_ref, tile_in_batch, tm, n_ctx):
    row = tile_in_batch * tm + lax.broadcasted_iota(jnp.int32, (tm, 1), 0)
    return jnp.where(row < n_ctx, gate_ref[0, 0:1, :], gate_ref[0, 1:2, :])


def _outproj_kernel(m_ref, w_ref, x_ref, gate_ref, o_ref, *, tm, tiles_per_batch, n_ctx):
    gate = _row_gate(gate_ref, pl.program_id(0) % tiles_per_batch, tm, n_ctx)
    o_ref[...] = x_ref[...] + gate * _dot(m_ref[...], w_ref[0])


def _outproj(m, w, layer, x2d, gate, dims):
    M, D = x2d.shape
    tm, tn = dims["tm"], dims["tn"]
    tpb = dims["rows"] // tm
    return pl.pallas_call(
        functools.partial(_outproj_kernel, tm=tm, tiles_per_batch=tpb, n_ctx=dims["n_ctx"]),
        out_shape=jax.ShapeDtypeStruct((M, D), F32),
        grid=(M // tm, D // tn),
        in_specs=[
            pl.BlockSpec((tm, D), lambda i, j: (i, 0)),
            pl.BlockSpec((1, D, tn), lambda i, j: (layer, 0, j)),
            pl.BlockSpec((tm, tn), lambda i, j: (i, j)),
            pl.BlockSpec((1, 2, tn), lambda i, j: (i // tpb, 0, j)),
        ],
        out_specs=pl.BlockSpec((tm, tn), lambda i, j: (i, j)),
        input_output_aliases={2: 0},
        compiler_params=_cparams(2),
        name="outproj_residual",
    )(m, w, x2d, gate)


def _route(logits, bias, n_experts):
    per_group = n_experts // N_GROUPS
    scores = _sigmoid(logits)
    sel = scores + bias
    srow = [scores[e:e + 1, :] for e in range(n_experts)]
    row = [sel[e:e + 1, :] for e in range(n_experts)]
    best, gidx = None, None
    for g in range(N_GROUPS):
        mem = row[g * per_group:(g + 1) * per_group]
        gs = None
        for a in range(per_group):
            for b in range(a + 1, per_group):
                s = mem[a] + mem[b]
                gs = s if gs is None else jnp.maximum(gs, s)
        if best is None:
            best, gidx = gs, jnp.zeros_like(gs, dtype=jnp.int32)
        else:
            better = gs > best
            gidx = jnp.where(better, g, gidx)
            best = jnp.where(better, gs, best)
    masked = [jnp.where(gidx == (e // per_group), row[e], NEG_INF) for e in range(n_experts)]
    m1, i1 = masked[0], jnp.zeros_like(gidx)
    for e in range(1, n_experts):
        better = masked[e] > m1
        i1 = jnp.where(better, e, i1)
        m1 = jnp.where(better, masked[e], m1)
    m2 = jnp.full_like(m1, -jnp.inf)
    i2 = jnp.zeros_like(gidx)
    for e in range(n_experts):
        cand = jnp.where(i1 == e, -jnp.inf, masked[e])
        better = cand > m2
        i2 = jnp.where(better, e, i2)
        m2 = jnp.where(better, cand, m2)
    w1 = functools.reduce(jnp.add, [jnp.where(i1 == e, srow[e], 0.0) for e in range(n_experts)])
    w2 = functools.reduce(jnp.add, [jnp.where(i2 == e, srow[e], 0.0) for e in range(n_experts)])
    tot = w1 + w2
    return i1, i2, w1 / tot * ROUTED_SCALE, w2 / tot * ROUTED_SCALE


META_ROWS = 8


def _norm_router_kernel(x_ref, gain_ref, shift_ref, scale_ref, wr_ref, bias_ref,
                        h_ref, meta_ref, cnt_ref, carry_ref, *, tr, n_ctx, n_experts):
    first = jnp.logical_and(pl.program_id(0) == 0, pl.program_id(1) == 0)

    @pl.when(first)
    def _():
        carry_ref[...] = jnp.zeros_like(carry_ref)

    x = x_ref[0]
    var = jnp.mean(x * x, axis=-1, keepdims=True)
    y = x * lax.rsqrt(var + NORM_EPS) * gain_ref[...]
    row = pl.program_id(1) * tr + lax.broadcasted_iota(jnp.int32, (tr, 1), 0)
    is_ctx = row < n_ctx
    shift = jnp.where(is_ctx, shift_ref[0, 0:1, :], shift_ref[0, 1:2, :])
    scale = jnp.where(is_ctx, scale_ref[0, 0:1, :], scale_ref[0, 1:2, :])
    h = y * (1.0 + scale) + shift
    h_ref[0] = h

    logits = _dot_nt(wr_ref[...], h.astype(BF16))
    i1, i2, w1, w2 = _route(logits, bias_ref[...], n_experts)
    erow = lax.broadcasted_iota(jnp.int32, (n_experts, tr), 0)
    hit1 = erow == i1
    hit2 = erow == i2
    onehot = jnp.where(jnp.logical_or(hit1, hit2), 1.0, 0.0)
    before = (lax.broadcasted_iota(jnp.int32, (tr, tr), 0)
              < lax.broadcasted_iota(jnp.int32, (tr, tr), 1))
    prefix = _dot(onehot.astype(BF16), jnp.where(before, 1.0, 0.0).astype(BF16))
    seen = prefix + carry_ref[:, 0:1]
    r1 = jnp.sum(jnp.where(hit1, seen, 0.0), axis=0, keepdims=True)
    r2 = jnp.sum(jnp.where(hit2, seen, 0.0), axis=0, keepdims=True)
    carry_ref[...] = carry_ref[...] + jnp.sum(onehot, axis=1, keepdims=True)
    zero = jnp.zeros_like(w1)
    meta_ref[...] = jnp.concatenate(
        [i1.astype(F32), i2.astype(F32), w1, w2, r1, r2, zero, zero], axis=0)
    cnt_ref[...] = carry_ref[...]


def _norm_router(xa, gain, shift, scale, wr_t, bias, n_ctx):
    B, R, D = xa.shape
    E = wr_t.shape[0]
    tr = _pick_tile(R, (256, 128))
    nt = R // tr
    return pl.pallas_call(
        functools.partial(_norm_router_kernel, tr=tr, n_ctx=n_ctx, n_experts=E),
        out_shape=(jax.ShapeDtypeStruct((B, R, D), F32),
                   jax.ShapeDtypeStruct((META_ROWS, B * R), F32),
                   jax.ShapeDtypeStruct((E, HEAD_DIM), F32)),
        grid=(B, nt),
        in_specs=[
            pl.BlockSpec((1, tr, D), lambda b, i: (b, i, 0)),
            pl.BlockSpec((1, D), lambda b, i: (0, 0)),
            pl.BlockSpec((1, 2, D), lambda b, i: (b, 0, 0)),
            pl.BlockSpec((1, 2, D), lambda b, i: (b, 0, 0)),
            pl.BlockSpec((E, D), lambda b, i: (0, 0)),
            pl.BlockSpec((E, 1), lambda b, i: (0, 0)),
        ],
        out_specs=(pl.BlockSpec((1, tr, D), lambda b, i: (b, i, 0)),
                   pl.BlockSpec((META_ROWS, tr), lambda b, i: (0, b * nt + i)),
                   pl.BlockSpec((E, HEAD_DIM), lambda b, i: (0, 0))),
        scratch_shapes=[pltpu.VMEM((E, HEAD_DIM), F32)],
        compiler_params=_cparams(2),
        name="norm_router",
    )(xa, gain.reshape(1, D), shift, scale, wr_t, bias.reshape(E, 1).astype(F32))


def _row_gather(idx_vmem_ref, idx_smem, isem, src_hbm, buf, sem, slot, n_rows, inline=False):
    cp = pltpu.make_async_copy(idx_vmem_ref.at[0, 0], idx_smem.at[slot], isem)
    cp.start()
    cp.wait()

    def body(r, carry):
        t = idx_smem[slot, r]
        pltpu.make_async_copy(src_hbm.at[pl.ds(t, 1), :], buf.at[slot, pl.ds(r, 1), :],
                              sem.at[slot]).start()
        return carry

    if inline:
        for r in range(n_rows):
            body(r, 0)
    else:
        lax.fori_loop(0, n_rows, body, 0, unroll=8)


def _row_gather_wait(src_hbm, buf, sem, slot, n_rows):
    pltpu.make_async_copy(src_hbm.at[pl.ds(0, n_rows), :], buf.at[slot], sem.at[slot]).wait()


def _expert_kernel(te_ref, nv_ref, tok_ref, tok_next_ref, h_hbm, wg_ref, wu_ref, wd_ref,
                   o_ref, buf, idx_smem, sem, isem, *, tm):
    j = pl.program_id(0)
    n_valid = nv_ref[0]
    slot = j % 2

    @pl.when(j == 0)
    def _():
        _row_gather(tok_ref, idx_smem, isem, h_hbm, buf, sem, 0, tm)

    @pl.when(j < n_valid)
    def _():
        _row_gather_wait(h_hbm, buf, sem, slot, tm)
        h = buf[slot].astype(BF16)
        _row_gather(tok_next_ref, idx_smem, isem, h_hbm, buf, sem, 1 - slot, tm, inline=True)
        g = _dot(h, wg_ref[0, 0])
        u = _dot(h, wu_ref[0, 0])
        a = (g * _sigmoid(g) * u).astype(BF16)
        o_ref[...] = _dot(a, wd_ref[0, 0])

    @pl.when(j >= n_valid)
    def _():
        o_ref[...] = jnp.zeros_like(o_ref)

    @pl.when(j == n_valid)
    def _():
        _row_gather_wait(h_hbm, buf, sem, slot, tm)

    @pl.when(jnp.logical_and(j == pl.num_programs(0) - 1, j < n_valid))
    def _():
        _row_gather_wait(h_hbm, buf, sem, 1 - slot, tm)


def _experts(h2d, tok3, tile_expert, n_valid, wg, wu, wd, layer, tm):
    M, D = h2d.shape
    F = wg.shape[-1]
    n_tiles = tok3.shape[0]
    P = n_tiles * tm
    grid_spec = pltpu.PrefetchScalarGridSpec(
        num_scalar_prefetch=2,
        grid=(n_tiles,),
        in_specs=[
            pl.BlockSpec((1, 1, tm), lambda j, te, nv: (j, 0, 0)),
            pl.BlockSpec((1, 1, tm), lambda j, te, nv: (jnp.minimum(j + 1, n_tiles - 1), 0, 0)),
            pl.BlockSpec(memory_space=pl.ANY),
            pl.BlockSpec((1, 1, D, F), lambda j, te, nv: (layer, te[j], 0, 0)),
            pl.BlockSpec((1, 1, D, F), lambda j, te, nv: (layer, te[j], 0, 0)),
            pl.BlockSpec((1, 1, F, D), lambda j, te, nv: (layer, te[j], 0, 0)),
        ],
        out_specs=pl.BlockSpec((tm, D), lambda j, te, nv: (j, 0)),
        scratch_shapes=[pltpu.VMEM((2, tm, D), F32), pltpu.SMEM((2, tm), jnp.int32),
                        pltpu.SemaphoreType.DMA((2,)), pltpu.SemaphoreType.DMA],
    )
    return pl.pallas_call(
        functools.partial(_expert_kernel, tm=tm),
        out_shape=jax.ShapeDtypeStruct((P, D), F32),
        grid_spec=grid_spec,
        compiler_params=_cparams(1),
        name="moe_experts",
    )(tile_expert, n_valid, tok3, tok3, h2d, wg, wu, wd)


def _combine_kernel(p1_ref, p1n_ref, p2_ref, p2n_ref, y_hbm, x_ref, w_ref, gate_ref, *rest,
                    tm, tiles_per_batch, n_ctx, with_norm):
    if with_norm:
        gain_ref, shift_ref, scale_ref, o_ref, h_ref = rest[:5]
    else:
        o_ref = rest[0]
    buf1, buf2, idx1, idx2, sem1, sem2, isem = rest[-7:]
    j = pl.program_id(0)
    slot = j % 2

    @pl.when(j == 0)
    def _():
        _row_gather(p1_ref, idx1, isem, y_hbm, buf1, sem1, 0, tm)
        _row_gather(p2_ref, idx2, isem, y_hbm, buf2, sem2, 0, tm)

    @pl.when(j + 1 < pl.num_programs(0))
    def _():
        _row_gather(p1n_ref, idx1, isem, y_hbm, buf1, sem1, 1 - slot, tm)
        _row_gather(p2n_ref, idx2, isem, y_hbm, buf2, sem2, 1 - slot, tm)

    _row_gather_wait(y_hbm, buf1, sem1, slot, tm)
    _row_gather_wait(y_hbm, buf2, sem2, slot, tm)
    gate = _row_gate(gate_ref, j % tiles_per_batch, tm, n_ctx)
    y = w_ref[:, 0:1] * buf1[slot] + w_ref[:, 1:2] * buf2[slot]
    x = x_ref[...] + gate * y
    o_ref[...] = x
    if with_norm:
        var = jnp.mean(x * x, axis=-1, keepdims=True)
        xn = x * lax.rsqrt(var + NORM_EPS) * gain_ref[...]
        shift = _row_gate(shift_ref, j % tiles_per_batch, tm, n_ctx)
        scale = _row_gate(scale_ref, j % tiles_per_batch, tm, n_ctx)
        h_ref[...] = (xn * (1.0 + scale) + shift).astype(h_ref.dtype)


def _combine(x2d, y_sorted, pos1, pos2, w12, gate, dims, next_norm=None):
    M, D = x2d.shape
    tm = dims["tm_comb"]
    n_tiles = M // tm
    tpb = dims["rows"] // tm
    p1 = pos1.reshape(n_tiles, 1, tm)
    p2 = pos2.reshape(n_tiles, 1, tm)
    cur = pl.BlockSpec((1, 1, tm), lambda j: (j, 0, 0))
    nxt = pl.BlockSpec((1, 1, tm), lambda j: (jnp.minimum(j + 1, n_tiles - 1), 0, 0))
    rows = pl.BlockSpec((tm, D), lambda j: (j, 0))
    per_batch = pl.BlockSpec((1, 2, D), lambda j: (j // tpb, 0, 0))
    with_norm = next_norm is not None
    extra_in, extra_args = [], []
    out_shape = jax.ShapeDtypeStruct((M, D), F32)
    out_specs = rows
    if with_norm:
        gain, shift, scale = next_norm
        extra_in = [pl.BlockSpec((1, D), lambda j: (0, 0)), per_batch, per_batch]
        extra_args = [gain.reshape(1, D), shift, scale]
        out_shape = (out_shape, jax.ShapeDtypeStruct((M, D), BF16))
        out_specs = (rows, rows)
    return pl.pallas_call(
        functools.partial(_combine_kernel, tm=tm, tiles_per_batch=tpb, n_ctx=dims["n_ctx"],
                          with_norm=with_norm),
        out_shape=out_shape,
        grid=(n_tiles,),
        in_specs=[cur, nxt, cur, nxt,
                  pl.BlockSpec(memory_space=pl.ANY),
                  rows,
                  pl.BlockSpec((tm, 2), lambda j: (j, 0)),
                  per_batch] + extra_in,
        out_specs=out_specs,
        scratch_shapes=[pltpu.VMEM((2, tm, D), F32), pltpu.VMEM((2, tm, D), F32),
                        pltpu.SMEM((2, tm), jnp.int32), pltpu.SMEM((2, tm), jnp.int32),
                        pltpu.SemaphoreType.DMA((2,)), pltpu.SemaphoreType.DMA((2,)),
                        pltpu.SemaphoreType.DMA],
        input_output_aliases={5: 0},
        compiler_params=_cparams(1),
        name="moe_combine",
    )(p1, p1, p2, p2, y_sorted, x2d, w12, gate, *extra_args)


def _dispatch_plan(meta, counts, tm, n_tiles):
    M = meta.shape[1]
    i1 = meta[0].astype(jnp.int32)
    i2 = meta[1].astype(jnp.int32)
    r1 = meta[4].astype(jnp.int32)
    r2 = meta[5].astype(jnp.int32)
    cnt = counts[:, 0].astype(jnp.int32)
    padded = ((cnt + tm - 1) // tm) * tm
    seg_end = jnp.cumsum(padded)
    seg_start = seg_end - padded
    pos1 = seg_start[i1] + r1
    pos2 = seg_start[i2] + r2
    n_valid = seg_end[-1] // tm
    tile_start = jnp.arange(n_tiles, dtype=jnp.int32) * tm
    probe = jnp.minimum(tile_start, seg_end[-1] - tm)
    tile_expert = jnp.sum(probe[:, None] >= seg_end[None, :], axis=1).astype(jnp.int32)
    tok = jnp.arange(M, dtype=jnp.int32)
    tok_sorted = jnp.zeros((n_tiles * tm,), jnp.int32).at[jnp.concatenate([pos1, pos2])].set(
        jnp.concatenate([tok, tok]), unique_indices=True)
    w12 = jnp.stack([meta[2], meta[3]], axis=1)
    return (pos1, pos2, w12, tok_sorted.reshape(n_tiles, 1, tm), tile_expert,
            n_valid.reshape(1).astype(jnp.int32))


def _rope_tables(seq, n_ctx):
    rows = seq // GRID_W
    row = jnp.repeat(jnp.arange(rows), GRID_W).astype(F32)
    col = jnp.tile(jnp.arange(GRID_W), rows).astype(F32)
    inv = ROPE_THETA ** (-(jnp.arange(0, ROPE_AXIS_DIM, 2, dtype=F32) / ROPE_AXIS_DIM))
    ang_r = row[:, None] * inv
    ang_c = col[:, None] * inv
    ang = jnp.concatenate([ang_r, ang_r, ang_c, ang_c], axis=-1)
    lane = jnp.arange(HEAD_DIM)
    sign = jnp.where((lane % ROPE_AXIS_DIM) < (ROPE_AXIS_DIM // 2), -1.0, 1.0).astype(F32)
    cos = jnp.concatenate([jnp.ones((n_ctx, HEAD_DIM), F32), jnp.cos(ang)], axis=0)
    sin = jnp.concatenate([jnp.zeros((n_ctx, HEAD_DIM), F32), jnp.sin(ang) * sign], axis=0)
    return cos, sin


def _dims(D, S, C, d_expert):
    ha = D // 256
    hkv = ha // 4
    hb = D // 512
    widths = [("ka", hkv * HEAD_DIM), ("va", hkv * HEAD_DIM), ("kb", hb * 2 * HEAD_DIM),
              ("vb", hb * 2 * HEAD_DIM), ("qa", ha * HEAD_DIM), ("qb", hb * 2 * HEAD_DIM),
              ("gate", 2 * D)]
    seg, off = {}, 0
    for name, wdt in widths:
        seg[name] = (off, off + wdt)
        off += wdt
    R = C + S
    tm = _pick_tile(R, (768, 512, 384, 256, 128))
    tn = _pick_tile(D, (512, 256, 128))
    tn_in = _pick_tile(off, (1024, 512, 256, 128))
    tm_in = _pick_tile(R, (1056, 768, 512, 384, 256, 128))
    tm_comb = _pick_tile(R, (256, 128))
    return dict(ha=ha, hkv=hkv, group=ha // hkv, hb=hb, seg=seg, n_cols=off, rows=R, n_ctx=C,
                tm=tm, tn=tn, tn_in=tn_in, tm_in=tm_in, tm_exp=256, tm_comb=tm_comb)


def kernel(x, c, ctx, c_ctx, ada_down, ada_up, ada_bias, norm_mix, norm_ffn, w_in, sink_logit,
           lam_q1, lam_k1, lam_q2, lam_k2, subln_gain, w_branch_a, w_branch_b, w_out,
           w_router, b_router, w_exp_gate, w_exp_up, w_exp_down, norm_final):
    B, S, D = x.shape
    C = ctx.shape[1]
    depth = w_in.shape[0]
    dims = _dims(D, S, C, w_exp_gate.shape[-1])
    R = dims["rows"]
    M = B * R

    cond = jnp.concatenate([c, c_ctx[None], jnp.zeros((8 - (B + 1) % 8 if (B + 1) % 8 else 0, D), F32)])
    mods = _adaln(cond, ada_down, ada_up, ada_bias)
    mods = mods.reshape(depth, mods.shape[1], N_MOD, D)
    lat = mods[:, :B]
    cx = jnp.broadcast_to(mods[:, B:B + 1], lat.shape)
    mod = jnp.stack([cx, lat], axis=2)

    cos_tab, sin_tab = _rope_tables(S, C)
    wr_t = w_router.T.astype(BF16)
    xa = jnp.concatenate([ctx, x], axis=1)
    w_in_b, wa_b, wb_b, wo_b = (w.astype(BF16) for w in (w_in, w_branch_a, w_branch_b, w_out))
    wg_b, wu_b, wd_b = (w.astype(BF16) for w in (w_exp_gate, w_exp_up, w_exp_down))
    n_experts = w_router.shape[1]
    tm_exp = dims["tm_exp"]
    n_exp_tiles = -(-(2 * M + n_experts * (tm_exp - 1)) // tm_exp)

    mods_of = lambda l: [mod[l, :, :, k] for k in range(N_MOD)]
    h_mix = _norm_mod(xa, norm_mix[0], mods_of(0)[0], mods_of(0)[1], C).reshape(M, D)
    for l in range(depth):
        lam_init = 0.8 - 0.6 * math.exp(-0.3 * l)
        m = mods_of(l)
        px = _inproj(h_mix, w_in_b, l, cos_tab, sin_tab, dims)
        px3 = px.reshape(B, R, -1)
        ya = _win_attn(px3, sink_logit[l], dims)
        yb = _diff_attn(px3, lam_q1[l], lam_k1[l], lam_q2[l], lam_k2[l], subln_gain[l], lam_init, dims)
        mg = _merge(ya.reshape(M, -1), yb.reshape(M, -1), wa_b, wb_b, l, px, dims)
        x2d = _outproj(mg, wo_b, l, xa.reshape(M, D), m[2], dims)
        xa = x2d.reshape(B, R, D)
        h, meta, counts = _norm_router(xa, norm_ffn[l], m[3], m[4], wr_t, b_router, C)
        pos1, pos2, w12, tok3, tile_expert, n_valid = _dispatch_plan(meta, counts, tm_exp, n_exp_tiles)
        y_sorted = _experts(h.reshape(M, D), tok3, tile_expert, n_valid, wg_b, wu_b, wd_b, l, tm_exp)
        if l + 1 < depth:
            nm = mods_of(l + 1)
            x2d, h_mix = _combine(x2d, y_sorted, pos1, pos2, w12, m[5], dims,
                                  next_norm=(norm_mix[l + 1], nm[0], nm[1]))
        else:
            x2d = _combine(x2d, y_sorted, pos1, pos2, w12, m[5], dims)
        xa = x2d.reshape(B, R, D)
    return _final_norm(xa, norm_final, C)
```

```python
import functools
import math

import jax
import jax.numpy as jnp
from jax import lax
from jax.experimental import pallas as pl
from jax.experimental.pallas import tpu as pltpu

HEAD_DIM = 128
GRID_W = 64
ROPE_AXIS_DIM = HEAD_DIM // 2
ROPE_THETA = 10000.0
BLOCK = 128
N_GROUPS = 4
ROUTED_SCALE = 1.0
NORM_EPS = 1e-6
NEG_INF = -1e30
N_MOD = 6
Q_SCALE = HEAD_DIM ** -0.5
LOG2E = math.log2(math.e)
MIN_SOFTMAX_DENOM = 2.0 ** -80

F32 = jnp.float32
BF16 = jnp.bfloat16

VMEM_LIMIT_BYTES = 56 * 1024 * 1024


def _cparams(n_axes):
    return pltpu.CompilerParams(dimension_semantics=("arbitrary",) * n_axes,
                                vmem_limit_bytes=VMEM_LIMIT_BYTES)


def _pick_tile(total, candidates):
    for t in candidates:
        if total % t == 0:
            return t
    raise ValueError(f"no tile in {candidates} divides {total}")


def _dot(a, b):
    return jnp.dot(a, b, preferred_element_type=F32)


def _dot_nt(a, b):
    return lax.dot_general(a, b, (((1,), (1,)), ((), ())), preferred_element_type=F32)


def _sigmoid(x):
    return 1.0 / (1.0 + jnp.exp(-x))


def _adaln_kernel(cond_ref, down_ref, up_ref, bias_ref, o_ref):
    cond = cond_ref[...]
    s = (cond * _sigmoid(cond)).astype(BF16)
    t = _dot(s, down_ref[0].astype(BF16)).astype(BF16)
    o_ref[0] = _dot(t, up_ref[0].astype(BF16)) + bias_ref[0]


def _adaln(cond, down, up, bias):
    L, D, rank = down.shape
    n_out = up.shape[-1]
    rows = cond.shape[0]
    tn = _pick_tile(n_out, (2048, 1024, 512, 256, 128))
    return pl.pallas_call(
        _adaln_kernel,
        out_shape=jax.ShapeDtypeStruct((L, rows, n_out), F32),
        grid=(L, n_out // tn),
        in_specs=[
            pl.BlockSpec((rows, D), lambda l, j: (0, 0)),
            pl.BlockSpec((1, D, rank), lambda l, j: (l, 0, 0)),
            pl.BlockSpec((1, rank, tn), lambda l, j: (l, 0, j)),
            pl.BlockSpec((1, 1, tn), lambda l, j: (l, 0, j)),
        ],
        out_specs=pl.BlockSpec((1, rows, tn), lambda l, j: (l, 0, j)),
        compiler_params=_cparams(2),
        name="adaln",
    )(cond, down, up, bias.reshape(L, 1, n_out))


def _norm_mod_kernel(x_ref, gain_ref, shift_ref, scale_ref, o_ref, *, tr, n_ctx):
    x = x_ref[0]
    var = jnp.mean(x * x, axis=-1, keepdims=True)
    y = x * lax.rsqrt(var + NORM_EPS) * gain_ref[...]
    row = pl.program_id(1) * tr + lax.broadcasted_iota(jnp.int32, (tr, 1), 0)
    is_ctx = row < n_ctx
    shift = jnp.where(is_ctx, shift_ref[0, 0:1, :], shift_ref[0, 1:2, :])
    scale = jnp.where(is_ctx, scale_ref[0, 0:1, :], scale_ref[0, 1:2, :])
    o_ref[0] = (y * (1.0 + scale) + shift).astype(o_ref.dtype)


def _norm_mod(xa, gain, shift, scale, n_ctx):
    B, R, D = xa.shape
    tr = _pick_tile(R, (256, 128))
    return pl.pallas_call(
        functools.partial(_norm_mod_kernel, tr=tr, n_ctx=n_ctx),
        out_shape=jax.ShapeDtypeStruct((B, R, D), BF16),
        grid=(B, R // tr),
        in_specs=[
            pl.BlockSpec((1, tr, D), lambda b, i: (b, i, 0)),
            pl.BlockSpec((1, D), lambda b, i: (0, 0)),
            pl.BlockSpec((1, 2, D), lambda b, i: (b, 0, 0)),
            pl.BlockSpec((1, 2, D), lambda b, i: (b, 0, 0)),
        ],
        out_specs=pl.BlockSpec((1, tr, D), lambda b, i: (b, i, 0)),
        compiler_params=_cparams(2),
        name="norm_mod",
    )(xa, gain.reshape(1, D), shift, scale)


def _final_norm_kernel(x_ref, gain_ref, o_ref):
    x = x_ref[0]
    var = jnp.mean(x * x, axis=-1, keepdims=True)
    o_ref[0] = x * lax.rsqrt(var + NORM_EPS) * gain_ref[...]


def _final_norm(xa, gain, n_ctx):
    B, R, D = xa.shape
    S = R - n_ctx
    tr = _pick_tile(math.gcd(S, n_ctx), (256, 128))
    off = n_ctx // tr
    return pl.pallas_call(
        _final_norm_kernel,
        out_shape=jax.ShapeDtypeStruct((B, S, D), F32),
        grid=(B, S // tr),
        in_specs=[
            pl.BlockSpec((1, tr, D), lambda b, i: (b, i + off, 0)),
            pl.BlockSpec((1, D), lambda b, i: (0, 0)),
        ],
        out_specs=pl.BlockSpec((1, tr, D), lambda b, i: (b, i, 0)),
        compiler_params=_cparams(2),
        name="final_norm",
    )(xa, gain.reshape(1, D))


def _rope_cols(a, cos, sin_signed, lane_lo):
    rot = jnp.where(lane_lo, pltpu.roll(a, HEAD_DIM - 32, 1), pltpu.roll(a, 32, 1))
    return a * cos + rot * sin_signed


def _in_ranges(x, ranges):
    return functools.reduce(jnp.logical_or, [(x >= lo) & (x < hi) for lo, hi in ranges])


def _inproj_kernel(h_ref, w_ref, cos_ref, sin_ref, o_ref, *, tn, sub, rope_ranges, qa_range, qb_range):
    j = pl.program_id(1)
    chunks = tn // HEAD_DIM
    first = j * chunks
    any_roped = functools.reduce(
        jnp.logical_or, [_in_ranges(first + c, rope_ranges) for c in range(chunks)])

    @pl.when(any_roped)
    def _():
        h = h_ref[...]
        cos = cos_ref[...]
        sin_signed = sin_ref[...]
        lane = lax.broadcasted_iota(jnp.int32, (1, HEAD_DIM), 1)
        lane_lo = (lane % ROPE_AXIS_DIM) < (ROPE_AXIS_DIM // 2)
        for c0 in range(0, tn, sub):
            acc = _dot(h, w_ref[0, :, c0:c0 + sub])
            for c in range(sub // HEAD_DIM):
                g = first + (c0 // HEAD_DIM + c)
                roped = _in_ranges(g, rope_ranges)
                mult = jnp.where(_in_ranges(g, (qa_range,)), Q_SCALE,
                                 jnp.where(_in_ranges(g, (qb_range,)), Q_SCALE * LOG2E, 1.0)).astype(F32)
                cos_g = jnp.where(roped, cos, 1.0) * mult
                sin_g = jnp.where(roped, sin_signed, 0.0) * mult
                a = acc[:, c * HEAD_DIM:(c + 1) * HEAD_DIM]
                col = c0 + c * HEAD_DIM
                o_ref[:, col:col + HEAD_DIM] = _rope_cols(a, cos_g, sin_g, lane_lo).astype(o_ref.dtype)

    @pl.when(jnp.logical_not(any_roped))
    def _():
        h = h_ref[...]
        for c0 in range(0, tn, sub):
            o_ref[:, c0:c0 + sub] = _dot(h, w_ref[0, :, c0:c0 + sub]).astype(o_ref.dtype)


def _inproj(h2d, w_all, layer, cos_tab, sin_tab, dims):
    M, D = h2d.shape
    N = w_all.shape[2]
    R = cos_tab.shape[0]
    tm = dims["tm_in"]
    tn = dims["tn_in"]
    tiles_per_batch = R // tm
    seg = dims["seg"]
    rng = lambda k: (seg[k][0] // HEAD_DIM, seg[k][1] // HEAD_DIM)
    rope_ranges = tuple(rng(k) for k in ("ka", "kb", "qa", "qb"))
    return pl.pallas_call(
        functools.partial(_inproj_kernel, tn=tn, sub=min(tn, 2 * HEAD_DIM), rope_ranges=rope_ranges,
                          qa_range=rng("qa"), qb_range=rng("qb")),
        out_shape=jax.ShapeDtypeStruct((M, N), BF16),
        grid=(M // tm, N // tn),
        in_specs=[
            pl.BlockSpec((tm, D), lambda i, j: (i, 0)),
            pl.BlockSpec((1, D, tn), lambda i, j: (layer, 0, j)),
            pl.BlockSpec((tm, HEAD_DIM), lambda i, j: (i % tiles_per_batch, 0)),
            pl.BlockSpec((tm, HEAD_DIM), lambda i, j: (i % tiles_per_batch, 0)),
        ],
        out_specs=pl.BlockSpec((tm, tn), lambda i, j: (i, j)),
        compiler_params=_cparams(2),
        name="inproj_rope",
    )(h2d, w_all, cos_tab, sin_tab)


def _win_attn_kernel(sink_ref, *refs, hkv, group, n_ctx_blocks, n_blocks):
    q_refs = refs[:hkv]
    kc_ref, vc_ref, kp_ref, ko_ref, kn_ref, vp_ref, vo_ref, vn_ref, o_ref = refs[hkv:]
    n = pl.program_id(1)
    rows = group * BLOCK
    qi = lax.broadcasted_iota(jnp.int32, (rows, BLOCK), 0) % BLOCK
    kj = lax.broadcasted_iota(jnp.int32, (rows, BLOCK), 1)
    is_lat = n >= n_ctx_blocks
    valid_p = jnp.logical_and(is_lat, n - 1 >= n_ctx_blocks)
    valid_n = jnp.logical_and(is_lat, n + 1 < n_blocks)
    mask = jnp.concatenate([
        jnp.logical_and(kj >= qi, valid_p),
        jnp.logical_and(kj >= 0, is_lat),
        jnp.logical_and(kj <= qi, valid_n)], axis=1)
    for h in range(hkv):
        cs = slice(h * HEAD_DIM, (h + 1) * HEAD_DIM)
        q = jnp.concatenate(
            [q_refs[h][0, :, g * HEAD_DIM:(g + 1) * HEAD_DIM] for g in range(group)], axis=0)
        sink = jnp.concatenate(
            [jnp.full((BLOCK, 1), sink_ref[h * group + g], F32) for g in range(group)], axis=0)
        k_band = jnp.concatenate([kp_ref[0, :, cs], ko_ref[0, :, cs], kn_ref[0, :, cs]], axis=0)
        v_band = jnp.concatenate([vp_ref[0, :, cs], vo_ref[0, :, cs], vn_ref[0, :, cs]], axis=0)
        s_c = _dot_nt(q, kc_ref[0, :, cs])
        s_b = jnp.where(mask, _dot_nt(q, k_band), NEG_INF)
        m = jnp.maximum(jnp.maximum(jnp.max(s_c, axis=-1, keepdims=True),
                                    jnp.max(s_b, axis=-1, keepdims=True)), sink)
        e_c = jnp.exp(s_c - m)
        e_b = jnp.exp(s_b - m)
        denom = (jnp.sum(e_c, axis=-1, keepdims=True) + jnp.sum(e_b, axis=-1, keepdims=True)
                 + jnp.exp(sink - m))
        out = (_dot(e_b.astype(BF16), v_band) + _dot(e_c.astype(BF16), vc_ref[0, :, cs])) / denom
        for g in range(group):
            col = (h * group + g) * HEAD_DIM
            o_ref[0, :, col:col + HEAD_DIM] = out[g * BLOCK:(g + 1) * BLOCK].astype(o_ref.dtype)


def _win_attn(px, sink, dims):
    B, R, _ = px.shape
    seg = dims["seg"]
    hkv, group, n_ctx = dims["hkv"], dims["group"], dims["n_ctx"]
    nb = R // BLOCK
    ncb = n_ctx // BLOCK
    qw = group * HEAD_DIM
    kw = hkv * HEAD_DIM
    k0 = seg["ka"][0] // kw
    v0 = seg["va"][0] // kw
    q0 = seg["qa"][0] // qw
    assert seg["qa"][0] % qw == 0 and seg["ka"][0] % kw == 0 and seg["va"][0] % kw == 0

    def q_spec(h):
        return pl.BlockSpec((1, BLOCK, qw), lambda b, n: (b, n, q0 + h))

    def band_spec(col, shift):
        return pl.BlockSpec((1, BLOCK, kw), lambda b, n: (b, jnp.clip(n + shift, 0, nb - 1), col))

    return pl.pallas_call(
        functools.partial(_win_attn_kernel, hkv=hkv, group=group, n_ctx_blocks=ncb, n_blocks=nb),
        out_shape=jax.ShapeDtypeStruct((B, R, hkv * qw), BF16),
        grid=(B, nb),
        in_specs=[pl.BlockSpec(memory_space=pltpu.SMEM)]
        + [q_spec(h) for h in range(hkv)]
        + [pl.BlockSpec((1, n_ctx, kw), lambda b, n: (b, 0, k0)),
           pl.BlockSpec((1, n_ctx, kw), lambda b, n: (b, 0, v0)),
           band_spec(k0, -1), band_spec(k0, 0), band_spec(k0, 1),
           band_spec(v0, -1), band_spec(v0, 0), band_spec(v0, 1)],
        out_specs=pl.BlockSpec((1, BLOCK, hkv * qw), lambda b, n: (b, n, 0)),
        compiler_params=_cparams(2),
        name="window_attn",
    )(sink, *([px] * (hkv + 8)))


def _diff_attn_kernel(lq1_ref, lk1_ref, lq2_ref, lk2_ref, gain_ref, q_ref, k_ref, v_ref, o_ref,
                      acc1_ref, acc2_ref, s_ref, *, tq, tk, n_ctx, n_rows, lam_init):
    lam = (jnp.exp(jnp.sum(lq1_ref[...] * lk1_ref[...], axis=-1, keepdims=True))
           - jnp.exp(jnp.sum(lq2_ref[...] * lk2_ref[...], axis=-1, keepdims=True)) + lam_init)
    gain = gain_ref[...] * (1.0 - lam_init)

    def finish(q_start, nq, l1, l2):
        o = acc1_ref[pl.ds(0, nq), :] / l1 - lam * (acc2_ref[pl.ds(0, nq), :] / l2)
        var = jnp.mean(o * o, axis=-1, keepdims=True)
        y = o * lax.rsqrt(var + NORM_EPS) * gain
        o_ref[0, pl.ds(q_start, nq), :] = y.astype(o_ref.dtype)

    def attend_online(q_start, nq, kv_len, chunk):
        q1 = q_ref[0, pl.ds(q_start, nq), :HEAD_DIM]
        q2 = q_ref[0, pl.ds(q_start, nq), HEAD_DIM:]
        accs = (acc1_ref.at[pl.ds(0, nq), :], acc2_ref.at[pl.ds(0, nq), :])

        def step(t, carry):
            start = pl.multiple_of(t * chunk, chunk)
            k = k_ref[0, pl.ds(start, chunk), :]
            v = v_ref[0, pl.ds(start, chunk), :]
            out = []
            for which, (q, acc) in enumerate(zip((q1, q2), accs)):
                m, l = carry[2 * which], carry[2 * which + 1]
                s = _dot_nt(q, k[:, which * HEAD_DIM:(which + 1) * HEAD_DIM])
                n = jnp.maximum(m, jnp.max(s, axis=-1, keepdims=True))
                alpha = jnp.exp2(m - n)
                p = jnp.exp2(s - n)
                acc[...] = alpha * acc[...] + _dot(p.astype(BF16), v)
                out += [n, alpha * l + jnp.sum(p, axis=-1, keepdims=True)]
            return tuple(out)

        for acc in accs:
            acc[...] = jnp.zeros((nq, 2 * HEAD_DIM), F32)
        carry = (jnp.full((nq, 1), -jnp.inf, F32), jnp.zeros((nq, 1), F32)) * 2
        if kv_len == chunk:
            carry = step(0, carry)
        else:
            carry = lax.fori_loop(0, kv_len // chunk, step, carry)
        finish(q_start, nq, carry[1], carry[3])

    def attend_bounded(q_start, nq, kv_len, chunk, key_norm):
        qs = tuple(q_ref[0, pl.ds(q_start, nq), w * HEAD_DIM:(w + 1) * HEAD_DIM] for w in range(2))
        accs = (acc1_ref.at[pl.ds(0, nq), :], acc2_ref.at[pl.ds(0, nq), :])
        bound = []
        for q, kn in zip(qs, key_norm):
            qf = q.astype(F32)
            bound.append(jnp.sqrt(jnp.sum(qf * qf, axis=-1, keepdims=True)) * kn)

        def scores(t, slot):
            k = k_ref[0, pl.ds(pl.multiple_of(t * chunk, chunk), chunk), :]
            for which in range(2):
                s_ref[slot, which, pl.ds(0, nq), pl.ds(0, chunk)] = _dot_nt(
                    qs[which], k[:, which * HEAD_DIM:(which + 1) * HEAD_DIM])

        def softmax_pv(t, slot, ls):
            v = v_ref[0, pl.ds(pl.multiple_of(t * chunk, chunk), chunk), :]
            out = []
            for which in range(2):
                p = jnp.exp2(s_ref[slot, which, pl.ds(0, nq), pl.ds(0, chunk)] - bound[which])
                accs[which][...] += _dot(p.astype(BF16), v)
                out.append(ls[which] + jnp.sum(p, axis=-1, keepdims=True))
            return tuple(out)

        for acc in accs:
            acc[...] = jnp.zeros((nq, 2 * HEAD_DIM), F32)
        ls = (jnp.zeros((nq, 1), F32),) * 2
        n_chunks = kv_len // chunk

        def pair(u, c):
            scores(2 * u + 1, 1)
            c = softmax_pv(2 * u, 0, c)
            scores(2 * u + 2, 0)
            return softmax_pv(2 * u + 1, 1, c)

        scores(0, 0)
        n_pairs = (n_chunks - 1) // 2
        if n_pairs > 0:
            ls = lax.fori_loop(0, n_pairs, pair, ls)
        if n_chunks % 2 == 0:
            scores(n_chunks - 1, 1)
            ls = softmax_pv(n_chunks - 2, 0, ls)
            ls = softmax_pv(n_chunks - 1, 1, ls)
        else:
            ls = softmax_pv(n_chunks - 1, 0, ls)
        return ls

    attend_online(0, n_ctx, n_ctx, n_ctx)

    def max_norm(t, mx):
        k = k_ref[0, pl.ds(pl.multiple_of(t * tk, tk), tk), :].astype(F32)
        sq = k * k
        return tuple(
            jnp.maximum(mx[w], jnp.max(jnp.sum(sq[:, w * HEAD_DIM:(w + 1) * HEAD_DIM], axis=-1, keepdims=True),
                                       axis=0, keepdims=True))
            for w in range(2))

    key_sq = lax.fori_loop(0, n_rows // tk, max_norm, (jnp.zeros((1, 1), F32),) * 2)
    key_norm = tuple(jnp.sqrt(x) for x in key_sq)
    align = math.gcd(n_ctx, tq)

    def latent_tile(i, _):
        q_start = pl.multiple_of(n_ctx + i * tq, align)
        l1, l2 = attend_bounded(q_start, tq, n_rows, tk, key_norm)
        smallest = jnp.min(jnp.minimum(l1, l2), axis=0, keepdims=True)[0, 0]
        safe = smallest >= MIN_SOFTMAX_DENOM

        @pl.when(safe)
        def _():
            finish(q_start, tq, l1, l2)

        @pl.when(jnp.logical_not(safe))
        def _():
            attend_online(q_start, tq, n_rows, tk)

        return 0

    lax.fori_loop(0, (n_rows - n_ctx) // tq, latent_tile, 0)


def _diff_attn(px, lq1, lk1, lq2, lk2, gain, lam_init, dims):
    B, R, _ = px.shape
    seg = dims["seg"]
    hb, n_ctx = dims["hb"], dims["n_ctx"]
    w = 2 * HEAD_DIM
    tq = _pick_tile(R - n_ctx, (512, 256, 128))
    tk = _pick_tile(R, (768, 512, 384, 256, 128))
    assert n_ctx <= tq and n_ctx % 16 == 0
    q0, k0, v0 = seg["qb"][0] // w, seg["kb"][0] // w, seg["vb"][0] // w
    vec = lambda a: a.reshape(1, -1)
    small = lambda n: pl.BlockSpec((1, n), lambda b, h: (0, 0))
    head = lambda col: pl.BlockSpec((1, R, w), lambda b, h: (b, 0, col + h))
    return pl.pallas_call(
        functools.partial(_diff_attn_kernel, tq=tq, tk=tk, n_ctx=n_ctx, n_rows=R, lam_init=lam_init),
        out_shape=jax.ShapeDtypeStruct((B, R, hb * w), BF16),
        grid=(B, hb),
        in_specs=[
            small(HEAD_DIM), small(HEAD_DIM), small(HEAD_DIM), small(HEAD_DIM), small(w),
            head(q0), head(k0), head(v0),
        ],
        out_specs=head(0),
        scratch_shapes=[pltpu.VMEM((tq, w), F32), pltpu.VMEM((tq, w), F32),
                        pltpu.VMEM((2, 2, tq, tk), F32)],
        compiler_params=_cparams(2),
        name="diff_attn",
    )(vec(lq1), vec(lk1), vec(lq2), vec(lk2), vec(gain), px, px, px)


def _merge_kernel(ya_ref, yb_ref, wa_ref, wb_ref, ga_ref, gb_ref, o_ref):
    pa = _dot(ya_ref[...], wa_ref[0])
    pb = _dot(yb_ref[...], wb_ref[0])
    m = _sigmoid(ga_ref[...].astype(F32)) * pa + _sigmoid(gb_ref[...].astype(F32)) * pb
    o_ref[...] = m.astype(o_ref.dtype)


def _merge(ya, yb, wa, wb, layer, px2d, dims):
    M = ya.shape[0]
    D = wa.shape[2]
    tm, tn = dims["tm"], dims["tn"]
    g0 = dims["seg"]["gate"][0] // tn
    g1 = g0 + D // tn
    return pl.pallas_call(
        _merge_kernel,
        out_shape=jax.ShapeDtypeStruct((M, D), BF16),
        grid=(M // tm, D // tn),
        in_specs=[
            pl.BlockSpec((tm, ya.shape[1]), lambda i, j: (i, 0)),
            pl.BlockSpec((tm, yb.shape[1]), lambda i, j: (i, 0)),
            pl.BlockSpec((1, wa.shape[1], tn), lambda i, j: (layer, 0, j)),
            pl.BlockSpec((1, wb.shape[1], tn), lambda i, j: (layer, 0, j)),
            pl.BlockSpec((tm, tn), lambda i, j: (i, g0 + j)),
            pl.BlockSpec((tm, tn), lambda i, j: (i, g1 + j)),
        ],
        out_specs=pl.BlockSpec((tm, tn), lambda i, j: (i, j)),
        compiler_params=_cparams(2),
        name="merge_branches",
    )(ya, yb, wa, wb, px2d, px2d)


def _row_gate(gate_ref, tile_in_batch, tm, n_ctx):
    row = tile_in_batch * tm + lax.broadcasted_iota(jnp.int32, (tm, 1), 0)
    return jnp.where(row < n_ctx, gate_ref[0, 0:1, :], gate_ref[0, 1:2, :])


def _outproj_kernel(m_ref, w_ref, x_ref, gate_ref, o_ref, *, tm, tiles_per_batch, n_ctx):
    gate = _row_gate(gate_ref, pl.program_id(0) % tiles_per_batch, tm, n_ctx)
    o_ref[...] = x_ref[...] + gate * _dot(m_ref[...], w_ref[0])


def _outproj(m, w, layer, x2d, gate, dims):
    M, D = x2d.shape
    tm, tn = dims["tm"], dims["tn"]
    tpb = dims["rows"] // tm
    return pl.pallas_call(
        functools.partial(_outproj_kernel, tm=tm, tiles_per_batch=tpb, n_ctx=dims["n_ctx"]),
        out_shape=jax.ShapeDtypeStruct((M, D), F32),
        grid=(M // tm, D // tn),
        in_specs=[
            pl.BlockSpec((tm, D), lambda i, j: (i, 0)),
            pl.BlockSpec((1, D, tn), lambda i, j: (layer, 0, j)),
            pl.BlockSpec((tm, tn), lambda i, j: (i, j)),
            pl.BlockSpec((1, 2, tn), lambda i, j: (i // tpb, 0, j)),
        ],
        out_specs=pl.BlockSpec((tm, tn), lambda i, j: (i, j)),
        input_output_aliases={2: 0},
        compiler_params=_cparams(2),
        name="outproj_residual",
    )(m, w, x2d, gate)


def _route(logits, bias, n_experts):
    per_group = n_experts // N_GROUPS
    scores = _sigmoid(logits)
    sel = scores + bias
    srow = [scores[e:e + 1, :] for e in range(n_experts)]
    row = [sel[e:e + 1, :] for e in range(n_experts)]
    best, gidx = None, None
    for g in range(N_GROUPS):
        mem = row[g * per_group:(g + 1) * per_group]
        gs = None
        for a in range(per_group):
            for b in range(a + 1, per_group):
                s = mem[a] + mem[b]
                gs = s if gs is None else jnp.maximum(gs, s)
        if best is None:
            best, gidx = gs, jnp.zeros_like(gs, dtype=jnp.int32)
        else:
            better = gs > best
            gidx = jnp.where(better, g, gidx)
            best = jnp.where(better, gs, best)
    masked = [jnp.where(gidx == (e // per_group), row[e], NEG_INF) for e in range(n_experts)]
    m1, i1 = masked[0], jnp.zeros_like(gidx)
    for e in range(1, n_experts):
        better = masked[e] > m1
        i1 = jnp.where(better, e, i1)
        m1 = jnp.where(better, masked[e], m1)
    m2 = jnp.full_like(m1, -jnp.inf)
    i2 = jnp.zeros_like(gidx)
    for e in range(n_experts):
        cand = jnp.where(i1 == e, -jnp.inf, masked[e])
        better = cand > m2
        i2 = jnp.where(better, e, i2)
        m2 = jnp.where(better, cand, m2)
    w1 = functools.reduce(jnp.add, [jnp.where(i1 == e, srow[e], 0.0) for e in range(n_experts)])
    w2 = functools.reduce(jnp.add, [jnp.where(i2 == e, srow[e], 0.0) for e in range(n_experts)])
    tot = w1 + w2
    return i1, i2, w1 / tot * ROUTED_SCALE, w2 / tot * ROUTED_SCALE


META_ROWS = 8


def _norm_router_kernel(x_ref, gain_ref, shift_ref, scale_ref, wr_ref, bias_ref,
                        h_ref, meta_ref, cnt_ref, carry_ref, *, tr, n_ctx, n_experts):
    first = jnp.logical_and(pl.program_id(0) == 0, pl.program_id(1) == 0)

    @pl.when(first)
    def _():
        carry_ref[...] = jnp.zeros_like(carry_ref)

    x = x_ref[0]
    var = jnp.mean(x * x, axis=-1, keepdims=True)
    y = x * lax.rsqrt(var + NORM_EPS) * gain_ref[...]
    row = pl.program_id(1) * tr + lax.broadcasted_iota(jnp.int32, (tr, 1), 0)
    is_ctx = row < n_ctx
    shift = jnp.where(is_ctx, shift_ref[0, 0:1, :], shift_ref[0, 1:2, :])
    scale = jnp.where(is_ctx, scale_ref[0, 0:1, :], scale_ref[0, 1:2, :])
    h = y * (1.0 + scale) + shift
    h_ref[0] = h

    logits = _dot_nt(wr_ref[...], h.astype(BF16))
    i1, i2, w1, w2 = _route(logits, bias_ref[...], n_experts)
    erow = lax.broadcasted_iota(jnp.int32, (n_experts, tr), 0)
    hit1 = erow == i1
    hit2 = erow == i2
    onehot = jnp.where(jnp.logical_or(hit1, hit2), 1.0, 0.0)
    before = (lax.broadcasted_iota(jnp.int32, (tr, tr), 0)
              < lax.broadcasted_iota(jnp.int32, (tr, tr), 1))
    prefix = _dot(onehot.astype(BF16), jnp.where(before, 1.0, 0.0).astype(BF16))
    seen = prefix + carry_ref[:, 0:1]
    r1 = jnp.sum(jnp.where(hit1, seen, 0.0), axis=0, keepdims=True)
    r2 = jnp.sum(jnp.where(hit2, seen, 0.0), axis=0, keepdims=True)
    carry_ref[...] = carry_ref[...] + jnp.sum(onehot, axis=1, keepdims=True)
    zero = jnp.zeros_like(w1)
    meta_ref[...] = jnp.concatenate(
        [i1.astype(F32), i2.astype(F32), w1, w2, r1, r2, zero, zero], axis=0)
    cnt_ref[...] = carry_ref[...]


def _norm_router(xa, gain, shift, scale, wr_t, bias, n_ctx):
    B, R, D = xa.shape
    E = wr_t.shape[0]
    tr = _pick_tile(R, (256, 128))
    nt = R // tr
    return pl.pallas_call(
        functools.partial(_norm_router_kernel, tr=tr, n_ctx=n_ctx, n_experts=E),
        out_shape=(jax.ShapeDtypeStruct((B, R, D), F32),
                   jax.ShapeDtypeStruct((META_ROWS, B * R), F32),
                   jax.ShapeDtypeStruct((E, HEAD_DIM), F32)),
        grid=(B, nt),
        in_specs=[
            pl.BlockSpec((1, tr, D), lambda b, i: (b, i, 0)),
            pl.BlockSpec((1, D), lambda b, i: (0, 0)),
            pl.BlockSpec((1, 2, D), lambda b, i: (b, 0, 0)),
            pl.BlockSpec((1, 2, D), lambda b, i: (b, 0, 0)),
            pl.BlockSpec((E, D), lambda b, i: (0, 0)),
            pl.BlockSpec((E, 1), lambda b, i: (0, 0)),
        ],
        out_specs=(pl.BlockSpec((1, tr, D), lambda b, i: (b, i, 0)),
                   pl.BlockSpec((META_ROWS, tr), lambda b, i: (0, b * nt + i)),
                   pl.BlockSpec((E, HEAD_DIM), lambda b, i: (0, 0))),
        scratch_shapes=[pltpu.VMEM((E, HEAD_DIM), F32)],
        compiler_params=_cparams(2),
        name="norm_router",
    )(xa, gain.reshape(1, D), shift, scale, wr_t, bias.reshape(E, 1).astype(F32))


def _row_gather(idx_vmem_ref, idx_smem, isem, src_hbm, buf, sem, slot, n_rows):
    cp = pltpu.make_async_copy(idx_vmem_ref.at[0, 0], idx_smem.at[slot], isem)
    cp.start()
    cp.wait()

    def body(r, carry):
        t = idx_smem[slot, r]
        pltpu.make_async_copy(src_hbm.at[pl.ds(t, 1), :], buf.at[slot, pl.ds(r, 1), :],
                              sem.at[slot]).start()
        return carry

    lax.fori_loop(0, n_rows, body, 0, unroll=8)


def _row_gather_wait(src_hbm, buf, sem, slot, n_rows):
    pltpu.make_async_copy(src_hbm.at[pl.ds(0, n_rows), :], buf.at[slot], sem.at[slot]).wait()


def _expert_kernel(te_ref, nv_ref, tok_ref, tok_next_ref, h_hbm, wg_ref, wu_ref, wd_ref,
                   o_ref, buf, idx_smem, sem, isem, *, tm):
    j = pl.program_id(0)
    n_valid = nv_ref[0]
    slot = j % 2

    @pl.when(j == 0)
    def _():
        _row_gather(tok_ref, idx_smem, isem, h_hbm, buf, sem, 0, tm)

    @pl.when(j + 1 < n_valid)
    def _():
        _row_gather(tok_next_ref, idx_smem, isem, h_hbm, buf, sem, 1 - slot, tm)

    @pl.when(j < n_valid)
    def _():
        _row_gather_wait(h_hbm, buf, sem, slot, tm)
        h = buf[slot].astype(BF16)
        g = _dot(h, wg_ref[0, 0])
        u = _dot(h, wu_ref[0, 0])
        a = (g * _sigmoid(g) * u).astype(BF16)
        o_ref[...] = _dot(a, wd_ref[0, 0])

    @pl.when(j >= n_valid)
    def _():
        o_ref[...] = jnp.zeros_like(o_ref)


def _experts(h2d, tok3, tile_expert, n_valid, wg, wu, wd, layer, tm):
    M, D = h2d.shape
    F = wg.shape[-1]
    n_tiles = tok3.shape[0]
    P = n_tiles * tm
    grid_spec = pltpu.PrefetchScalarGridSpec(
        num_scalar_prefetch=2,
        grid=(n_tiles,),
        in_specs=[
            pl.BlockSpec((1, 1, tm), lambda j, te, nv: (j, 0, 0)),
            pl.BlockSpec((1, 1, tm), lambda j, te, nv: (jnp.minimum(j + 1, n_tiles - 1), 0, 0)),
            pl.BlockSpec(memory_space=pl.ANY),
            pl.BlockSpec((1, 1, D, F), lambda j, te, nv: (layer, te[j], 0, 0)),
            pl.BlockSpec((1, 1, D, F), lambda j, te, nv: (layer, te[j], 0, 0)),
            pl.BlockSpec((1, 1, F, D), lambda j, te, nv: (layer, te[j], 0, 0)),
        ],
        out_specs=pl.BlockSpec((tm, D), lambda j, te, nv: (j, 0)),
        scratch_shapes=[pltpu.VMEM((2, tm, D), F32), pltpu.SMEM((2, tm), jnp.int32),
                        pltpu.SemaphoreType.DMA((2,)), pltpu.SemaphoreType.DMA],
    )
    return pl.pallas_call(
        functools.partial(_expert_kernel, tm=tm),
        out_shape=jax.ShapeDtypeStruct((P, D), F32),
        grid_spec=grid_spec,
        compiler_params=_cparams(1),
        name="moe_experts",
    )(tile_expert, n_valid, tok3, tok3, h2d, wg, wu, wd)


def _combine_kernel(p1_ref, p1n_ref, p2_ref, p2n_ref, y_hbm, x_ref, w_ref, gate_ref, *rest,
                    tm, tiles_per_batch, n_ctx, with_norm):
    if with_norm:
        gain_ref, shift_ref, scale_ref, o_ref, h_ref = rest[:5]
    else:
        o_ref = rest[0]
    buf1, buf2, idx1, idx2, sem1, sem2, isem = rest[-7:]
    j = pl.program_id(0)
    slot = j % 2

    @pl.when(j == 0)
    def _():
        _row_gather(p1_ref, idx1, isem, y_hbm, buf1, sem1, 0, tm)
        _row_gather(p2_ref, idx2, isem, y_hbm, buf2, sem2, 0, tm)

    @pl.when(j + 1 < pl.num_programs(0))
    def _():
        _row_gather(p1n_ref, idx1, isem, y_hbm, buf1, sem1, 1 - slot, tm)
        _row_gather(p2n_ref, idx2, isem, y_hbm, buf2, sem2, 1 - slot, tm)

    _row_gather_wait(y_hbm, buf1, sem1, slot, tm)
    _row_gather_wait(y_hbm, buf2, sem2, slot, tm)
    gate = _row_gate(gate_ref, j % tiles_per_batch, tm, n_ctx)
    y = w_ref[:, 0:1] * buf1[slot] + w_ref[:, 1:2] * buf2[slot]
    x = x_ref[...] + gate * y
    o_ref[...] = x
    if with_norm:
        var = jnp.mean(x * x, axis=-1, keepdims=True)
        xn = x * lax.rsqrt(var + NORM_EPS) * gain_ref[...]
        shift = _row_gate(shift_ref, j % tiles_per_batch, tm, n_ctx)
        scale = _row_gate(scale_ref, j % tiles_per_batch, tm, n_ctx)
        h_ref[...] = (xn * (1.0 + scale) + shift).astype(h_ref.dtype)


def _combine(x2d, y_sorted, pos1, pos2, w12, gate, dims, next_norm=None):
    M, D = x2d.shape
    tm = dims["tm_comb"]
    n_tiles = M // tm
    tpb = dims["rows"] // tm
    p1 = pos1.reshape(n_tiles, 1, tm)
    p2 = pos2.reshape(n_tiles, 1, tm)
    cur = pl.BlockSpec((1, 1, tm), lambda j: (j, 0, 0))
    nxt = pl.BlockSpec((1, 1, tm), lambda j: (jnp.minimum(j + 1, n_tiles - 1), 0, 0))
    rows = pl.BlockSpec((tm, D), lambda j: (j, 0))
    per_batch = pl.BlockSpec((1, 2, D), lambda j: (j // tpb, 0, 0))
    with_norm = next_norm is not None
    extra_in, extra_args = [], []
    out_shape = jax.ShapeDtypeStruct((M, D), F32)
    out_specs = rows
    if with_norm:
        gain, shift, scale = next_norm
        extra_in = [pl.BlockSpec((1, D), lambda j: (0, 0)), per_batch, per_batch]
        extra_args = [gain.reshape(1, D), shift, scale]
        out_shape = (out_shape, jax.ShapeDtypeStruct((M, D), BF16))
        out_specs = (rows, rows)
    return pl.pallas_call(
        functools.partial(_combine_kernel, tm=tm, tiles_per_batch=tpb, n_ctx=dims["n_ctx"],
                          with_norm=with_norm),
        out_shape=out_shape,
        grid=(n_tiles,),
        in_specs=[cur, nxt, cur, nxt,
                  pl.BlockSpec(memory_space=pl.ANY),
                  rows,
                  pl.BlockSpec((tm, 2), lambda j: (j, 0)),
                  per_batch] + extra_in,
        out_specs=out_specs,
        scratch_shapes=[pltpu.VMEM((2, tm, D), F32), pltpu.VMEM((2, tm, D), F32),
                        pltpu.SMEM((2, tm), jnp.int32), pltpu.SMEM((2, tm), jnp.int32),
                        pltpu.SemaphoreType.DMA((2,)), pltpu.SemaphoreType.DMA((2,)),
                        pltpu.SemaphoreType.DMA],
        input_output_aliases={5: 0},
        compiler_params=_cparams(1),
        name="moe_combine",
    )(p1, p1, p2, p2, y_sorted, x2d, w12, gate, *extra_args)


def _dispatch_plan(meta, counts, tm, n_tiles):
    M = meta.shape[1]
    i1 = meta[0].astype(jnp.int32)
    i2 = meta[1].astype(jnp.int32)
    r1 = meta[4].astype(jnp.int32)
    r2 = meta[5].astype(jnp.int32)
    cnt = counts[:, 0].astype(jnp.int32)
    padded = ((cnt + tm - 1) // tm) * tm
    seg_end = jnp.cumsum(padded)
    seg_start = seg_end - padded
    pos1 = seg_start[i1] + r1
    pos2 = seg_start[i2] + r2
    n_valid = seg_end[-1] // tm
    tile_start = jnp.arange(n_tiles, dtype=jnp.int32) * tm
    probe = jnp.minimum(tile_start, seg_end[-1] - tm)
    tile_expert = jnp.sum(probe[:, None] >= seg_end[None, :], axis=1).astype(jnp.int32)
    tok = jnp.arange(M, dtype=jnp.int32)
    tok_sorted = jnp.zeros((n_tiles * tm,), jnp.int32).at[jnp.concatenate([pos1, pos2])].set(
        jnp.concatenate([tok, tok]), unique_indices=True)
    w12 = jnp.stack([meta[2], meta[3]], axis=1)
    return (pos1, pos2, w12, tok_sorted.reshape(n_tiles, 1, tm), tile_expert,
            n_valid.reshape(1).astype(jnp.int32))


def _rope_tables(seq, n_ctx):
    rows = seq // GRID_W
    row = jnp.repeat(jnp.arange(rows), GRID_W).astype(F32)
    col = jnp.tile(jnp.arange(GRID_W), rows).astype(F32)
    inv = ROPE_THETA ** (-(jnp.arange(0, ROPE_AXIS_DIM, 2, dtype=F32) / ROPE_AXIS_DIM))
    ang_r = row[:, None] * inv
    ang_c = col[:, None] * inv
    ang = jnp.concatenate([ang_r, ang_r, ang_c, ang_c], axis=-1)
    lane = jnp.arange(HEAD_DIM)
    sign = jnp.where((lane % ROPE_AXIS_DIM) < (ROPE_AXIS_DIM // 2), -1.0, 1.0).astype(F32)
    cos = jnp.concatenate([jnp.ones((n_ctx, HEAD_DIM), F32), jnp.cos(ang)], axis=0)
    sin = jnp.concatenate([jnp.zeros((n_ctx, HEAD_DIM), F32), jnp.sin(ang) * sign], axis=0)
    return cos, sin


def _dims(D, S, C, d_expert):
    ha = D // 256
    hkv = ha // 4
    hb = D // 512
    widths = [("ka", hkv * HEAD_DIM), ("va", hkv * HEAD_DIM), ("kb", hb * 2 * HEAD_DIM),
              ("vb", hb * 2 * HEAD_DIM), ("qa", ha * HEAD_DIM), ("qb", hb * 2 * HEAD_DIM),
              ("gate", 2 * D)]
    seg, off = {}, 0
    for name, wdt in widths:
        seg[name] = (off, off + wdt)
        off += wdt
    R = C + S
    tm = _pick_tile(R, (768, 512, 384, 256, 128))
    tn = _pick_tile(D, (512, 256, 128))
    tn_in = _pick_tile(off, (1024, 512, 256, 128))
    tm_in = _pick_tile(R, (1056, 768, 512, 384, 256, 128))
    tm_comb = _pick_tile(R, (256, 128))
    return dict(ha=ha, hkv=hkv, group=ha // hkv, hb=hb, seg=seg, n_cols=off, rows=R, n_ctx=C,
                tm=tm, tn=tn, tn_in=tn_in, tm_in=tm_in, tm_exp=256, tm_comb=tm_comb)


def kernel(x, c, ctx, c_ctx, ada_down, ada_up, ada_bias, norm_mix, norm_ffn, w_in, sink_logit,
           lam_q1, lam_k1, lam_q2, lam_k2, subln_gain, w_branch_a, w_branch_b, w_out,
           w_router, b_router, w_exp_gate, w_exp_up, w_exp_down, norm_final):
    B, S, D = x.shape
    C = ctx.shape[1]
    depth = w_in.shape[0]
    dims = _dims(D, S, C, w_exp_gate.shape[-1])
    R = dims["rows"]
    M = B * R

    cond = jnp.concatenate([c, c_ctx[None], jnp.zeros((8 - (B + 1) % 8 if (B + 1) % 8 else 0, D), F32)])
    mods = _adaln(cond, ada_down, ada_up, ada_bias)
    mods = mods.reshape(depth, mods.shape[1], N_MOD, D)
    lat = mods[:, :B]
    cx = jnp.broadcast_to(mods[:, B:B + 1], lat.shape)
    mod = jnp.stack([cx, lat], axis=2)

    cos_tab, sin_tab = _rope_tables(S, C)
    wr_t = w_router.T.astype(BF16)
    xa = jnp.concatenate([ctx, x], axis=1)
    w_in_b, wa_b, wb_b, wo_b = (w.astype(BF16) for w in (w_in, w_branch_a, w_branch_b, w_out))
    wg_b, wu_b, wd_b = (w.astype(BF16) for w in (w_exp_gate, w_exp_up, w_exp_down))
    n_experts = w_router.shape[1]
    tm_exp = dims["tm_exp"]
    n_exp_tiles = -(-(2 * M + n_experts * (tm_exp - 1)) // tm_exp)

    mods_of = lambda l: [mod[l, :, :, k] for k in range(N_MOD)]
    h_mix = _norm_mod(xa, norm_mix[0], mods_of(0)[0], mods_of(0)[1], C).reshape(M, D)
    for l in range(depth):
        lam_init = 0.8 - 0.6 * math.exp(-0.3 * l)
        m = mods_of(l)
        px = _inproj(h_mix, w_in_b, l, cos_tab, sin_tab, dims)
        px3 = px.reshape(B, R, -1)
        ya = _win_attn(px3, sink_logit[l], dims)
        yb = _diff_attn(px3, lam_q1[l], lam_k1[l], lam_q2[l], lam_k2[l], subln_gain[l], lam_init, dims)
        mg = _merge(ya.reshape(M, -1), yb.reshape(M, -1), wa_b, wb_b, l, px, dims)
        x2d = _outproj(mg, wo_b, l, xa.reshape(M, D), m[2], dims)
        xa = x2d.reshape(B, R, D)
        h, meta, counts = _norm_router(xa, norm_ffn[l], m[3], m[4], wr_t, b_router, C)
        pos1, pos2, w12, tok3, tile_expert, n_valid = _dispatch_plan(meta, counts, tm_exp, n_exp_tiles)
        y_sorted = _experts(h.reshape(M, D), tok3, tile_expert, n_valid, wg_b, wu_b, wd_b, l, tm_exp)
        if l + 1 < depth:
            nm = mods_of(l + 1)
            x2d, h_mix = _combine(x2d, y_sorted, pos1, pos2, w12, m[5], dims,
                                  next_norm=(norm_mix[l + 1], nm[0], nm[1]))
        else:
            x2d = _combine(x2d, y_sorted, pos1, pos2, w12, m[5], dims)
        xa = x2d.reshape(B, R, D)
    return _final_norm(xa, norm_final, C)
```

```python
import functools
import math

import jax
import jax.numpy as jnp
from jax import lax
from jax.experimental import pallas as pl
from jax.experimental.pallas import tpu as pltpu

HEAD_DIM = 128
GRID_W = 64
ROPE_AXIS_DIM = HEAD_DIM // 2
ROPE_THETA = 10000.0
BLOCK = 128
N_GROUPS = 4
ROUTED_SCALE = 1.0
NORM_EPS = 1e-6
NEG_INF = -1e30
N_MOD = 6
Q_SCALE = HEAD_DIM ** -0.5
LOG2E = math.log2(math.e)
MIN_SOFTMAX_DENOM = 2.0 ** -80

F32 = jnp.float32
BF16 = jnp.bfloat16

VMEM_LIMIT_BYTES = 56 * 1024 * 1024


def _cparams(n_axes):
    return pltpu.CompilerParams(dimension_semantics=("arbitrary",) * n_axes,
                                vmem_limit_bytes=VMEM_LIMIT_BYTES)


def _pick_tile(total, candidates):
    for t in candidates:
        if total % t == 0:
            return t
    raise ValueError(f"no tile in {candidates} divides {total}")


def _dot(a, b):
    return jnp.dot(a, b, preferred_element_type=F32)


def _dot_nt(a, b):
    return lax.dot_general(a, b, (((1,), (1,)), ((), ())), preferred_element_type=F32)


def _sigmoid(x):
    return 1.0 / (1.0 + jnp.exp(-x))


def _adaln_kernel(cond_ref, down_ref, up_ref, bias_ref, o_ref):
    cond = cond_ref[...]
    s = (cond * _sigmoid(cond)).astype(BF16)
    t = _dot(s, down_ref[0].astype(BF16)).astype(BF16)
    o_ref[0] = _dot(t, up_ref[0].astype(BF16)) + bias_ref[0]


def _adaln(cond, down, up, bias):
    L, D, rank = down.shape
    n_out = up.shape[-1]
    rows = cond.shape[0]
    tn = _pick_tile(n_out, (2048, 1024, 512, 256, 128))
    return pl.pallas_call(
        _adaln_kernel,
        out_shape=jax.ShapeDtypeStruct((L, rows, n_out), F32),
        grid=(L, n_out // tn),
        in_specs=[
            pl.BlockSpec((rows, D), lambda l, j: (0, 0)),
            pl.BlockSpec((1, D, rank), lambda l, j: (l, 0, 0)),
            pl.BlockSpec((1, rank, tn), lambda l, j: (l, 0, j)),
            pl.BlockSpec((1, 1, tn), lambda l, j: (l, 0, j)),
        ],
        out_specs=pl.BlockSpec((1, rows, tn), lambda l, j: (l, 0, j)),
        compiler_params=_cparams(2),
        name="adaln",
    )(cond, down, up, bias.reshape(L, 1, n_out))


def _norm_mod_kernel(x_ref, gain_ref, shift_ref, scale_ref, o_ref, *, tr, n_ctx):
    x = x_ref[0]
    var = jnp.mean(x * x, axis=-1, keepdims=True)
    y = x * lax.rsqrt(var + NORM_EPS) * gain_ref[...]
    row = pl.program_id(1) * tr + lax.broadcasted_iota(jnp.int32, (tr, 1), 0)
    is_ctx = row < n_ctx
    shift = jnp.where(is_ctx, shift_ref[0, 0:1, :], shift_ref[0, 1:2, :])
    scale = jnp.where(is_ctx, scale_ref[0, 0:1, :], scale_ref[0, 1:2, :])
    o_ref[0] = (y * (1.0 + scale) + shift).astype(o_ref.dtype)


def _norm_mod(xa, gain, shift, scale, n_ctx):
    B, R, D = xa.shape
    tr = _pick_tile(R, (256, 128))
    return pl.pallas_call(
        functools.partial(_norm_mod_kernel, tr=tr, n_ctx=n_ctx),
        out_shape=jax.ShapeDtypeStruct((B, R, D), BF16),
        grid=(B, R // tr),
        in_specs=[
            pl.BlockSpec((1, tr, D), lambda b, i: (b, i, 0)),
            pl.BlockSpec((1, D), lambda b, i: (0, 0)),
            pl.BlockSpec((1, 2, D), lambda b, i: (b, 0, 0)),
            pl.BlockSpec((1, 2, D), lambda b, i: (b, 0, 0)),
        ],
        out_specs=pl.BlockSpec((1, tr, D), lambda b, i: (b, i, 0)),
        compiler_params=_cparams(2),
        name="norm_mod",
    )(xa, gain.reshape(1, D), shift, scale)


def _final_norm_kernel(x_ref, gain_ref, o_ref):
    x = x_ref[0]
    var = jnp.mean(x * x, axis=-1, keepdims=True)
    o_ref[0] = x * lax.rsqrt(var + NORM_EPS) * gain_ref[...]


def _final_norm(xa, gain, n_ctx):
    B, R, D = xa.shape
    S = R - n_ctx
    tr = _pick_tile(math.gcd(S, n_ctx), (256, 128))
    off = n_ctx // tr
    return pl.pallas_call(
        _final_norm_kernel,
        out_shape=jax.ShapeDtypeStruct((B, S, D), F32),
        grid=(B, S // tr),
        in_specs=[
            pl.BlockSpec((1, tr, D), lambda b, i: (b, i + off, 0)),
            pl.BlockSpec((1, D), lambda b, i: (0, 0)),
        ],
        out_specs=pl.BlockSpec((1, tr, D), lambda b, i: (b, i, 0)),
        compiler_params=_cparams(2),
        name="final_norm",
    )(xa, gain.reshape(1, D))


def _rope_cols(a, cos, sin_signed, lane_lo):
    rot = jnp.where(lane_lo, pltpu.roll(a, HEAD_DIM - 32, 1), pltpu.roll(a, 32, 1))
    return a * cos + rot * sin_signed


def _in_ranges(x, ranges):
    return functools.reduce(jnp.logical_or, [(x >= lo) & (x < hi) for lo, hi in ranges])


def _inproj_kernel(h_ref, w_ref, cos_ref, sin_ref, o_ref, *, tn, sub, rope_ranges, qa_range, qb_range):
    j = pl.program_id(1)
    chunks = tn // HEAD_DIM
    first = j * chunks
    any_roped = functools.reduce(
        jnp.logical_or, [_in_ranges(first + c, rope_ranges) for c in range(chunks)])

    @pl.when(any_roped)
    def _():
        h = h_ref[...]
        cos = cos_ref[...]
        sin_signed = sin_ref[...]
        lane = lax.broadcasted_iota(jnp.int32, (1, HEAD_DIM), 1)
        lane_lo = (lane % ROPE_AXIS_DIM) < (ROPE_AXIS_DIM // 2)
        for c0 in range(0, tn, sub):
            acc = _dot(h, w_ref[0, :, c0:c0 + sub])
            for c in range(sub // HEAD_DIM):
                g = first + (c0 // HEAD_DIM + c)
                roped = _in_ranges(g, rope_ranges)
                mult = jnp.where(_in_ranges(g, (qa_range,)), Q_SCALE,
                                 jnp.where(_in_ranges(g, (qb_range,)), Q_SCALE * LOG2E, 1.0)).astype(F32)
                cos_g = jnp.where(roped, cos, 1.0) * mult
                sin_g = jnp.where(roped, sin_signed, 0.0) * mult
                a = acc[:, c * HEAD_DIM:(c + 1) * HEAD_DIM]
                col = c0 + c * HEAD_DIM
                o_ref[:, col:col + HEAD_DIM] = _rope_cols(a, cos_g, sin_g, lane_lo).astype(o_ref.dtype)

    @pl.when(jnp.logical_not(any_roped))
    def _():
        h = h_ref[...]
        for c0 in range(0, tn, sub):
            o_ref[:, c0:c0 + sub] = _dot(h, w_ref[0, :, c0:c0 + sub]).astype(o_ref.dtype)


def _inproj(h2d, w_all, layer, cos_tab, sin_tab, dims):
    M, D = h2d.shape
    N = w_all.shape[2]
    R = cos_tab.shape[0]
    tm = dims["tm_in"]
    tn = dims["tn_in"]
    tiles_per_batch = R // tm
    seg = dims["seg"]
    rng = lambda k: (seg[k][0] // HEAD_DIM, seg[k][1] // HEAD_DIM)
    rope_ranges = tuple(rng(k) for k in ("ka", "kb", "qa", "qb"))
    return pl.pallas_call(
        functools.partial(_inproj_kernel, tn=tn, sub=min(tn, 2 * HEAD_DIM), rope_ranges=rope_ranges,
                          qa_range=rng("qa"), qb_range=rng("qb")),
        out_shape=jax.ShapeDtypeStruct((M, N), BF16),
        grid=(M // tm, N // tn),
        in_specs=[
            pl.BlockSpec((tm, D), lambda i, j: (i, 0)),
            pl.BlockSpec((1, D, tn), lambda i, j: (layer, 0, j)),
            pl.BlockSpec((tm, HEAD_DIM), lambda i, j: (i % tiles_per_batch, 0)),
            pl.BlockSpec((tm, HEAD_DIM), lambda i, j: (i % tiles_per_batch, 0)),
        ],
        out_specs=pl.BlockSpec((tm, tn), lambda i, j: (i, j)),
        compiler_params=_cparams(2),
        name="inproj_rope",
    )(h2d, w_all, cos_tab, sin_tab)


def _win_attn_kernel(sink_ref, *refs, hkv, group, n_ctx_blocks, n_blocks):
    q_refs = refs[:hkv]
    kc_ref, vc_ref, kp_ref, ko_ref, kn_ref, vp_ref, vo_ref, vn_ref, o_ref = refs[hkv:]
    n = pl.program_id(1)
    rows = group * BLOCK
    qi = lax.broadcasted_iota(jnp.int32, (rows, BLOCK), 0) % BLOCK
    kj = lax.broadcasted_iota(jnp.int32, (rows, BLOCK), 1)
    is_lat = n >= n_ctx_blocks
    valid_p = jnp.logical_and(is_lat, n - 1 >= n_ctx_blocks)
    valid_n = jnp.logical_and(is_lat, n + 1 < n_blocks)
    mask = jnp.concatenate([
        jnp.logical_and(kj >= qi, valid_p),
        jnp.logical_and(kj >= 0, is_lat),
        jnp.logical_and(kj <= qi, valid_n)], axis=1)
    for h in range(hkv):
        cs = slice(h * HEAD_DIM, (h + 1) * HEAD_DIM)
        q = jnp.concatenate(
            [q_refs[h][0, :, g * HEAD_DIM:(g + 1) * HEAD_DIM] for g in range(group)], axis=0)
        sink = jnp.concatenate(
            [jnp.full((BLOCK, 1), sink_ref[h * group + g], F32) for g in range(group)], axis=0)
        k_band = jnp.concatenate([kp_ref[0, :, cs], ko_ref[0, :, cs], kn_ref[0, :, cs]], axis=0)
        v_band = jnp.concatenate([vp_ref[0, :, cs], vo_ref[0, :, cs], vn_ref[0, :, cs]], axis=0)
        s_c = _dot_nt(q, kc_ref[0, :, cs])
        s_b = jnp.where(mask, _dot_nt(q, k_band), NEG_INF)
        m = jnp.maximum(jnp.maximum(jnp.max(s_c, axis=-1, keepdims=True),
                                    jnp.max(s_b, axis=-1, keepdims=True)), sink)
        e_c = jnp.exp(s_c - m)
        e_b = jnp.exp(s_b - m)
        denom = (jnp.sum(e_c, axis=-1, keepdims=True) + jnp.sum(e_b, axis=-1, keepdims=True)
                 + jnp.exp(sink - m))
        out = (_dot(e_b.astype(BF16), v_band) + _dot(e_c.astype(BF16), vc_ref[0, :, cs])) / denom
        for g in range(group):
            col = (h * group + g) * HEAD_DIM
            o_ref[0, :, col:col + HEAD_DIM] = out[g * BLOCK:(g + 1) * BLOCK].astype(o_ref.dtype)


def _win_attn(px, sink, dims):
    B, R, _ = px.shape
    seg = dims["seg"]
    hkv, group, n_ctx = dims["hkv"], dims["group"], dims["n_ctx"]
    nb = R // BLOCK
    ncb = n_ctx // BLOCK
    qw = group * HEAD_DIM
    kw = hkv * HEAD_DIM
    k0 = seg["ka"][0] // kw
    v0 = seg["va"][0] // kw
    q0 = seg["qa"][0] // qw
    assert seg["qa"][0] % qw == 0 and seg["ka"][0] % kw == 0 and seg["va"][0] % kw == 0

    def q_spec(h):
        return pl.BlockSpec((1, BLOCK, qw), lambda b, n: (b, n, q0 + h))

    def band_spec(col, shift):
        return pl.BlockSpec((1, BLOCK, kw), lambda b, n: (b, jnp.clip(n + shift, 0, nb - 1), col))

    return pl.pallas_call(
        functools.partial(_win_attn_kernel, hkv=hkv, group=group, n_ctx_blocks=ncb, n_blocks=nb),
        out_shape=jax.ShapeDtypeStruct((B, R, hkv * qw), BF16),
        grid=(B, nb),
        in_specs=[pl.BlockSpec(memory_space=pltpu.SMEM)]
        + [q_spec(h) for h in range(hkv)]
        + [pl.BlockSpec((1, n_ctx, kw), lambda b, n: (b, 0, k0)),
           pl.BlockSpec((1, n_ctx, kw), lambda b, n: (b, 0, v0)),
           band_spec(k0, -1), band_spec(k0, 0), band_spec(k0, 1),
           band_spec(v0, -1), band_spec(v0, 0), band_spec(v0, 1)],
        out_specs=pl.BlockSpec((1, BLOCK, hkv * qw), lambda b, n: (b, n, 0)),
        compiler_params=_cparams(2),
        name="window_attn",
    )(sink, *([px] * (hkv + 8)))


def _diff_attn_kernel(lq1_ref, lk1_ref, lq2_ref, lk2_ref, gain_ref, q_ref, k_ref, v_ref, o_ref,
                      acc1_ref, acc2_ref, s_ref, *, tq, tk, n_ctx, n_rows, lam_init):
    lam = (jnp.exp(jnp.sum(lq1_ref[...] * lk1_ref[...], axis=-1, keepdims=True))
           - jnp.exp(jnp.sum(lq2_ref[...] * lk2_ref[...], axis=-1, keepdims=True)) + lam_init)
    gain = gain_ref[...] * (1.0 - lam_init)

    def finish(q_start, nq, l1, l2):
        o = acc1_ref[pl.ds(0, nq), :] / l1 - lam * (acc2_ref[pl.ds(0, nq), :] / l2)
        var = jnp.mean(o * o, axis=-1, keepdims=True)
        y = o * lax.rsqrt(var + NORM_EPS) * gain
        o_ref[0, pl.ds(q_start, nq), :] = y.astype(o_ref.dtype)

    def attend_online(q_start, nq, kv_len, chunk):
        q1 = q_ref[0, pl.ds(q_start, nq), :HEAD_DIM]
        q2 = q_ref[0, pl.ds(q_start, nq), HEAD_DIM:]
        accs = (acc1_ref.at[pl.ds(0, nq), :], acc2_ref.at[pl.ds(0, nq), :])

        def step(t, carry):
            start = pl.multiple_of(t * chunk, chunk)
            k = k_ref[0, pl.ds(start, chunk), :]
            v = v_ref[0, pl.ds(start, chunk), :]
            out = []
            for which, (q, acc) in enumerate(zip((q1, q2), accs)):
                m, l = carry[2 * which], carry[2 * which + 1]
                s = _dot_nt(q, k[:, which * HEAD_DIM:(which + 1) * HEAD_DIM])
                n = jnp.maximum(m, jnp.max(s, axis=-1, keepdims=True))
                alpha = jnp.exp2(m - n)
                p = jnp.exp2(s - n)
                acc[...] = alpha * acc[...] + _dot(p.astype(BF16), v)
                out += [n, alpha * l + jnp.sum(p, axis=-1, keepdims=True)]
            return tuple(out)

        for acc in accs:
            acc[...] = jnp.zeros((nq, 2 * HEAD_DIM), F32)
        carry = (jnp.full((nq, 1), -jnp.inf, F32), jnp.zeros((nq, 1), F32)) * 2
        if kv_len == chunk:
            carry = step(0, carry)
        else:
            carry = lax.fori_loop(0, kv_len // chunk, step, carry)
        finish(q_start, nq, carry[1], carry[3])

    def attend_bounded(q_start, nq, kv_len, chunk, key_norm):
        qs = tuple(q_ref[0, pl.ds(q_start, nq), w * HEAD_DIM:(w + 1) * HEAD_DIM] for w in range(2))
        accs = (acc1_ref.at[pl.ds(0, nq), :], acc2_ref.at[pl.ds(0, nq), :])
        bound = []
        for q, kn in zip(qs, key_norm):
            qf = q.astype(F32)
            bound.append(jnp.sqrt(jnp.sum(qf * qf, axis=-1, keepdims=True)) * kn)

        def scores(t, slot):
            k = k_ref[0, pl.ds(pl.multiple_of(t * chunk, chunk), chunk), :]
            for which in range(2):
                s_ref[slot, which, pl.ds(0, nq), pl.ds(0, chunk)] = _dot_nt(
                    qs[which], k[:, which * HEAD_DIM:(which + 1) * HEAD_DIM])

        def softmax_pv(t, slot, ls):
            v = v_ref[0, pl.ds(pl.multiple_of(t * chunk, chunk), chunk), :]
            out = []
            for which in range(2):
                p = jnp.exp2(s_ref[slot, which, pl.ds(0, nq), pl.ds(0, chunk)] - bound[which])
                accs[which][...] += _dot(p.astype(BF16), v)
                out.append(ls[which] + jnp.sum(p, axis=-1, keepdims=True))
            return tuple(out)

        for acc in accs:
            acc[...] = jnp.zeros((nq, 2 * HEAD_DIM), F32)
        ls = (jnp.zeros((nq, 1), F32),) * 2
        n_chunks = kv_len // chunk

        def pair(u, c):
            scores(2 * u + 1, 1)
            c = softmax_pv(2 * u, 0, c)
            scores(2 * u + 2, 0)
            return softmax_pv(2 * u + 1, 1, c)

        scores(0, 0)
        n_pairs = (n_chunks - 1) // 2
        if n_pairs > 0:
            ls = lax.fori_loop(0, n_pairs, pair, ls)
        if n_chunks % 2 == 0:
            scores(n_chunks - 1, 1)
            ls = softmax_pv(n_chunks - 2, 0, ls)
            ls = softmax_pv(n_chunks - 1, 1, ls)
        else:
            ls = softmax_pv(n_chunks - 1, 0, ls)
        return ls

    attend_online(0, n_ctx, n_ctx, n_ctx)

    def max_norm(t, mx):
        k = k_ref[0, pl.ds(pl.multiple_of(t * tk, tk), tk), :].astype(F32)
        sq = k * k
        return tuple(
            jnp.maximum(mx[w], jnp.max(jnp.sum(sq[:, w * HEAD_DIM:(w + 1) * HEAD_DIM], axis=-1, keepdims=True),
                                       axis=0, keepdims=True))
            for w in range(2))

    key_sq = lax.fori_loop(0, n_rows // tk, max_norm, (jnp.zeros((1, 1), F32),) * 2)
    key_norm = tuple(jnp.sqrt(x) for x in key_sq)
    align = math.gcd(n_ctx, tq)

    def latent_tile(i, _):
        q_start = pl.multiple_of(n_ctx + i * tq, align)
        l1, l2 = attend_bounded(q_start, tq, n_rows, tk, key_norm)
        smallest = jnp.min(jnp.minimum(l1, l2), axis=0, keepdims=True)[0, 0]
        safe = smallest >= MIN_SOFTMAX_DENOM

        @pl.when(safe)
        def _():
            finish(q_start, tq, l1, l2)

        @pl.when(jnp.logical_not(safe))
        def _():
            attend_online(q_start, tq, n_rows, tk)

        return 0

    lax.fori_loop(0, (n_rows - n_ctx) // tq, latent_tile, 0)


def _diff_attn(px, lq1, lk1, lq2, lk2, gain, lam_init, dims):
    B, R, _ = px.shape
    seg = dims["seg"]
    hb, n_ctx = dims["hb"], dims["n_ctx"]
    w = 2 * HEAD_DIM
    tq = _pick_tile(R - n_ctx, (512, 256, 128))
    tk = _pick_tile(R, (768, 512, 384, 256, 128))
    assert n_ctx <= tq and n_ctx % 16 == 0
    q0, k0, v0 = seg["qb"][0] // w, seg["kb"][0] // w, seg["vb"][0] // w
    vec = lambda a: a.reshape(1, -1)
    small = lambda n: pl.BlockSpec((1, n), lambda b, h: (0, 0))
    head = lambda col: pl.BlockSpec((1, R, w), lambda b, h: (b, 0, col + h))
    return pl.pallas_call(
        functools.partial(_diff_attn_kernel, tq=tq, tk=tk, n_ctx=n_ctx, n_rows=R, lam_init=lam_init),
        out_shape=jax.ShapeDtypeStruct((B, R, hb * w), BF16),
        grid=(B, hb),
        in_specs=[
            small(HEAD_DIM), small(HEAD_DIM), small(HEAD_DIM), small(HEAD_DIM), small(w),
            head(q0), head(k0), head(v0),
        ],
        out_specs=head(0),
        scratch_shapes=[pltpu.VMEM((tq, w), F32), pltpu.VMEM((tq, w), F32),
                        pltpu.VMEM((2, 2, tq, tk), F32)],
        compiler_params=_cparams(2),
        name="diff_attn",
    )(vec(lq1), vec(lk1), vec(lq2), vec(lk2), vec(gain), px, px, px)


def _merge_kernel(ya_ref, yb_ref, wa_ref, wb_ref, ga_ref, gb_ref, o_ref):
    pa = _dot(ya_ref[...], wa_ref[0])
    pb = _dot(yb_ref[...], wb_ref[0])
    m = _sigmoid(ga_ref[...].astype(F32)) * pa + _sigmoid(gb_ref[...].astype(F32)) * pb
    o_ref[...] = m.astype(o_ref.dtype)


def _merge(ya, yb, wa, wb, layer, px2d, dims):
    M = ya.shape[0]
    D = wa.shape[2]
    tm, tn = dims["tm"], dims["tn"]
    g0 = dims["seg"]["gate"][0] // tn
    g1 = g0 + D // tn
    return pl.pallas_call(
        _merge_kernel,
        out_shape=jax.ShapeDtypeStruct((M, D), BF16),
        grid=(M // tm, D // tn),
        in_specs=[
            pl.BlockSpec((tm, ya.shape[1]), lambda i, j: (i, 0)),
            pl.BlockSpec((tm, yb.shape[1]), lambda i, j: (i, 0)),
            pl.BlockSpec((1, wa.shape[1], tn), lambda i, j: (layer, 0, j)),
            pl.BlockSpec((1, wb.shape[1], tn), lambda i, j: (layer, 0, j)),
            pl.BlockSpec((tm, tn), lambda i, j: (i, g0 + j)),
            pl.BlockSpec((tm, tn), lambda i, j: (i, g1 + j)),
        ],
        out_specs=pl.BlockSpec((tm, tn), lambda i, j: (i, j)),
        compiler_params=_cparams(2),
        name="merge_branches",
    )(ya, yb, wa, wb, px2d, px2d)


def _row_gate(gate_ref, tile_in_batch, tm, n_ctx):
    row = tile_in_batch * tm + lax.broadcasted_iota(jnp.int32, (tm, 1), 0)
    return jnp.where(row < n_ctx, gate_ref[0, 0:1, :], gate_ref[0, 1:2, :])


def _outproj_kernel(m_ref, w_ref, x_ref, gate_ref, o_ref, *, tm, tiles_per_batch, n_ctx):
    gate = _row_gate(gate_ref, pl.program_id(0) % tiles_per_batch, tm, n_ctx)
    o_ref[...] = x_ref[...] + gate * _dot(m_ref[...], w_ref[0])


def _outproj(m, w, layer, x2d, gate, dims):
    M, D = x2d.shape
    tm, tn = dims["tm"], dims["tn"]
    tpb = dims["rows"] // tm
    return pl.pallas_call(
        functools.partial(_outproj_kernel, tm=tm, tiles_per_batch=tpb, n_ctx=dims["n_ctx"]),
        out_shape=jax.ShapeDtypeStruct((M, D), F32),
        grid=(M // tm, D // tn),
        in_specs=[
            pl.BlockSpec((tm, D), lambda i, j: (i, 0)),
            pl.BlockSpec((1, D, tn), lambda i, j: (layer, 0, j)),
            pl.BlockSpec((tm, tn), lambda i, j: (i, j)),
            pl.BlockSpec((1, 2, tn), lambda i, j: (i // tpb, 0, j)),
        ],
        out_specs=pl.BlockSpec((tm, tn), lambda i, j: (i, j)),
        input_output_aliases={2: 0},
        compiler_params=_cparams(2),
        name="outproj_residual",
    )(m, w, x2d, gate)


def _route(logits, bias, n_experts):
    per_group = n_experts // N_GROUPS
    scores = _sigmoid(logits)
    sel = scores + bias
    srow = [scores[e:e + 1, :] for e in range(n_experts)]
    row = [sel[e:e + 1, :] for e in range(n_experts)]
    best, gidx = None, None
    for g in range(N_GROUPS):
        mem = row[g * per_group:(g + 1) * per_group]
        gs = None
        for a in range(per_group):
            for b in range(a + 1, per_group):
                s = mem[a] + mem[b]
                gs = s if gs is None else jnp.maximum(gs, s)
        if best is None:
            best, gidx = gs, jnp.zeros_like(gs, dtype=jnp.int32)
        else:
            better = gs > best
            gidx = jnp.where(better, g, gidx)
            best = jnp.where(better, gs, best)
    masked = [jnp.where(gidx == (e // per_group), row[e], NEG_INF) for e in range(n_experts)]
    m1, i1 = masked[0], jnp.zeros_like(gidx)
    for e in range(1, n_experts):
        better = masked[e] > m1
        i1 = jnp.where(better, e, i1)
        m1 = jnp.where(better, masked[e], m1)
    m2 = jnp.full_like(m1, -jnp.inf)
    i2 = jnp.zeros_like(gidx)
    for e in range(n_experts):
        cand = jnp.where(i1 == e, -jnp.inf, masked[e])
        better = cand > m2
        i2 = jnp.where(better, e, i2)
        m2 = jnp.where(better, cand, m2)
    w1 = functools.reduce(jnp.add, [jnp.where(i1 == e, srow[e], 0.0) for e in range(n_experts)])
    w2 = functools.reduce(jnp.add, [jnp.where(i2 == e, srow[e], 0.0) for e in range(n_experts)])
    tot = w1 + w2
    return i1, i2, w1 / tot * ROUTED_SCALE, w2 / tot * ROUTED_SCALE


META_ROWS = 8


def _norm_router_kernel(x_ref, gain_ref, shift_ref, scale_ref, wr_ref, bias_ref,
                        h_ref, meta_ref, cnt_ref, carry_ref, *, tr, n_ctx, n_experts):
    first = jnp.logical_and(pl.program_id(0) == 0, pl.program_id(1) == 0)

    @pl.when(first)
    def _():
        carry_ref[...] = jnp.zeros_like(carry_ref)

    x = x_ref[0]
    var = jnp.mean(x * x, axis=-1, keepdims=True)
    y = x * lax.rsqrt(var + NORM_EPS) * gain_ref[...]
    row = pl.program_id(1) * tr + lax.broadcasted_iota(jnp.int32, (tr, 1), 0)
    is_ctx = row < n_ctx
    shift = jnp.where(is_ctx, shift_ref[0, 0:1, :], shift_ref[0, 1:2, :])
    scale = jnp.where(is_ctx, scale_ref[0, 0:1, :], scale_ref[0, 1:2, :])
    h = y * (1.0 + scale) + shift
    h_ref[0] = h

    logits = _dot_nt(wr_ref[...], h.astype(BF16))
    i1, i2, w1, w2 = _route(logits, bias_ref[...], n_experts)
    erow = lax.broadcasted_iota(jnp.int32, (n_experts, tr), 0)
    hit1 = erow == i1
    hit2 = erow == i2
    onehot = jnp.where(jnp.logical_or(hit1, hit2), 1.0, 0.0)
    before = (lax.broadcasted_iota(jnp.int32, (tr, tr), 0)
              < lax.broadcasted_iota(jnp.int32, (tr, tr), 1))
    prefix = _dot(onehot.astype(BF16), jnp.where(before, 1.0, 0.0).astype(BF16))
    seen = prefix + carry_ref[:, 0:1]
    r1 = jnp.sum(jnp.where(hit1, seen, 0.0), axis=0, keepdims=True)
    r2 = jnp.sum(jnp.where(hit2, seen, 0.0), axis=0, keepdims=True)
    carry_ref[...] = carry_ref[...] + jnp.sum(onehot, axis=1, keepdims=True)
    zero = jnp.zeros_like(w1)
    meta_ref[...] = jnp.concatenate(
        [i1.astype(F32), i2.astype(F32), w1, w2, r1, r2, zero, zero], axis=0)
    cnt_ref[...] = carry_ref[...]


def _norm_router(xa, gain, shift, scale, wr_t, bias, n_ctx):
    B, R, D = xa.shape
    E = wr_t.shape[0]
    tr = _pick_tile(R, (256, 128))
    nt = R // tr
    return pl.pallas_call(
        functools.partial(_norm_router_kernel, tr=tr, n_ctx=n_ctx, n_experts=E),
        out_shape=(jax.ShapeDtypeStruct((B, R, D), F32),
                   jax.ShapeDtypeStruct((META_ROWS, B * R), F32),
                   jax.ShapeDtypeStruct((E, HEAD_DIM), F32)),
        grid=(B, nt),
        in_specs=[
            pl.BlockSpec((1, tr, D), lambda b, i: (b, i, 0)),
            pl.BlockSpec((1, D), lambda b, i: (0, 0)),
            pl.BlockSpec((1, 2, D), lambda b, i: (b, 0, 0)),
            pl.BlockSpec((1, 2, D), lambda b, i: (b, 0, 0)),
            pl.BlockSpec((E, D), lambda b, i: (0, 0)),
            pl.BlockSpec((E, 1), lambda b, i: (0, 0)),
        ],
        out_specs=(pl.BlockSpec((1, tr, D), lambda b, i: (b, i, 0)),
                   pl.BlockSpec((META_ROWS, tr), lambda b, i: (0, b * nt + i)),
                   pl.BlockSpec((E, HEAD_DIM), lambda b, i: (0, 0))),
        scratch_shapes=[pltpu.VMEM((E, HEAD_DIM), F32)],
        compiler_params=_cparams(2),
        name="norm_router",
    )(xa, gain.reshape(1, D), shift, scale, wr_t, bias.reshape(E, 1).astype(F32))


SUBLANES = 8


def _row_gather(idx_vmem_ref, idx_smem, isem, src_hbm, buf, sem, slot, n_rows):
    cp = pltpu.make_async_copy(idx_vmem_ref.at[0, 0], idx_smem.at[pl.ds(slot * n_rows, n_rows)], isem)
    cp.start()
    cp.wait()

    def body(g, carry):
        base = slot * n_rows + g * SUBLANES
        for s in range(SUBLANES):
            t = idx_smem[base + s]
            pltpu.make_async_copy(src_hbm.at[pl.ds(t, 1), :], buf.at[slot, g, pl.ds(s, 1), :],
                                  sem.at[slot]).start()
        return carry

    lax.fori_loop(0, n_rows // SUBLANES, body, 0)


def _row_gather_wait(src_hbm, buf, sem, slot, n_rows):
    def body(g, carry):
        pltpu.make_async_copy(src_hbm.at[pl.ds(0, SUBLANES), :], buf.at[slot, g], sem.at[slot]).wait()
        return carry

    lax.fori_loop(0, n_rows // SUBLANES, body, 0)


def _gathered(buf, slot):
    rows = buf[slot]
    return rows.reshape(rows.shape[0] * SUBLANES, rows.shape[2])


def _expert_kernel(te_ref, nv_ref, tok_ref, tok_next_ref, h_hbm, wg_ref, wu_ref, wd_ref,
                   o_ref, buf, idx_smem, sem, isem, *, tm):
    j = pl.program_id(0)
    n_valid = nv_ref[0]
    slot = j % 2

    @pl.when(j == 0)
    def _():
        _row_gather(tok_ref, idx_smem, isem, h_hbm, buf, sem, 0, tm)

    @pl.when(j + 1 < n_valid)
    def _():
        _row_gather(tok_next_ref, idx_smem, isem, h_hbm, buf, sem, 1 - slot, tm)

    @pl.when(j < n_valid)
    def _():
        _row_gather_wait(h_hbm, buf, sem, slot, tm)
        h = _gathered(buf, slot).astype(BF16)
        g = _dot(h, wg_ref[0, 0])
        u = _dot(h, wu_ref[0, 0])
        a = (g * _sigmoid(g) * u).astype(BF16)
        o_ref[...] = _dot(a, wd_ref[0, 0])

    @pl.when(j >= n_valid)
    def _():
        o_ref[...] = jnp.zeros_like(o_ref)


def _experts(h2d, tok3, tile_expert, n_valid, wg, wu, wd, layer, tm):
    M, D = h2d.shape
    F = wg.shape[-1]
    n_tiles = tok3.shape[0]
    P = n_tiles * tm
    grid_spec = pltpu.PrefetchScalarGridSpec(
        num_scalar_prefetch=2,
        grid=(n_tiles,),
        in_specs=[
            pl.BlockSpec((1, 1, tm), lambda j, te, nv: (j, 0, 0)),
            pl.BlockSpec((1, 1, tm), lambda j, te, nv: (jnp.minimum(j + 1, n_tiles - 1), 0, 0)),
            pl.BlockSpec(memory_space=pl.ANY),
            pl.BlockSpec((1, 1, D, F), lambda j, te, nv: (layer, te[j], 0, 0)),
            pl.BlockSpec((1, 1, D, F), lambda j, te, nv: (layer, te[j], 0, 0)),
            pl.BlockSpec((1, 1, F, D), lambda j, te, nv: (layer, te[j], 0, 0)),
        ],
        out_specs=pl.BlockSpec((tm, D), lambda j, te, nv: (j, 0)),
        scratch_shapes=[pltpu.VMEM((2, tm // SUBLANES, SUBLANES, D), F32), pltpu.SMEM((2 * tm,), jnp.int32),
                        pltpu.SemaphoreType.DMA((2,)), pltpu.SemaphoreType.DMA],
    )
    return pl.pallas_call(
        functools.partial(_expert_kernel, tm=tm),
        out_shape=jax.ShapeDtypeStruct((P, D), F32),
        grid_spec=grid_spec,
        compiler_params=_cparams(1),
        name="moe_experts",
    )(tile_expert, n_valid, tok3, tok3, h2d, wg, wu, wd)


def _combine_kernel(p1_ref, p1n_ref, p2_ref, p2n_ref, y_hbm, x_ref, w_ref, gate_ref, *rest,
                    tm, tiles_per_batch, n_ctx, with_norm):
    if with_norm:
        gain_ref, shift_ref, scale_ref, o_ref, h_ref = rest[:5]
    else:
        o_ref = rest[0]
    buf1, buf2, idx1, idx2, sem1, sem2, isem = rest[-7:]
    j = pl.program_id(0)
    slot = j % 2

    @pl.when(j == 0)
    def _():
        _row_gather(p1_ref, idx1, isem, y_hbm, buf1, sem1, 0, tm)
        _row_gather(p2_ref, idx2, isem, y_hbm, buf2, sem2, 0, tm)

    @pl.when(j + 1 < pl.num_programs(0))
    def _():
        _row_gather(p1n_ref, idx1, isem, y_hbm, buf1, sem1, 1 - slot, tm)
        _row_gather(p2n_ref, idx2, isem, y_hbm, buf2, sem2, 1 - slot, tm)

    _row_gather_wait(y_hbm, buf1, sem1, slot, tm)
    _row_gather_wait(y_hbm, buf2, sem2, slot, tm)
    step_rows = 2 * SUBLANES
    ctx_tile = (j % tiles_per_batch) * tm < n_ctx
    pick = lambda ref: jnp.where(ctx_tile, ref[0, 0:1, :], ref[0, 1:2, :])
    gate = pick(gate_ref)
    if with_norm:
        gain_scale = gain_ref[...] * (1.0 + pick(scale_ref))
        shift = pick(shift_ref)

    def rows_step(i, carry):
        rows = pl.ds(pl.multiple_of(i * step_rows, step_rows), step_rows)
        b1 = jnp.concatenate([buf1[slot, 2 * i], buf1[slot, 2 * i + 1]], axis=0)
        b2 = jnp.concatenate([buf2[slot, 2 * i], buf2[slot, 2 * i + 1]], axis=0)
        w = w_ref[rows, :]
        x = x_ref[rows, :] + gate * (w[:, 0:1] * b1 + w[:, 1:2] * b2)
        o_ref[rows, :] = x
        if with_norm:
            var = jnp.mean(x * x, axis=-1, keepdims=True)
            h_ref[rows, :] = (x * lax.rsqrt(var + NORM_EPS) * gain_scale + shift).astype(h_ref.dtype)
        return carry

    lax.fori_loop(0, tm // step_rows, rows_step, 0, unroll=4)


def _combine(x2d, y_sorted, pos1, pos2, w12, gate, dims, next_norm=None):
    M, D = x2d.shape
    tm = dims["tm_comb"]
    assert dims["n_ctx"] % tm == 0
    n_tiles = M // tm
    tpb = dims["rows"] // tm
    p1 = pos1.reshape(n_tiles, 1, tm)
    p2 = pos2.reshape(n_tiles, 1, tm)
    cur = pl.BlockSpec((1, 1, tm), lambda j: (j, 0, 0))
    nxt = pl.BlockSpec((1, 1, tm), lambda j: (jnp.minimum(j + 1, n_tiles - 1), 0, 0))
    rows = pl.BlockSpec((tm, D), lambda j: (j, 0))
    per_batch = pl.BlockSpec((1, 2, D), lambda j: (j // tpb, 0, 0))
    with_norm = next_norm is not None
    extra_in, extra_args = [], []
    out_shape = jax.ShapeDtypeStruct((M, D), F32)
    out_specs = rows
    if with_norm:
        gain, shift, scale = next_norm
        extra_in = [pl.BlockSpec((1, D), lambda j: (0, 0)), per_batch, per_batch]
        extra_args = [gain.reshape(1, D), shift, scale]
        out_shape = (out_shape, jax.ShapeDtypeStruct((M, D), BF16))
        out_specs = (rows, rows)
    return pl.pallas_call(
        functools.partial(_combine_kernel, tm=tm, tiles_per_batch=tpb, n_ctx=dims["n_ctx"],
                          with_norm=with_norm),
        out_shape=out_shape,
        grid=(n_tiles,),
        in_specs=[cur, nxt, cur, nxt,
                  pl.BlockSpec(memory_space=pl.ANY),
                  rows,
                  pl.BlockSpec((tm, 2), lambda j: (j, 0)),
                  per_batch] + extra_in,
        out_specs=out_specs,
        scratch_shapes=[pltpu.VMEM((2, tm // SUBLANES, SUBLANES, D), F32),
                        pltpu.VMEM((2, tm // SUBLANES, SUBLANES, D), F32),
                        pltpu.SMEM((2 * tm,), jnp.int32), pltpu.SMEM((2 * tm,), jnp.int32),
                        pltpu.SemaphoreType.DMA((2,)), pltpu.SemaphoreType.DMA((2,)),
                        pltpu.SemaphoreType.DMA],
        input_output_aliases={5: 0},
        compiler_params=_cparams(1),
        name="moe_combine",
    )(p1, p1, p2, p2, y_sorted, x2d, w12, gate, *extra_args)


def _dispatch_plan(meta, counts, tm, n_tiles):
    M = meta.shape[1]
    i1 = meta[0].astype(jnp.int32)
    i2 = meta[1].astype(jnp.int32)
    r1 = meta[4].astype(jnp.int32)
    r2 = meta[5].astype(jnp.int32)
    cnt = counts[:, 0].astype(jnp.int32)
    padded = ((cnt + tm - 1) // tm) * tm
    seg_end = jnp.cumsum(padded)
    seg_start = seg_end - padded
    pos1 = seg_start[i1] + r1
    pos2 = seg_start[i2] + r2
    n_valid = seg_end[-1] // tm
    tile_start = jnp.arange(n_tiles, dtype=jnp.int32) * tm
    probe = jnp.minimum(tile_start, seg_end[-1] - tm)
    tile_expert = jnp.sum(probe[:, None] >= seg_end[None, :], axis=1).astype(jnp.int32)
    tok = jnp.arange(M, dtype=jnp.int32)
    tok_sorted = jnp.zeros((n_tiles * tm,), jnp.int32).at[jnp.concatenate([pos1, pos2])].set(
        jnp.concatenate([tok, tok]), unique_indices=True)
    w12 = jnp.stack([meta[2], meta[3]], axis=1)
    return (pos1, pos2, w12, tok_sorted.reshape(n_tiles, 1, tm), tile_expert,
            n_valid.reshape(1).astype(jnp.int32))


def _rope_tables(seq, n_ctx):
    rows = seq // GRID_W
    row = jnp.repeat(jnp.arange(rows), GRID_W).astype(F32)
    col = jnp.tile(jnp.arange(GRID_W), rows).astype(F32)
    inv = ROPE_THETA ** (-(jnp.arange(0, ROPE_AXIS_DIM, 2, dtype=F32) / ROPE_AXIS_DIM))
    ang_r = row[:, None] * inv
    ang_c = col[:, None] * inv
    ang = jnp.concatenate([ang_r, ang_r, ang_c, ang_c], axis=-1)
    lane = jnp.arange(HEAD_DIM)
    sign = jnp.where((lane % ROPE_AXIS_DIM) < (ROPE_AXIS_DIM // 2), -1.0, 1.0).astype(F32)
    cos = jnp.concatenate([jnp.ones((n_ctx, HEAD_DIM), F32), jnp.cos(ang)], axis=0)
    sin = jnp.concatenate([jnp.zeros((n_ctx, HEAD_DIM), F32), jnp.sin(ang) * sign], axis=0)
    return cos, sin


def _dims(D, S, C, d_expert):
    ha = D // 256
    hkv = ha // 4
    hb = D // 512
    widths = [("ka", hkv * HEAD_DIM), ("va", hkv * HEAD_DIM), ("kb", hb * 2 * HEAD_DIM),
              ("vb", hb * 2 * HEAD_DIM), ("qa", ha * HEAD_DIM), ("qb", hb * 2 * HEAD_DIM),
              ("gate", 2 * D)]
    seg, off = {}, 0
    for name, wdt in widths:
        seg[name] = (off, off + wdt)
        off += wdt
    R = C + S
    tm = _pick_tile(R, (768, 512, 384, 256, 128))
    tn = _pick_tile(D, (512, 256, 128))
    tn_in = _pick_tile(off, (1024, 512, 256, 128))
    tm_in = _pick_tile(R, (1056, 768, 512, 384, 256, 128))
    tm_comb = _pick_tile(R, (256, 128))
    return dict(ha=ha, hkv=hkv, group=ha // hkv, hb=hb, seg=seg, n_cols=off, rows=R, n_ctx=C,
                tm=tm, tn=tn, tn_in=tn_in, tm_in=tm_in, tm_exp=256, tm_comb=tm_comb)


def kernel(x, c, ctx, c_ctx, ada_down, ada_up, ada_bias, norm_mix, norm_ffn, w_in, sink_logit,
           lam_q1, lam_k1, lam_q2, lam_k2, subln_gain, w_branch_a, w_branch_b, w_out,
           w_router, b_router, w_exp_gate, w_exp_up, w_exp_down, norm_final):
    B, S, D = x.shape
    C = ctx.shape[1]
    depth = w_in.shape[0]
    dims = _dims(D, S, C, w_exp_gate.shape[-1])
    R = dims["rows"]
    M = B * R

    cond = jnp.concatenate([c, c_ctx[None], jnp.zeros((8 - (B + 1) % 8 if (B + 1) % 8 else 0, D), F32)])
    mods = _adaln(cond, ada_down, ada_up, ada_bias)
    mods = mods.reshape(depth, mods.shape[1], N_MOD, D)
    lat = mods[:, :B]
    cx = jnp.broadcast_to(mods[:, B:B + 1], lat.shape)
    mod = jnp.stack([cx, lat], axis=2)

    cos_tab, sin_tab = _rope_tables(S, C)
    wr_t = w_router.T.astype(BF16)
    xa = jnp.concatenate([ctx, x], axis=1)
    w_in_b, wa_b, wb_b, wo_b = (w.astype(BF16) for w in (w_in, w_branch_a, w_branch_b, w_out))
    wg_b, wu_b, wd_b = (w.astype(BF16) for w in (w_exp_gate, w_exp_up, w_exp_down))
    n_experts = w_router.shape[1]
    tm_exp = dims["tm_exp"]
    n_exp_tiles = -(-(2 * M + n_experts * (tm_exp - 1)) // tm_exp)

    mods_of = lambda l: [mod[l, :, :, k] for k in range(N_MOD)]
    h_mix = _norm_mod(xa, norm_mix[0], mods_of(0)[0], mods_of(0)[1], C).reshape(M, D)
    for l in range(depth):
        lam_init = 0.8 - 0.6 * math.exp(-0.3 * l)
        m = mods_of(l)
        px = _inproj(h_mix, w_in_b, l, cos_tab, sin_tab, dims)
        px3 = px.reshape(B, R, -1)
        ya = _win_attn(px3, sink_logit[l], dims)
        yb = _diff_attn(px3, lam_q1[l], lam_k1[l], lam_q2[l], lam_k2[l], subln_gain[l], lam_init, dims)
        mg = _merge(ya.reshape(M, -1), yb.reshape(M, -1), wa_b, wb_b, l, px, dims)
        x2d = _outproj(mg, wo_b, l, xa.reshape(M, D), m[2], dims)
        xa = x2d.reshape(B, R, D)
        h, meta, counts = _norm_router(xa, norm_ffn[l], m[3], m[4], wr_t, b_router, C)
        pos1, pos2, w12, tok3, tile_expert, n_valid = _dispatch_plan(meta, counts, tm_exp, n_exp_tiles)
        y_sorted = _experts(h.reshape(M, D), tok3, tile_expert, n_valid, wg_b, wu_b, wd_b, l, tm_exp)
        if l + 1 < depth:
            nm = mods_of(l + 1)
            x2d, h_mix = _combine(x2d, y_sorted, pos1, pos2, w12, m[5], dims,
                                  next_norm=(norm_mix[l + 1], nm[0], nm[1]))
        else:
            x2d = _combine(x2d, y_sorted, pos1, pos2, w12, m[5], dims)
        xa = x2d.reshape(B, R, D)
    return _final_norm(xa, norm_final, C)
```

```python
import functools
import math

import jax
import jax.numpy as jnp
from jax import lax
from jax.experimental import pallas as pl
from jax.experimental.pallas import tpu as pltpu

HEAD_DIM = 128
GRID_W = 64
ROPE_AXIS_DIM = HEAD_DIM // 2
ROPE_THETA = 10000.0
BLOCK = 128
N_GROUPS = 4
ROUTED_SCALE = 1.0
NORM_EPS = 1e-6
NEG_INF = -1e30
N_MOD = 6
Q_SCALE = HEAD_DIM ** -0.5
LOG2E = math.log2(math.e)
MIN_SOFTMAX_DENOM = 2.0 ** -80

F32 = jnp.float32
BF16 = jnp.bfloat16

VMEM_LIMIT_BYTES = 56 * 1024 * 1024


def _cparams(n_axes):
    return pltpu.CompilerParams(dimension_semantics=("arbitrary",) * n_axes,
                                vmem_limit_bytes=VMEM_LIMIT_BYTES)


def _pick_tile(total, candidates):
    for t in candidates:
        if total % t == 0:
            return t
    raise ValueError(f"no tile in {candidates} divides {total}")


def _dot(a, b):
    return jnp.dot(a, b, preferred_element_type=F32)


def _dot_nt(a, b):
    return lax.dot_general(a, b, (((1,), (1,)), ((), ())), preferred_element_type=F32)


def _sigmoid(x):
    return 1.0 / (1.0 + jnp.exp(-x))


def _adaln_kernel(cond_ref, down_ref, up_ref, bias_ref, o_ref):
    cond = cond_ref[...]
    s = (cond * _sigmoid(cond)).astype(BF16)
    t = _dot(s, down_ref[0].astype(BF16)).astype(BF16)
    o_ref[0] = _dot(t, up_ref[0].astype(BF16)) + bias_ref[0]


def _adaln(cond, down, up, bias):
    L, D, rank = down.shape
    n_out = up.shape[-1]
    rows = cond.shape[0]
    tn = _pick_tile(n_out, (2048, 1024, 512, 256, 128))
    return pl.pallas_call(
        _adaln_kernel,
        out_shape=jax.ShapeDtypeStruct((L, rows, n_out), F32),
        grid=(L, n_out // tn),
        in_specs=[
            pl.BlockSpec((rows, D), lambda l, j: (0, 0)),
            pl.BlockSpec((1, D, rank), lambda l, j: (l, 0, 0)),
            pl.BlockSpec((1, rank, tn), lambda l, j: (l, 0, j)),
            pl.BlockSpec((1, 1, tn), lambda l, j: (l, 0, j)),
        ],
        out_specs=pl.BlockSpec((1, rows, tn), lambda l, j: (l, 0, j)),
        compiler_params=_cparams(2),
        name="adaln",
    )(cond, down, up, bias.reshape(L, 1, n_out))


def _norm_mod_kernel(x_ref, gain_ref, shift_ref, scale_ref, o_ref, *, tr, n_ctx):
    x = x_ref[0]
    var = jnp.mean(x * x, axis=-1, keepdims=True)
    y = x * lax.rsqrt(var + NORM_EPS) * gain_ref[...]
    row = pl.program_id(1) * tr + lax.broadcasted_iota(jnp.int32, (tr, 1), 0)
    is_ctx = row < n_ctx
    shift = jnp.where(is_ctx, shift_ref[0, 0:1, :], shift_ref[0, 1:2, :])
    scale = jnp.where(is_ctx, scale_ref[0, 0:1, :], scale_ref[0, 1:2, :])
    o_ref[0] = (y * (1.0 + scale) + shift).astype(o_ref.dtype)


def _norm_mod(xa, gain, shift, scale, n_ctx):
    B, R, D = xa.shape
    tr = _pick_tile(R, (256, 128))
    return pl.pallas_call(
        functools.partial(_norm_mod_kernel, tr=tr, n_ctx=n_ctx),
        out_shape=jax.ShapeDtypeStruct((B, R, D), BF16),
        grid=(B, R // tr),
        in_specs=[
            pl.BlockSpec((1, tr, D), lambda b, i: (b, i, 0)),
            pl.BlockSpec((1, D), lambda b, i: (0, 0)),
            pl.BlockSpec((1, 2, D), lambda b, i: (b, 0, 0)),
            pl.BlockSpec((1, 2, D), lambda b, i: (b, 0, 0)),
        ],
        out_specs=pl.BlockSpec((1, tr, D), lambda b, i: (b, i, 0)),
        compiler_params=_cparams(2),
        name="norm_mod",
    )(xa, gain.reshape(1, D), shift, scale)


def _rope_cols(a, cos, sin_signed, lane_lo):
    rot = jnp.where(lane_lo, pltpu.roll(a, HEAD_DIM - 32, 1), pltpu.roll(a, 32, 1))
    return a * cos + rot * sin_signed


def _in_ranges(x, ranges):
    return functools.reduce(jnp.logical_or, [(x >= lo) & (x < hi) for lo, hi in ranges])


def _inproj_kernel(h_ref, w_ref, cos_ref, sin_ref, o_ref, *, tn, sub, rope_ranges, qa_range, qb_range):
    j = pl.program_id(1)
    chunks = tn // HEAD_DIM
    first = j * chunks
    any_roped = functools.reduce(
        jnp.logical_or, [_in_ranges(first + c, rope_ranges) for c in range(chunks)])

    @pl.when(any_roped)
    def _():
        h = h_ref[...]
        cos = cos_ref[...]
        sin_signed = sin_ref[...]
        lane = lax.broadcasted_iota(jnp.int32, (1, HEAD_DIM), 1)
        lane_lo = (lane % ROPE_AXIS_DIM) < (ROPE_AXIS_DIM // 2)
        for c0 in range(0, tn, sub):
            acc = _dot(h, w_ref[0, :, c0:c0 + sub])
            for c in range(sub // HEAD_DIM):
                g = first + (c0 // HEAD_DIM + c)
                roped = _in_ranges(g, rope_ranges)
                mult = jnp.where(_in_ranges(g, (qa_range, qb_range)), Q_SCALE * LOG2E, 1.0).astype(F32)
                cos_g = jnp.where(roped, cos, 1.0) * mult
                sin_g = jnp.where(roped, sin_signed, 0.0) * mult
                a = acc[:, c * HEAD_DIM:(c + 1) * HEAD_DIM]
                col = c0 + c * HEAD_DIM
                o_ref[:, col:col + HEAD_DIM] = _rope_cols(a, cos_g, sin_g, lane_lo).astype(o_ref.dtype)

    @pl.when(jnp.logical_not(any_roped))
    def _():
        h = h_ref[...]
        for c0 in range(0, tn, sub):
            o_ref[:, c0:c0 + sub] = _dot(h, w_ref[0, :, c0:c0 + sub]).astype(o_ref.dtype)


def _inproj(h2d, w_all, layer, cos_tab, sin_tab, dims):
    M, D = h2d.shape
    N = w_all.shape[2]
    R = cos_tab.shape[0]
    tm = dims["tm_in"]
    tn = dims["tn_in"]
    tiles_per_batch = R // tm
    seg = dims["seg"]
    rng = lambda k: (seg[k][0] // HEAD_DIM, seg[k][1] // HEAD_DIM)
    rope_ranges = tuple(rng(k) for k in ("ka", "kb", "qa", "qb"))
    return pl.pallas_call(
        functools.partial(_inproj_kernel, tn=tn, sub=min(tn, 2 * HEAD_DIM), rope_ranges=rope_ranges,
                          qa_range=rng("qa"), qb_range=rng("qb")),
        out_shape=jax.ShapeDtypeStruct((M, N), BF16),
        grid=(M // tm, N // tn),
        in_specs=[
            pl.BlockSpec((tm, D), lambda i, j: (i, 0)),
            pl.BlockSpec((1, D, tn), lambda i, j: (layer, 0, j)),
            pl.BlockSpec((tm, HEAD_DIM), lambda i, j: (i % tiles_per_batch, 0)),
            pl.BlockSpec((tm, HEAD_DIM), lambda i, j: (i % tiles_per_batch, 0)),
        ],
        out_specs=pl.BlockSpec((tm, tn), lambda i, j: (i, j)),
        compiler_params=_cparams(2),
        name="inproj_rope",
    )(h2d, w_all, cos_tab, sin_tab)


def _win_attn_kernel(sink_ref, *refs, hkv, group, n_ctx_blocks, n_blocks):
    q_refs = refs[:hkv]
    kc_ref, vc_ref, kp_ref, ko_ref, kn_ref, vp_ref, vo_ref, vn_ref, o_ref = refs[hkv:]
    n = pl.program_id(1)
    rows = group * BLOCK
    qi = lax.broadcasted_iota(jnp.int32, (rows, BLOCK), 0) % BLOCK
    kj = lax.broadcasted_iota(jnp.int32, (rows, BLOCK), 1)
    is_lat = n >= n_ctx_blocks
    valid_p = jnp.logical_and(is_lat, n - 1 >= n_ctx_blocks)
    valid_n = jnp.logical_and(is_lat, n + 1 < n_blocks)
    mask = jnp.concatenate([
        jnp.logical_and(kj >= qi, valid_p),
        jnp.logical_and(kj >= 0, is_lat),
        jnp.logical_and(kj <= qi, valid_n)], axis=1)
    for h in range(hkv):
        cs = slice(h * HEAD_DIM, (h + 1) * HEAD_DIM)
        q = jnp.concatenate(
            [q_refs[h][0, :, g * HEAD_DIM:(g + 1) * HEAD_DIM] for g in range(group)], axis=0)
        sink = jnp.concatenate(
            [jnp.full((BLOCK, 1), sink_ref[h * group + g] * LOG2E, F32) for g in range(group)], axis=0)
        k_band = jnp.concatenate([kp_ref[0, :, cs], ko_ref[0, :, cs], kn_ref[0, :, cs]], axis=0)
        v_band = jnp.concatenate([vp_ref[0, :, cs], vo_ref[0, :, cs], vn_ref[0, :, cs]], axis=0)
        s_c = _dot_nt(q, kc_ref[0, :, cs])
        s_b = jnp.where(mask, _dot_nt(q, k_band), NEG_INF)
        m = jnp.maximum(jnp.maximum(jnp.max(s_c, axis=-1, keepdims=True),
                                    jnp.max(s_b, axis=-1, keepdims=True)), sink)
        e_c = jnp.exp2(s_c - m)
        e_b = jnp.exp2(s_b - m)
        denom = (jnp.sum(e_c, axis=-1, keepdims=True) + jnp.sum(e_b, axis=-1, keepdims=True)
                 + jnp.exp2(sink - m))
        out = (_dot(e_b.astype(BF16), v_band) + _dot(e_c.astype(BF16), vc_ref[0, :, cs])) / denom
        for g in range(group):
            col = (h * group + g) * HEAD_DIM
            o_ref[0, :, col:col + HEAD_DIM] = out[g * BLOCK:(g + 1) * BLOCK].astype(o_ref.dtype)


def _win_attn(px, sink, dims):
    B, R, _ = px.shape
    seg = dims["seg"]
    hkv, group, n_ctx = dims["hkv"], dims["group"], dims["n_ctx"]
    nb = R // BLOCK
    ncb = n_ctx // BLOCK
    qw = group * HEAD_DIM
    kw = hkv * HEAD_DIM
    k0 = seg["ka"][0] // kw
    v0 = seg["va"][0] // kw
    q0 = seg["qa"][0] // qw
    assert seg["qa"][0] % qw == 0 and seg["ka"][0] % kw == 0 and seg["va"][0] % kw == 0

    def q_spec(h):
        return pl.BlockSpec((1, BLOCK, qw), lambda b, n: (b, n, q0 + h))

    def band_spec(col, shift):
        return pl.BlockSpec((1, BLOCK, kw), lambda b, n: (b, jnp.clip(n + shift, 0, nb - 1), col))

    return pl.pallas_call(
        functools.partial(_win_attn_kernel, hkv=hkv, group=group, n_ctx_blocks=ncb, n_blocks=nb),
        out_shape=jax.ShapeDtypeStruct((B, R, hkv * qw), BF16),
        grid=(B, nb),
        in_specs=[pl.BlockSpec(memory_space=pltpu.SMEM)]
        + [q_spec(h) for h in range(hkv)]
        + [pl.BlockSpec((1, n_ctx, kw), lambda b, n: (b, 0, k0)),
           pl.BlockSpec((1, n_ctx, kw), lambda b, n: (b, 0, v0)),
           band_spec(k0, -1), band_spec(k0, 0), band_spec(k0, 1),
           band_spec(v0, -1), band_spec(v0, 0), band_spec(v0, 1)],
        out_specs=pl.BlockSpec((1, BLOCK, hkv * qw), lambda b, n: (b, n, 0)),
        compiler_params=_cparams(2),
        name="window_attn",
    )(sink, *([px] * (hkv + 8)))


def _diff_attn_kernel(lq1_ref, lk1_ref, lq2_ref, lk2_ref, gain_ref, q_ref, k_ref, v_ref, o_ref,
                      acc1_ref, acc2_ref, s_ref, *, tq, tk, n_ctx, n_rows, lam_init):
    lam = (jnp.exp(jnp.sum(lq1_ref[...] * lk1_ref[...], axis=-1, keepdims=True))
           - jnp.exp(jnp.sum(lq2_ref[...] * lk2_ref[...], axis=-1, keepdims=True)) + lam_init)
    gain = gain_ref[...] * (1.0 - lam_init)

    def finish(q_start, nq, l1, l2):
        o = acc1_ref[pl.ds(0, nq), :] / l1 - lam * (acc2_ref[pl.ds(0, nq), :] / l2)
        var = jnp.mean(o * o, axis=-1, keepdims=True)
        y = o * lax.rsqrt(var + NORM_EPS) * gain
        o_ref[0, pl.ds(q_start, nq), :] = y.astype(o_ref.dtype)

    def attend_online(q_start, nq, kv_len, chunk):
        q1 = q_ref[0, pl.ds(q_start, nq), :HEAD_DIM]
        q2 = q_ref[0, pl.ds(q_start, nq), HEAD_DIM:]
        accs = (acc1_ref.at[pl.ds(0, nq), :], acc2_ref.at[pl.ds(0, nq), :])

        def step(t, carry):
            start = pl.multiple_of(t * chunk, chunk)
            k = k_ref[0, pl.ds(start, chunk), :]
            v = v_ref[0, pl.ds(start, chunk), :]
            out = []
            for which, (q, acc) in enumerate(zip((q1, q2), accs)):
                m, l = carry[2 * which], carry[2 * which + 1]
                s = _dot_nt(q, k[:, which * HEAD_DIM:(which + 1) * HEAD_DIM])
                n = jnp.maximum(m, jnp.max(s, axis=-1, keepdims=True))
                alpha = jnp.exp2(m - n)
                p = jnp.exp2(s - n)
                acc[...] = alpha * acc[...] + _dot(p.astype(BF16), v)
                out += [n, alpha * l + jnp.sum(p, axis=-1, keepdims=True)]
            return tuple(out)

        for acc in accs:
            acc[...] = jnp.zeros((nq, 2 * HEAD_DIM), F32)
        carry = (jnp.full((nq, 1), -jnp.inf, F32), jnp.zeros((nq, 1), F32)) * 2
        if kv_len == chunk:
            carry = step(0, carry)
        else:
            carry = lax.fori_loop(0, kv_len // chunk, step, carry)
        finish(q_start, nq, carry[1], carry[3])

    def attend_bounded(q_start, nq, kv_len, chunk, key_norm):
        qs = tuple(q_ref[0, pl.ds(q_start, nq), w * HEAD_DIM:(w + 1) * HEAD_DIM] for w in range(2))
        accs = (acc1_ref.at[pl.ds(0, nq), :], acc2_ref.at[pl.ds(0, nq), :])
        bound = []
        for q, kn in zip(qs, key_norm):
            qf = q.astype(F32)
            bound.append(jnp.sqrt(jnp.sum(qf * qf, axis=-1, keepdims=True)) * kn)

        def scores(t, slot):
            k = k_ref[0, pl.ds(pl.multiple_of(t * chunk, chunk), chunk), :]
            for which in range(2):
                s_ref[slot, which, pl.ds(0, nq), pl.ds(0, chunk)] = _dot_nt(
                    qs[which], k[:, which * HEAD_DIM:(which + 1) * HEAD_DIM])

        def softmax_pv(t, slot, ls):
            v = v_ref[0, pl.ds(pl.multiple_of(t * chunk, chunk), chunk), :]
            out = []
            for which in range(2):
                p = jnp.exp2(s_ref[slot, which, pl.ds(0, nq), pl.ds(0, chunk)] - bound[which])
                accs[which][...] += _dot(p.astype(BF16), v)
                out.append(ls[which] + jnp.sum(p, axis=-1, keepdims=True))
            return tuple(out)

        for acc in accs:
            acc[...] = jnp.zeros((nq, 2 * HEAD_DIM), F32)
        ls = (jnp.zeros((nq, 1), F32),) * 2
        n_chunks = kv_len // chunk

        def pair(u, c):
            scores(2 * u + 1, 1)
            c = softmax_pv(2 * u, 0, c)
            scores(2 * u + 2, 0)
            return softmax_pv(2 * u + 1, 1, c)

        scores(0, 0)
        n_pairs = (n_chunks - 1) // 2
        if n_pairs > 0:
            ls = lax.fori_loop(0, n_pairs, pair, ls)
        if n_chunks % 2 == 0:
            scores(n_chunks - 1, 1)
            ls = softmax_pv(n_chunks - 2, 0, ls)
            ls = softmax_pv(n_chunks - 1, 1, ls)
        else:
            ls = softmax_pv(n_chunks - 1, 0, ls)
        return ls

    attend_online(0, n_ctx, n_ctx, n_ctx)

    def max_norm(t, mx):
        k = k_ref[0, pl.ds(pl.multiple_of(t * tk, tk), tk), :].astype(F32)
        sq = k * k
        return tuple(
            jnp.maximum(mx[w], jnp.max(jnp.sum(sq[:, w * HEAD_DIM:(w + 1) * HEAD_DIM], axis=-1, keepdims=True),
                                       axis=0, keepdims=True))
            for w in range(2))

    key_sq = lax.fori_loop(0, n_rows // tk, max_norm, (jnp.zeros((1, 1), F32),) * 2)
    key_norm = tuple(jnp.sqrt(x) for x in key_sq)
    align = math.gcd(n_ctx, tq)

    def latent_tile(i, _):
        q_start = pl.multiple_of(n_ctx + i * tq, align)
        l1, l2 = attend_bounded(q_start, tq, n_rows, tk, key_norm)
        smallest = jnp.min(jnp.minimum(l1, l2), axis=0, keepdims=True)[0, 0]
        safe = smallest >= MIN_SOFTMAX_DENOM

        @pl.when(safe)
        def _():
            finish(q_start, tq, l1, l2)

        @pl.when(jnp.logical_not(safe))
        def _():
            attend_online(q_start, tq, n_rows, tk)

        return 0

    lax.fori_loop(0, (n_rows - n_ctx) // tq, latent_tile, 0)


def _diff_attn(px, lq1, lk1, lq2, lk2, gain, lam_init, dims):
    B, R, _ = px.shape
    seg = dims["seg"]
    hb, n_ctx = dims["hb"], dims["n_ctx"]
    w = 2 * HEAD_DIM
    tq = _pick_tile(R - n_ctx, (512, 256, 128))
    tk = _pick_tile(R, (768, 512, 384, 256, 128))
    assert n_ctx <= tq and n_ctx % 16 == 0
    q0, k0, v0 = seg["qb"][0] // w, seg["kb"][0] // w, seg["vb"][0] // w
    vec = lambda a: a.reshape(1, -1)
    small = lambda n: pl.BlockSpec((1, n), lambda b, h: (0, 0))
    head = lambda col: pl.BlockSpec((1, R, w), lambda b, h: (b, 0, col + h))
    return pl.pallas_call(
        functools.partial(_diff_attn_kernel, tq=tq, tk=tk, n_ctx=n_ctx, n_rows=R, lam_init=lam_init),
        out_shape=jax.ShapeDtypeStruct((B, R, hb * w), BF16),
        grid=(B, hb),
        in_specs=[
            small(HEAD_DIM), small(HEAD_DIM), small(HEAD_DIM), small(HEAD_DIM), small(w),
            head(q0), head(k0), head(v0),
        ],
        out_specs=head(0),
        scratch_shapes=[pltpu.VMEM((tq, w), F32), pltpu.VMEM((tq, w), F32),
                        pltpu.VMEM((2, 2, tq, tk), F32)],
        compiler_params=_cparams(2),
        name="diff_attn",
    )(vec(lq1), vec(lk1), vec(lq2), vec(lk2), vec(gain), px, px, px)


def _merge_kernel(ya_ref, yb_ref, wa_ref, wb_ref, ga_ref, gb_ref, o_ref):
    pa = _dot(ya_ref[...], wa_ref[0])
    pb = _dot(yb_ref[...], wb_ref[0])
    m = _sigmoid(ga_ref[...].astype(F32)) * pa + _sigmoid(gb_ref[...].astype(F32)) * pb
    o_ref[...] = m.astype(o_ref.dtype)


def _merge(ya, yb, wa, wb, layer, px2d, dims):
    M = ya.shape[0]
    D = wa.shape[2]
    tm, tn = dims["tm"], dims["tn"]
    g0 = dims["seg"]["gate"][0] // tn
    g1 = g0 + D // tn
    return pl.pallas_call(
        _merge_kernel,
        out_shape=jax.ShapeDtypeStruct((M, D), BF16),
        grid=(M // tm, D // tn),
        in_specs=[
            pl.BlockSpec((tm, ya.shape[1]), lambda i, j: (i, 0)),
            pl.BlockSpec((tm, yb.shape[1]), lambda i, j: (i, 0)),
            pl.BlockSpec((1, wa.shape[1], tn), lambda i, j: (layer, 0, j)),
            pl.BlockSpec((1, wb.shape[1], tn), lambda i, j: (layer, 0, j)),
            pl.BlockSpec((tm, tn), lambda i, j: (i, g0 + j)),
            pl.BlockSpec((tm, tn), lambda i, j: (i, g1 + j)),
        ],
        out_specs=pl.BlockSpec((tm, tn), lambda i, j: (i, j)),
        compiler_params=_cparams(2),
        name="merge_branches",
    )(ya, yb, wa, wb, px2d, px2d)


def _row_gate(gate_ref, tile_in_batch, tm, n_ctx):
    row = tile_in_batch * tm + lax.broadcasted_iota(jnp.int32, (tm, 1), 0)
    return jnp.where(row < n_ctx, gate_ref[0, 0:1, :], gate_ref[0, 1:2, :])


def _outproj_kernel(m_ref, w_ref, x_ref, gate_ref, o_ref, *, tm, tiles_per_batch, n_ctx):
    gate = _row_gate(gate_ref, pl.program_id(0) % tiles_per_batch, tm, n_ctx)
    o_ref[...] = x_ref[...] + gate * _dot(m_ref[...], w_ref[0])


def _outproj(m, w, layer, x2d, gate, dims):
    M, D = x2d.shape
    tm, tn = dims["tm"], dims["tn"]
    tpb = dims["rows"] // tm
    return pl.pallas_call(
        functools.partial(_outproj_kernel, tm=tm, tiles_per_batch=tpb, n_ctx=dims["n_ctx"]),
        out_shape=jax.ShapeDtypeStruct((M, D), F32),
        grid=(M // tm, D // tn),
        in_specs=[
            pl.BlockSpec((tm, D), lambda i, j: (i, 0)),
            pl.BlockSpec((1, D, tn), lambda i, j: (layer, 0, j)),
            pl.BlockSpec((tm, tn), lambda i, j: (i, j)),
            pl.BlockSpec((1, 2, tn), lambda i, j: (i // tpb, 0, j)),
        ],
        out_specs=pl.BlockSpec((tm, tn), lambda i, j: (i, j)),
        input_output_aliases={2: 0},
        compiler_params=_cparams(2),
        name="outproj_residual",
    )(m, w, x2d, gate)


def _route(logits, bias, n_experts):
    per_group = n_experts // N_GROUPS
    scores = _sigmoid(logits)
    sel = scores + bias
    srow = [scores[e:e + 1, :] for e in range(n_experts)]
    row = [sel[e:e + 1, :] for e in range(n_experts)]
    best, gidx = None, None
    for g in range(N_GROUPS):
        mem = row[g * per_group:(g + 1) * per_group]
        gs = None
        for a in range(per_group):
            for b in range(a + 1, per_group):
                s = mem[a] + mem[b]
                gs = s if gs is None else jnp.maximum(gs, s)
        if best is None:
            best, gidx = gs, jnp.zeros_like(gs, dtype=jnp.int32)
        else:
            better = gs > best
            gidx = jnp.where(better, g, gidx)
            best = jnp.where(better, gs, best)
    masked = [jnp.where(gidx == (e // per_group), row[e], NEG_INF) for e in range(n_experts)]
    m1, i1 = masked[0], jnp.zeros_like(gidx)
    for e in range(1, n_experts):
        better = masked[e] > m1
        i1 = jnp.where(better, e, i1)
        m1 = jnp.where(better, masked[e], m1)
    m2 = jnp.full_like(m1, -jnp.inf)
    i2 = jnp.zeros_like(gidx)
    for e in range(n_experts):
        cand = jnp.where(i1 == e, -jnp.inf, masked[e])
        better = cand > m2
        i2 = jnp.where(better, e, i2)
        m2 = jnp.where(better, cand, m2)
    w1 = functools.reduce(jnp.add, [jnp.where(i1 == e, srow[e], 0.0) for e in range(n_experts)])
    w2 = functools.reduce(jnp.add, [jnp.where(i2 == e, srow[e], 0.0) for e in range(n_experts)])
    tot = w1 + w2
    return i1, i2, w1 / tot * ROUTED_SCALE, w2 / tot * ROUTED_SCALE


META_ROWS = 8


def _norm_router_kernel(x_ref, gain_ref, shift_ref, scale_ref, wr_ref, bias_ref,
                        h_ref, meta_ref, cnt_ref, carry_ref, *, tr, n_ctx, n_experts):
    first = jnp.logical_and(pl.program_id(0) == 0, pl.program_id(1) == 0)

    @pl.when(first)
    def _():
        carry_ref[...] = jnp.zeros_like(carry_ref)

    x = x_ref[0]
    var = jnp.mean(x * x, axis=-1, keepdims=True)
    y = x * lax.rsqrt(var + NORM_EPS) * gain_ref[...]
    row = pl.program_id(1) * tr + lax.broadcasted_iota(jnp.int32, (tr, 1), 0)
    is_ctx = row < n_ctx
    shift = jnp.where(is_ctx, shift_ref[0, 0:1, :], shift_ref[0, 1:2, :])
    scale = jnp.where(is_ctx, scale_ref[0, 0:1, :], scale_ref[0, 1:2, :])
    h = y * (1.0 + scale) + shift
    h_ref[0] = h

    logits = _dot_nt(wr_ref[...], h.astype(BF16))
    i1, i2, w1, w2 = _route(logits, bias_ref[...], n_experts)
    erow = lax.broadcasted_iota(jnp.int32, (n_experts, tr), 0)
    hit1 = erow == i1
    hit2 = erow == i2
    onehot = jnp.where(jnp.logical_or(hit1, hit2), 1.0, 0.0)
    before = (lax.broadcasted_iota(jnp.int32, (tr, tr), 0)
              < lax.broadcasted_iota(jnp.int32, (tr, tr), 1))
    prefix = _dot(onehot.astype(BF16), jnp.where(before, 1.0, 0.0).astype(BF16))
    seen = prefix + carry_ref[:, 0:1]
    r1 = jnp.sum(jnp.where(hit1, seen, 0.0), axis=0, keepdims=True)
    r2 = jnp.sum(jnp.where(hit2, seen, 0.0), axis=0, keepdims=True)
    carry_ref[...] = carry_ref[...] + jnp.sum(onehot, axis=1, keepdims=True)
    zero = jnp.zeros_like(w1)
    meta_ref[...] = jnp.concatenate(
        [i1.astype(F32), i2.astype(F32), w1, w2, r1, r2, zero, zero], axis=0)
    cnt_ref[...] = carry_ref[...]


def _norm_router(xa, gain, shift, scale, wr_t, bias, n_ctx):
    B, R, D = xa.shape
    E = wr_t.shape[0]
    tr = _pick_tile(R, (256, 128))
    nt = R // tr
    return pl.pallas_call(
        functools.partial(_norm_router_kernel, tr=tr, n_ctx=n_ctx, n_experts=E),
        out_shape=(jax.ShapeDtypeStruct((B, R, D), F32),
                   jax.ShapeDtypeStruct((META_ROWS, B * R), F32),
                   jax.ShapeDtypeStruct((E, HEAD_DIM), F32)),
        grid=(B, nt),
        in_specs=[
            pl.BlockSpec((1, tr, D), lambda b, i: (b, i, 0)),
            pl.BlockSpec((1, D), lambda b, i: (0, 0)),
            pl.BlockSpec((1, 2, D), lambda b, i: (b, 0, 0)),
            pl.BlockSpec((1, 2, D), lambda b, i: (b, 0, 0)),
            pl.BlockSpec((E, D), lambda b, i: (0, 0)),
            pl.BlockSpec((E, 1), lambda b, i: (0, 0)),
        ],
        out_specs=(pl.BlockSpec((1, tr, D), lambda b, i: (b, i, 0)),
                   pl.BlockSpec((META_ROWS, tr), lambda b, i: (0, b * nt + i)),
                   pl.BlockSpec((E, HEAD_DIM), lambda b, i: (0, 0))),
        scratch_shapes=[pltpu.VMEM((E, HEAD_DIM), F32)],
        compiler_params=_cparams(2),
        name="norm_router",
    )(xa, gain.reshape(1, D), shift, scale, wr_t, bias.reshape(E, 1).astype(F32))


SUBLANES = 8


def _row_gather(idx_vmem_ref, idx_smem, isem, src_hbm, buf, sem, slot, n_rows):
    cp = pltpu.make_async_copy(idx_vmem_ref.at[0, 0], idx_smem.at[pl.ds(slot * n_rows, n_rows)], isem)
    cp.start()
    cp.wait()

    def body(g, carry):
        base = slot * n_rows + g * SUBLANES
        for s in range(SUBLANES):
            t = idx_smem[base + s]
            pltpu.make_async_copy(src_hbm.at[pl.ds(t, 1), :], buf.at[slot, g, pl.ds(s, 1), :],
                                  sem.at[slot]).start()
        return carry

    lax.fori_loop(0, n_rows // SUBLANES, body, 0)


def _row_gather_wait(src_hbm, buf, sem, slot, n_rows):
    def body(g, carry):
        pltpu.make_async_copy(src_hbm.at[pl.ds(0, SUBLANES), :], buf.at[slot, g], sem.at[slot]).wait()
        return carry

    lax.fori_loop(0, n_rows // SUBLANES, body, 0)


def _gathered(buf, slot):
    rows = buf[slot]
    return rows.reshape(rows.shape[0] * SUBLANES, rows.shape[2])


def _expert_kernel(te_ref, nv_ref, tok_ref, tok_next_ref, h_hbm, wg_ref, wu_ref, wd_ref,
                   o_ref, buf, idx_smem, sem, isem, *, tm):
    j = pl.program_id(0)
    n_valid = nv_ref[0]
    slot = j % 2

    @pl.when(j == 0)
    def _():
        _row_gather(tok_ref, idx_smem, isem, h_hbm, buf, sem, 0, tm)

    @pl.when(j + 1 < n_valid)
    def _():
        _row_gather(tok_next_ref, idx_smem, isem, h_hbm, buf, sem, 1 - slot, tm)

    @pl.when(j < n_valid)
    def _():
        _row_gather_wait(h_hbm, buf, sem, slot, tm)
        h = _gathered(buf, slot).astype(BF16)
        g = _dot(h, wg_ref[0, 0])
        u = _dot(h, wu_ref[0, 0])
        a = (g * _sigmoid(g) * u).astype(BF16)
        o_ref[...] = _dot(a, wd_ref[0, 0])

    @pl.when(j >= n_valid)
    def _():
        o_ref[...] = jnp.zeros_like(o_ref)


def _experts(h2d, tok3, tile_expert, n_valid, wg, wu, wd, layer, tm):
    M, D = h2d.shape
    F = wg.shape[-1]
    n_tiles = tok3.shape[0]
    P = n_tiles * tm
    grid_spec = pltpu.PrefetchScalarGridSpec(
        num_scalar_prefetch=2,
        grid=(n_tiles,),
        in_specs=[
            pl.BlockSpec((1, 1, tm), lambda j, te, nv: (j, 0, 0)),
            pl.BlockSpec((1, 1, tm), lambda j, te, nv: (jnp.minimum(j + 1, n_tiles - 1), 0, 0)),
            pl.BlockSpec(memory_space=pl.ANY),
            pl.BlockSpec((1, 1, D, F), lambda j, te, nv: (layer, te[j], 0, 0)),
            pl.BlockSpec((1, 1, D, F), lambda j, te, nv: (layer, te[j], 0, 0)),
            pl.BlockSpec((1, 1, F, D), lambda j, te, nv: (layer, te[j], 0, 0)),
        ],
        out_specs=pl.BlockSpec((tm, D), lambda j, te, nv: (j, 0)),
        scratch_shapes=[pltpu.VMEM((2, tm // SUBLANES, SUBLANES, D), F32), pltpu.SMEM((2 * tm,), jnp.int32),
                        pltpu.SemaphoreType.DMA((2,)), pltpu.SemaphoreType.DMA],
    )
    return pl.pallas_call(
        functools.partial(_expert_kernel, tm=tm),
        out_shape=jax.ShapeDtypeStruct((P, D), F32),
        grid_spec=grid_spec,
        compiler_params=_cparams(1),
        name="moe_experts",
    )(tile_expert, n_valid, tok3, tok3, h2d, wg, wu, wd)


def _combine_kernel(p1_ref, p1n_ref, p2_ref, p2n_ref, y_hbm, x_ref, w_ref, gate_ref, *rest,
                    tm, tiles_per_batch, n_ctx, with_norm):
    if with_norm == "next":
        gain_ref, shift_ref, scale_ref, o_ref, h_ref = rest[:5]
    else:
        gain_ref, o_ref = rest[:2]
    buf1, buf2, idx1, idx2, sem1, sem2, isem = rest[-7:]
    j = pl.program_id(0)
    slot = j % 2

    @pl.when(j == 0)
    def _():
        _row_gather(p1_ref, idx1, isem, y_hbm, buf1, sem1, 0, tm)
        _row_gather(p2_ref, idx2, isem, y_hbm, buf2, sem2, 0, tm)

    @pl.when(j + 1 < pl.num_programs(0))
    def _():
        _row_gather(p1n_ref, idx1, isem, y_hbm, buf1, sem1, 1 - slot, tm)
        _row_gather(p2n_ref, idx2, isem, y_hbm, buf2, sem2, 1 - slot, tm)

    _row_gather_wait(y_hbm, buf1, sem1, slot, tm)
    _row_gather_wait(y_hbm, buf2, sem2, slot, tm)
    step_rows = 2 * SUBLANES
    ctx_tile = (j % tiles_per_batch) * tm < n_ctx
    pick = lambda ref: jnp.where(ctx_tile, ref[0, 0:1, :], ref[0, 1:2, :])
    gate = pick(gate_ref)
    if with_norm == "next":
        gain_scale = gain_ref[...] * (1.0 + pick(scale_ref))
        shift = pick(shift_ref)

    def rows_step(i, carry):
        rows = pl.ds(pl.multiple_of(i * step_rows, step_rows), step_rows)
        b1 = jnp.concatenate([buf1[slot, 2 * i], buf1[slot, 2 * i + 1]], axis=0)
        b2 = jnp.concatenate([buf2[slot, 2 * i], buf2[slot, 2 * i + 1]], axis=0)
        w = w_ref[rows, :]
        x = x_ref[rows, :] + gate * (w[:, 0:1] * b1 + w[:, 1:2] * b2)
        normed = x * lax.rsqrt(jnp.mean(x * x, axis=-1, keepdims=True) + NORM_EPS)
        if with_norm == "next":
            o_ref[rows, :] = x
            h_ref[rows, :] = (normed * gain_scale + shift).astype(h_ref.dtype)
        else:
            o_ref[0, rows, :] = normed * gain_ref[...]
        return carry

    lax.fori_loop(0, tm // step_rows, rows_step, 0, unroll=4)


def _combine(x2d, y_sorted, pos1, pos2, w12, gate, dims, next_norm=None, final_gain=None):
    M, D = x2d.shape
    tm = dims["tm_comb"]
    n_ctx = dims["n_ctx"]
    assert n_ctx % tm == 0
    n_tiles = M // tm
    tpb = dims["rows"] // tm
    p1 = pos1.reshape(n_tiles, 1, tm)
    p2 = pos2.reshape(n_tiles, 1, tm)
    cur = pl.BlockSpec((1, 1, tm), lambda j: (j, 0, 0))
    nxt = pl.BlockSpec((1, 1, tm), lambda j: (jnp.minimum(j + 1, n_tiles - 1), 0, 0))
    rows = pl.BlockSpec((tm, D), lambda j: (j, 0))
    per_batch = pl.BlockSpec((1, 2, D), lambda j: (j // tpb, 0, 0))
    one_row = pl.BlockSpec((1, D), lambda j: (0, 0))
    if next_norm is not None:
        with_norm = "next"
        gain, shift, scale = next_norm
        extra_in = [one_row, per_batch, per_batch]
        extra_args = [gain.reshape(1, D), shift, scale]
        out_shape = (jax.ShapeDtypeStruct((M, D), F32), jax.ShapeDtypeStruct((M, D), BF16))
        out_specs = (rows, rows)
        aliases = {5: 0}
    else:
        with_norm = "final"
        extra_in = [one_row]
        extra_args = [final_gain.reshape(1, D)]
        ctx_tiles = n_ctx // tm
        out_shape = jax.ShapeDtypeStruct((M // dims["rows"], dims["rows"] - n_ctx, D), F32)
        out_specs = pl.BlockSpec(
            (1, tm, D), lambda j: (j // tpb, jnp.maximum(j % tpb - ctx_tiles, 0), 0))
        aliases = {}
    return pl.pallas_call(
        functools.partial(_combine_kernel, tm=tm, tiles_per_batch=tpb, n_ctx=dims["n_ctx"],
                          with_norm=with_norm),
        out_shape=out_shape,
        grid=(n_tiles,),
        in_specs=[cur, nxt, cur, nxt,
                  pl.BlockSpec(memory_space=pl.ANY),
                  rows,
                  pl.BlockSpec((tm, 2), lambda j: (j, 0)),
                  per_batch] + extra_in,
        out_specs=out_specs,
        scratch_shapes=[pltpu.VMEM((2, tm // SUBLANES, SUBLANES, D), F32),
                        pltpu.VMEM((2, tm // SUBLANES, SUBLANES, D), F32),
                        pltpu.SMEM((2 * tm,), jnp.int32), pltpu.SMEM((2 * tm,), jnp.int32),
                        pltpu.SemaphoreType.DMA((2,)), pltpu.SemaphoreType.DMA((2,)),
                        pltpu.SemaphoreType.DMA],
        input_output_aliases=aliases,
        compiler_params=_cparams(1),
        name="moe_combine",
    )(p1, p1, p2, p2, y_sorted, x2d, w12, gate, *extra_args)


def _dispatch_plan(meta, counts, tm, n_tiles):
    M = meta.shape[1]
    i1 = meta[0].astype(jnp.int32)
    i2 = meta[1].astype(jnp.int32)
    r1 = meta[4].astype(jnp.int32)
    r2 = meta[5].astype(jnp.int32)
    cnt = counts[:, 0].astype(jnp.int32)
    padded = ((cnt + tm - 1) // tm) * tm
    seg_end = jnp.cumsum(padded)
    seg_start = seg_end - padded
    pos1 = seg_start[i1] + r1
    pos2 = seg_start[i2] + r2
    n_valid = seg_end[-1] // tm
    tile_start = jnp.arange(n_tiles, dtype=jnp.int32) * tm
    probe = jnp.minimum(tile_start, seg_end[-1] - tm)
    tile_expert = jnp.sum(probe[:, None] >= seg_end[None, :], axis=1).astype(jnp.int32)
    tok = jnp.arange(M, dtype=jnp.int32)
    tok_sorted = jnp.zeros((n_tiles * tm,), jnp.int32).at[jnp.concatenate([pos1, pos2])].set(
        jnp.concatenate([tok, tok]), unique_indices=True)
    w12 = jnp.stack([meta[2], meta[3]], axis=1)
    return (pos1, pos2, w12, tok_sorted.reshape(n_tiles, 1, tm), tile_expert,
            n_valid.reshape(1).astype(jnp.int32))


def _rope_tables(seq, n_ctx):
    rows = seq // GRID_W
    row = jnp.repeat(jnp.arange(rows), GRID_W).astype(F32)
    col = jnp.tile(jnp.arange(GRID_W), rows).astype(F32)
    inv = ROPE_THETA ** (-(jnp.arange(0, ROPE_AXIS_DIM, 2, dtype=F32) / ROPE_AXIS_DIM))
    ang_r = row[:, None] * inv
    ang_c = col[:, None] * inv
    ang = jnp.concatenate([ang_r, ang_r, ang_c, ang_c], axis=-1)
    lane = jnp.arange(HEAD_DIM)
    sign = jnp.where((lane % ROPE_AXIS_DIM) < (ROPE_AXIS_DIM // 2), -1.0, 1.0).astype(F32)
    cos = jnp.concatenate([jnp.ones((n_ctx, HEAD_DIM), F32), jnp.cos(ang)], axis=0)
    sin = jnp.concatenate([jnp.zeros((n_ctx, HEAD_DIM), F32), jnp.sin(ang) * sign], axis=0)
    return cos, sin


def _dims(D, S, C, d_expert):
    ha = D // 256
    hkv = ha // 4
    hb = D // 512
    widths = [("ka", hkv * HEAD_DIM), ("va", hkv * HEAD_DIM), ("kb", hb * 2 * HEAD_DIM),
              ("vb", hb * 2 * HEAD_DIM), ("qa", ha * HEAD_DIM), ("qb", hb * 2 * HEAD_DIM),
              ("gate", 2 * D)]
    seg, off = {}, 0
    for name, wdt in widths:
        seg[name] = (off, off + wdt)
        off += wdt
    R = C + S
    tm = _pick_tile(R, (768, 512, 384, 256, 128))
    tn = _pick_tile(D, (512, 256, 128))
    tn_in = _pick_tile(off, (1024, 512, 256, 128))
    tm_in = _pick_tile(R, (1056, 768, 512, 384, 256, 128))
    tm_comb = _pick_tile(R, (256, 128))
    return dict(ha=ha, hkv=hkv, group=ha // hkv, hb=hb, seg=seg, n_cols=off, rows=R, n_ctx=C,
                tm=tm, tn=tn, tn_in=tn_in, tm_in=tm_in, tm_exp=256, tm_comb=tm_comb)


def kernel(x, c, ctx, c_ctx, ada_down, ada_up, ada_bias, norm_mix, norm_ffn, w_in, sink_logit,
           lam_q1, lam_k1, lam_q2, lam_k2, subln_gain, w_branch_a, w_branch_b, w_out,
           w_router, b_router, w_exp_gate, w_exp_up, w_exp_down, norm_final):
    B, S, D = x.shape
    C = ctx.shape[1]
    depth = w_in.shape[0]
    dims = _dims(D, S, C, w_exp_gate.shape[-1])
    R = dims["rows"]
    M = B * R

    cond = jnp.concatenate([c, c_ctx[None], jnp.zeros((8 - (B + 1) % 8 if (B + 1) % 8 else 0, D), F32)])
    mods = _adaln(cond, ada_down, ada_up, ada_bias)
    mods = mods.reshape(depth, mods.shape[1], N_MOD, D)
    lat = mods[:, :B]
    cx = jnp.broadcast_to(mods[:, B:B + 1], lat.shape)
    mod = jnp.stack([cx, lat], axis=2)

    cos_tab, sin_tab = _rope_tables(S, C)
    wr_t = w_router.T.astype(BF16)
    xa = jnp.concatenate([ctx, x], axis=1)
    w_in_b, wa_b, wb_b, wo_b = (w.astype(BF16) for w in (w_in, w_branch_a, w_branch_b, w_out))
    wg_b, wu_b, wd_b = (w.astype(BF16) for w in (w_exp_gate, w_exp_up, w_exp_down))
    n_experts = w_router.shape[1]
    tm_exp = dims["tm_exp"]
    n_exp_tiles = -(-(2 * M + n_experts * (tm_exp - 1)) // tm_exp)

    mods_of = lambda l: [mod[l, :, :, k] for k in range(N_MOD)]
    h_mix = _norm_mod(xa, norm_mix[0], mods_of(0)[0], mods_of(0)[1], C).reshape(M, D)
    for l in range(depth):
        lam_init = 0.8 - 0.6 * math.exp(-0.3 * l)
        m = mods_of(l)
        px = _inproj(h_mix, w_in_b, l, cos_tab, sin_tab, dims)
        px3 = px.reshape(B, R, -1)
        ya = _win_attn(px3, sink_logit[l], dims)
        yb = _diff_attn(px3, lam_q1[l], lam_k1[l], lam_q2[l], lam_k2[l], subln_gain[l], lam_init, dims)
        mg = _merge(ya.reshape(M, -1), yb.reshape(M, -1), wa_b, wb_b, l, px, dims)
        x2d = _outproj(mg, wo_b, l, xa.reshape(M, D), m[2], dims)
        xa = x2d.reshape(B, R, D)
        h, meta, counts = _norm_router(xa, norm_ffn[l], m[3], m[4], wr_t, b_router, C)
        pos1, pos2, w12, tok3, tile_expert, n_valid = _dispatch_plan(meta, counts, tm_exp, n_exp_tiles)
        y_sorted = _experts(h.reshape(M, D), tok3, tile_expert, n_valid, wg_b, wu_b, wd_b, l, tm_exp)
        if l + 1 < depth:
            nm = mods_of(l + 1)
            x2d, h_mix = _combine(x2d, y_sorted, pos1, pos2, w12, m[5], dims,
                                  next_norm=(norm_mix[l + 1], nm[0], nm[1]))
            xa = x2d.reshape(B, R, D)
        else:
            return _combine(x2d, y_sorted, pos1, pos2, w12, m[5], dims, final_gain=norm_final)
```

```python
import functools
import math

import jax
import jax.numpy as jnp
from jax import lax
from jax.experimental import pallas as pl
from jax.experimental.pallas import tpu as pltpu

HEAD_DIM = 128
GRID_W = 64
ROPE_AXIS_DIM = HEAD_DIM // 2
ROPE_THETA = 10000.0
BLOCK = 128
N_GROUPS = 4
ROUTED_SCALE = 1.0
NORM_EPS = 1e-6
NEG_INF = -1e30
N_MOD = 6
Q_SCALE = HEAD_DIM ** -0.5
LOG2E = math.log2(math.e)
MIN_SOFTMAX_DENOM = 2.0 ** -80

F32 = jnp.float32
BF16 = jnp.bfloat16

VMEM_LIMIT_BYTES = 56 * 1024 * 1024


def _cparams(n_axes):
    return pltpu.CompilerParams(dimension_semantics=("arbitrary",) * n_axes,
                                vmem_limit_bytes=VMEM_LIMIT_BYTES)


def _pick_tile(total, candidates):
    for t in candidates:
        if total % t == 0:
            return t
    raise ValueError(f"no tile in {candidates} divides {total}")


def _dot(a, b):
    return jnp.dot(a, b, preferred_element_type=F32)


def _dot_nt(a, b):
    return lax.dot_general(a, b, (((1,), (1,)), ((), ())), preferred_element_type=F32)


def _sigmoid(x):
    return 1.0 / (1.0 + jnp.exp(-x))


def _adaln_kernel(cond_ref, down_ref, up_ref, bias_ref, o_ref):
    cond = cond_ref[...]
    s = (cond * _sigmoid(cond)).astype(BF16)
    t = _dot(s, down_ref[0].astype(BF16)).astype(BF16)
    o_ref[0] = _dot(t, up_ref[0].astype(BF16)) + bias_ref[0]


def _adaln(cond, down, up, bias):
    L, D, rank = down.shape
    n_out = up.shape[-1]
    rows = cond.shape[0]
    tn = _pick_tile(n_out, (2048, 1024, 512, 256, 128))
    return pl.pallas_call(
        _adaln_kernel,
        out_shape=jax.ShapeDtypeStruct((L, rows, n_out), F32),
        grid=(L, n_out // tn),
        in_specs=[
            pl.BlockSpec((rows, D), lambda l, j: (0, 0)),
            pl.BlockSpec((1, D, rank), lambda l, j: (l, 0, 0)),
            pl.BlockSpec((1, rank, tn), lambda l, j: (l, 0, j)),
            pl.BlockSpec((1, 1, tn), lambda l, j: (l, 0, j)),
        ],
        out_specs=pl.BlockSpec((1, rows, tn), lambda l, j: (l, 0, j)),
        compiler_params=_cparams(2),
        name="adaln",
    )(cond, down, up, bias.reshape(L, 1, n_out))


def _norm_mod_kernel(x_ref, gain_ref, shift_ref, scale_ref, o_ref, *, tr, n_ctx):
    x = x_ref[0]
    var = jnp.mean(x * x, axis=-1, keepdims=True)
    y = x * lax.rsqrt(var + NORM_EPS) * gain_ref[...]
    row = pl.program_id(1) * tr + lax.broadcasted_iota(jnp.int32, (tr, 1), 0)
    is_ctx = row < n_ctx
    shift = jnp.where(is_ctx, shift_ref[0, 0:1, :], shift_ref[0, 1:2, :])
    scale = jnp.where(is_ctx, scale_ref[0, 0:1, :], scale_ref[0, 1:2, :])
    o_ref[0] = (y * (1.0 + scale) + shift).astype(o_ref.dtype)


def _norm_mod(xa, gain, shift, scale, n_ctx):
    B, R, D = xa.shape
    tr = _pick_tile(R, (256, 128))
    return pl.pallas_call(
        functools.partial(_norm_mod_kernel, tr=tr, n_ctx=n_ctx),
        out_shape=jax.ShapeDtypeStruct((B, R, D), BF16),
        grid=(B, R // tr),
        in_specs=[
            pl.BlockSpec((1, tr, D), lambda b, i: (b, i, 0)),
            pl.BlockSpec((1, D), lambda b, i: (0, 0)),
            pl.BlockSpec((1, 2, D), lambda b, i: (b, 0, 0)),
            pl.BlockSpec((1, 2, D), lambda b, i: (b, 0, 0)),
        ],
        out_specs=pl.BlockSpec((1, tr, D), lambda b, i: (b, i, 0)),
        compiler_params=_cparams(2),
        name="norm_mod",
    )(xa, gain.reshape(1, D), shift, scale)


def _rope_cols(a, cos, sin_signed, lane_lo):
    rot = jnp.where(lane_lo, pltpu.roll(a, HEAD_DIM - 32, 1), pltpu.roll(a, 32, 1))
    return a * cos + rot * sin_signed


def _in_ranges(x, ranges):
    return functools.reduce(jnp.logical_or, [(x >= lo) & (x < hi) for lo, hi in ranges])


def _inproj_kernel(h_ref, w_ref, cos_ref, sin_ref, o_ref, *, tn, sub, rope_ranges, qa_range, qb_range):
    j = pl.program_id(1)
    chunks = tn // HEAD_DIM
    first = j * chunks
    any_roped = functools.reduce(
        jnp.logical_or, [_in_ranges(first + c, rope_ranges) for c in range(chunks)])

    @pl.when(any_roped)
    def _():
        h = h_ref[...]
        cos = cos_ref[...]
        sin_signed = sin_ref[...]
        lane = lax.broadcasted_iota(jnp.int32, (1, HEAD_DIM), 1)
        lane_lo = (lane % ROPE_AXIS_DIM) < (ROPE_AXIS_DIM // 2)
        for c0 in range(0, tn, sub):
            acc = _dot(h, w_ref[0, :, c0:c0 + sub])
            for c in range(sub // HEAD_DIM):
                g = first + (c0 // HEAD_DIM + c)
                roped = _in_ranges(g, rope_ranges)
                mult = jnp.where(_in_ranges(g, (qa_range, qb_range)), Q_SCALE * LOG2E, 1.0).astype(F32)
                cos_g = jnp.where(roped, cos, 1.0) * mult
                sin_g = jnp.where(roped, sin_signed, 0.0) * mult
                a = acc[:, c * HEAD_DIM:(c + 1) * HEAD_DIM]
                col = c0 + c * HEAD_DIM
                o_ref[:, col:col + HEAD_DIM] = _rope_cols(a, cos_g, sin_g, lane_lo).astype(o_ref.dtype)

    @pl.when(jnp.logical_not(any_roped))
    def _():
        h = h_ref[...]
        for c0 in range(0, tn, sub):
            o_ref[:, c0:c0 + sub] = _dot(h, w_ref[0, :, c0:c0 + sub]).astype(o_ref.dtype)


def _inproj(h2d, w_all, layer, cos_tab, sin_tab, dims):
    M, D = h2d.shape
    N = w_all.shape[2]
    R = cos_tab.shape[0]
    tm = dims["tm_in"]
    tn = dims["tn_in"]
    tiles_per_batch = R // tm
    seg = dims["seg"]
    rng = lambda k: (seg[k][0] // HEAD_DIM, seg[k][1] // HEAD_DIM)
    rope_ranges = tuple(rng(k) for k in ("ka", "kb", "qa", "qb"))
    return pl.pallas_call(
        functools.partial(_inproj_kernel, tn=tn, sub=min(tn, 2 * HEAD_DIM), rope_ranges=rope_ranges,
                          qa_range=rng("qa"), qb_range=rng("qb")),
        out_shape=jax.ShapeDtypeStruct((M, N), BF16),
        grid=(M // tm, N // tn),
        in_specs=[
            pl.BlockSpec((tm, D), lambda i, j: (i, 0)),
            pl.BlockSpec((1, D, tn), lambda i, j: (layer, 0, j)),
            pl.BlockSpec((tm, HEAD_DIM), lambda i, j: (i % tiles_per_batch, 0)),
            pl.BlockSpec((tm, HEAD_DIM), lambda i, j: (i % tiles_per_batch, 0)),
        ],
        out_specs=pl.BlockSpec((tm, tn), lambda i, j: (i, j)),
        compiler_params=_cparams(2),
        name="inproj_rope",
    )(h2d, w_all, cos_tab, sin_tab)


def _win_attn_kernel(sink_ref, *refs, hkv, group, n_ctx_blocks, n_blocks):
    q_refs = refs[:hkv]
    kc_ref, vc_ref, kp_ref, ko_ref, kn_ref, vp_ref, vo_ref, vn_ref, o_ref = refs[hkv:]
    n = pl.program_id(1)
    rows = group * BLOCK
    qi = lax.broadcasted_iota(jnp.int32, (rows, BLOCK), 0) % BLOCK
    kj = lax.broadcasted_iota(jnp.int32, (rows, BLOCK), 1)
    is_lat = n >= n_ctx_blocks
    valid_p = jnp.logical_and(is_lat, n - 1 >= n_ctx_blocks)
    valid_n = jnp.logical_and(is_lat, n + 1 < n_blocks)
    mask = jnp.concatenate([
        jnp.logical_and(kj >= qi, valid_p),
        jnp.logical_and(kj >= 0, is_lat),
        jnp.logical_and(kj <= qi, valid_n)], axis=1)
    for h in range(hkv):
        cs = slice(h * HEAD_DIM, (h + 1) * HEAD_DIM)
        q = jnp.concatenate(
            [q_refs[h][0, :, g * HEAD_DIM:(g + 1) * HEAD_DIM] for g in range(group)], axis=0)
        sink = jnp.concatenate(
            [jnp.full((BLOCK, 1), sink_ref[h * group + g] * LOG2E, F32) for g in range(group)], axis=0)
        k_band = jnp.concatenate([kp_ref[0, :, cs], ko_ref[0, :, cs], kn_ref[0, :, cs]], axis=0)
        v_band = jnp.concatenate([vp_ref[0, :, cs], vo_ref[0, :, cs], vn_ref[0, :, cs]], axis=0)
        s_c = _dot_nt(q, kc_ref[0, :, cs])
        s_b = jnp.where(mask, _dot_nt(q, k_band), NEG_INF)
        m = jnp.maximum(jnp.maximum(jnp.max(s_c, axis=-1, keepdims=True),
                                    jnp.max(s_b, axis=-1, keepdims=True)), sink)
        e_c = jnp.exp2(s_c - m)
        e_b = jnp.exp2(s_b - m)
        denom = (jnp.sum(e_c, axis=-1, keepdims=True) + jnp.sum(e_b, axis=-1, keepdims=True)
                 + jnp.exp2(sink - m))
        out = (_dot(e_b.astype(BF16), v_band) + _dot(e_c.astype(BF16), vc_ref[0, :, cs])) / denom
        for g in range(group):
            col = (h * group + g) * HEAD_DIM
            o_ref[0, :, col:col + HEAD_DIM] = out[g * BLOCK:(g + 1) * BLOCK].astype(o_ref.dtype)


def _win_attn(px, sink, dims):
    B, R, _ = px.shape
    seg = dims["seg"]
    hkv, group, n_ctx = dims["hkv"], dims["group"], dims["n_ctx"]
    nb = R // BLOCK
    ncb = n_ctx // BLOCK
    qw = group * HEAD_DIM
    kw = hkv * HEAD_DIM
    k0 = seg["ka"][0] // kw
    v0 = seg["va"][0] // kw
    q0 = seg["qa"][0] // qw
    assert seg["qa"][0] % qw == 0 and seg["ka"][0] % kw == 0 and seg["va"][0] % kw == 0

    def q_spec(h):
        return pl.BlockSpec((1, BLOCK, qw), lambda b, n: (b, n, q0 + h))

    def band_spec(col, shift):
        return pl.BlockSpec((1, BLOCK, kw), lambda b, n: (b, jnp.clip(n + shift, 0, nb - 1), col))

    return pl.pallas_call(
        functools.partial(_win_attn_kernel, hkv=hkv, group=group, n_ctx_blocks=ncb, n_blocks=nb),
        out_shape=jax.ShapeDtypeStruct((B, R, hkv * qw), BF16),
        grid=(B, nb),
        in_specs=[pl.BlockSpec(memory_space=pltpu.SMEM)]
        + [q_spec(h) for h in range(hkv)]
        + [pl.BlockSpec((1, n_ctx, kw), lambda b, n: (b, 0, k0)),
           pl.BlockSpec((1, n_ctx, kw), lambda b, n: (b, 0, v0)),
           band_spec(k0, -1), band_spec(k0, 0), band_spec(k0, 1),
           band_spec(v0, -1), band_spec(v0, 0), band_spec(v0, 1)],
        out_specs=pl.BlockSpec((1, BLOCK, hkv * qw), lambda b, n: (b, n, 0)),
        compiler_params=_cparams(2),
        name="window_attn",
    )(sink, *([px] * (hkv + 8)))


def _diff_attn_kernel(lq1_ref, lk1_ref, lq2_ref, lk2_ref, gain_ref, q_ref, k_ref, v_ref, o_ref,
                      acc1_ref, acc2_ref, s_ref, *, tq, tk, n_ctx, n_rows, lam_init):
    lam = (jnp.exp(jnp.sum(lq1_ref[...] * lk1_ref[...], axis=-1, keepdims=True))
           - jnp.exp(jnp.sum(lq2_ref[...] * lk2_ref[...], axis=-1, keepdims=True)) + lam_init)
    gain = gain_ref[...] * (1.0 - lam_init)

    def finish(q_start, nq, l1, l2):
        o = acc1_ref[pl.ds(0, nq), :] / l1 - lam * (acc2_ref[pl.ds(0, nq), :] / l2)
        var = jnp.mean(o * o, axis=-1, keepdims=True)
        y = o * lax.rsqrt(var + NORM_EPS) * gain
        o_ref[0, pl.ds(q_start, nq), :] = y.astype(o_ref.dtype)

    def attend_online(q_start, nq, kv_len, chunk):
        q1 = q_ref[0, pl.ds(q_start, nq), :HEAD_DIM]
        q2 = q_ref[0, pl.ds(q_start, nq), HEAD_DIM:]
        accs = (acc1_ref.at[pl.ds(0, nq), :], acc2_ref.at[pl.ds(0, nq), :])

        def step(t, carry):
            start = pl.multiple_of(t * chunk, chunk)
            k = k_ref[0, pl.ds(start, chunk), :]
            v = v_ref[0, pl.ds(start, chunk), :]
            out = []
            for which, (q, acc) in enumerate(zip((q1, q2), accs)):
                m, l = carry[2 * which], carry[2 * which + 1]
                s = _dot_nt(q, k[:, which * HEAD_DIM:(which + 1) * HEAD_DIM])
                n = jnp.maximum(m, jnp.max(s, axis=-1, keepdims=True))
                alpha = jnp.exp2(m - n)
                p = jnp.exp2(s - n)
                acc[...] = alpha * acc[...] + _dot(p.astype(BF16), v)
                out += [n, alpha * l + jnp.sum(p, axis=-1, keepdims=True)]
            return tuple(out)

        for acc in accs:
            acc[...] = jnp.zeros((nq, 2 * HEAD_DIM), F32)
        carry = (jnp.full((nq, 1), -jnp.inf, F32), jnp.zeros((nq, 1), F32)) * 2
        if kv_len == chunk:
            carry = step(0, carry)
        else:
            carry = lax.fori_loop(0, kv_len // chunk, step, carry)
        finish(q_start, nq, carry[1], carry[3])

    def attend_bounded(q_start, nq, kv_len, chunk, key_norm):
        qs = tuple(q_ref[0, pl.ds(q_start, nq), w * HEAD_DIM:(w + 1) * HEAD_DIM] for w in range(2))
        accs = (acc1_ref.at[pl.ds(0, nq), :], acc2_ref.at[pl.ds(0, nq), :])
        bound = []
        for q, kn in zip(qs, key_norm):
            qf = q.astype(F32)
            bound.append(jnp.sqrt(jnp.sum(qf * qf, axis=-1, keepdims=True)) * kn)

        def scores(t, slot):
            k = k_ref[0, pl.ds(pl.multiple_of(t * chunk, chunk), chunk), :]
            for which in range(2):
                s_ref[slot, which, pl.ds(0, nq), pl.ds(0, chunk)] = _dot_nt(
                    qs[which], k[:, which * HEAD_DIM:(which + 1) * HEAD_DIM])

        def softmax_pv(t, slot, ls):
            v = v_ref[0, pl.ds(pl.multiple_of(t * chunk, chunk), chunk), :]
            out = []
            for which in range(2):
                p = jnp.exp2(s_ref[slot, which, pl.ds(0, nq), pl.ds(0, chunk)] - bound[which])
                accs[which][...] += _dot(p.astype(BF16), v)
                out.append(ls[which] + jnp.sum(p, axis=-1, keepdims=True))
            return tuple(out)

        for acc in accs:
            acc[...] = jnp.zeros((nq, 2 * HEAD_DIM), F32)
        ls = (jnp.zeros((nq, 1), F32),) * 2
        n_chunks = kv_len // chunk

        def pair(u, c):
            scores(2 * u + 1, 1)
            c = softmax_pv(2 * u, 0, c)
            scores(2 * u + 2, 0)
            return softmax_pv(2 * u + 1, 1, c)

        scores(0, 0)
        n_pairs = (n_chunks - 1) // 2
        if n_pairs > 0:
            ls = lax.fori_loop(0, n_pairs, pair, ls)
        if n_chunks % 2 == 0:
            scores(n_chunks - 1, 1)
            ls = softmax_pv(n_chunks - 2, 0, ls)
            ls = softmax_pv(n_chunks - 1, 1, ls)
        else:
            ls = softmax_pv(n_chunks - 1, 0, ls)
        return ls

    attend_online(0, n_ctx, n_ctx, n_ctx)

    def max_norm(t, mx):
        k = k_ref[0, pl.ds(pl.multiple_of(t * tk, tk), tk), :].astype(F32)
        sq = k * k
        return tuple(
            jnp.maximum(mx[w], jnp.max(jnp.sum(sq[:, w * HEAD_DIM:(w + 1) * HEAD_DIM], axis=-1, keepdims=True),
                                       axis=0, keepdims=True))
            for w in range(2))

    key_sq = lax.fori_loop(0, n_rows // tk, max_norm, (jnp.zeros((1, 1), F32),) * 2)
    key_norm = tuple(jnp.sqrt(x) for x in key_sq)
    align = math.gcd(n_ctx, tq)

    def latent_tile(i, _):
        q_start = pl.multiple_of(n_ctx + i * tq, align)
        l1, l2 = attend_bounded(q_start, tq, n_rows, tk, key_norm)
        smallest = jnp.min(jnp.minimum(l1, l2), axis=0, keepdims=True)[0, 0]
        safe = smallest >= MIN_SOFTMAX_DENOM

        @pl.when(safe)
        def _():
            finish(q_start, tq, l1, l2)

        @pl.when(jnp.logical_not(safe))
        def _():
            attend_online(q_start, tq, n_rows, tk)

        return 0

    lax.fori_loop(0, (n_rows - n_ctx) // tq, latent_tile, 0)


def _diff_attn(px, lq1, lk1, lq2, lk2, gain, lam_init, dims):
    B, R, _ = px.shape
    seg = dims["seg"]
    hb, n_ctx = dims["hb"], dims["n_ctx"]
    w = 2 * HEAD_DIM
    tq = _pick_tile(R - n_ctx, (512, 256, 128))
    tk = _pick_tile(R, (768, 512, 384, 256, 128))
    assert n_ctx <= tq and n_ctx % 16 == 0
    q0, k0, v0 = seg["qb"][0] // w, seg["kb"][0] // w, seg["vb"][0] // w
    vec = lambda a: a.reshape(1, -1)
    small = lambda n: pl.BlockSpec((1, n), lambda b, h: (0, 0))
    head = lambda col: pl.BlockSpec((1, R, w), lambda b, h: (b, 0, col + h))
    return pl.pallas_call(
        functools.partial(_diff_attn_kernel, tq=tq, tk=tk, n_ctx=n_ctx, n_rows=R, lam_init=lam_init),
        out_shape=jax.ShapeDtypeStruct((B, R, hb * w), BF16),
        grid=(B, hb),
        in_specs=[
            small(HEAD_DIM), small(HEAD_DIM), small(HEAD_DIM), small(HEAD_DIM), small(w),
            head(q0), head(k0), head(v0),
        ],
        out_specs=head(0),
        scratch_shapes=[pltpu.VMEM((tq, w), F32), pltpu.VMEM((tq, w), F32),
                        pltpu.VMEM((2, 2, tq, tk), F32)],
        compiler_params=_cparams(2),
        name="diff_attn",
    )(vec(lq1), vec(lk1), vec(lq2), vec(lk2), vec(gain), px, px, px)


def _merge_kernel(ya_ref, yb_ref, wa_ref, wb_ref, ga_ref, gb_ref, o_ref):
    pa = _dot(ya_ref[...], wa_ref[0])
    pb = _dot(yb_ref[...], wb_ref[0])
    m = _sigmoid(ga_ref[...].astype(F32)) * pa + _sigmoid(gb_ref[...].astype(F32)) * pb
    o_ref[...] = m.astype(o_ref.dtype)


def _merge(ya, yb, wa, wb, layer, px2d, dims):
    M = ya.shape[0]
    D = wa.shape[2]
    tm, tn = dims["tm"], dims["tn"]
    g0 = dims["seg"]["gate"][0] // tn
    g1 = g0 + D // tn
    return pl.pallas_call(
        _merge_kernel,
        out_shape=jax.ShapeDtypeStruct((M, D), BF16),
        grid=(M // tm, D // tn),
        in_specs=[
            pl.BlockSpec((tm, ya.shape[1]), lambda i, j: (i, 0)),
            pl.BlockSpec((tm, yb.shape[1]), lambda i, j: (i, 0)),
            pl.BlockSpec((1, wa.shape[1], tn), lambda i, j: (layer, 0, j)),
            pl.BlockSpec((1, wb.shape[1], tn), lambda i, j: (layer, 0, j)),
            pl.BlockSpec((tm, tn), lambda i, j: (i, g0 + j)),
            pl.BlockSpec((tm, tn), lambda i, j: (i, g1 + j)),
        ],
        out_specs=pl.BlockSpec((tm, tn), lambda i, j: (i, j)),
        compiler_params=_cparams(2),
        name="merge_branches",
    )(ya, yb, wa, wb, px2d, px2d)


def _row_gate(gate_ref, tile_in_batch, tm, n_ctx):
    row = tile_in_batch * tm + lax.broadcasted_iota(jnp.int32, (tm, 1), 0)
    return jnp.where(row < n_ctx, gate_ref[0, 0:1, :], gate_ref[0, 1:2, :])


def _outproj_kernel(m_ref, w_ref, x_ref, gate_ref, o_ref, *, tm, tiles_per_batch, n_ctx):
    gate = _row_gate(gate_ref, pl.program_id(0) % tiles_per_batch, tm, n_ctx)
    o_ref[...] = x_ref[...] + gate * _dot(m_ref[...], w_ref[0])


def _outproj(m, w, layer, x2d, gate, dims):
    M, D = x2d.shape
    tm, tn = dims["tm"], dims["tn"]
    tpb = dims["rows"] // tm
    return pl.pallas_call(
        functools.partial(_outproj_kernel, tm=tm, tiles_per_batch=tpb, n_ctx=dims["n_ctx"]),
        out_shape=jax.ShapeDtypeStruct((M, D), F32),
        grid=(M // tm, D // tn),
        in_specs=[
            pl.BlockSpec((tm, D), lambda i, j: (i, 0)),
            pl.BlockSpec((1, D, tn), lambda i, j: (layer, 0, j)),
            pl.BlockSpec((tm, tn), lambda i, j: (i, j)),
            pl.BlockSpec((1, 2, tn), lambda i, j: (i // tpb, 0, j)),
        ],
        out_specs=pl.BlockSpec((tm, tn), lambda i, j: (i, j)),
        input_output_aliases={2: 0},
        compiler_params=_cparams(2),
        name="outproj_residual",
    )(m, w, x2d, gate)


def _route(logits, bias, n_experts):
    per_group = n_experts // N_GROUPS
    scores = _sigmoid(logits)
    sel = scores + bias
    srow = [scores[e:e + 1, :] for e in range(n_experts)]
    row = [sel[e:e + 1, :] for e in range(n_experts)]
    best, gidx = None, None
    for g in range(N_GROUPS):
        mem = row[g * per_group:(g + 1) * per_group]
        gs = None
        for a in range(per_group):
            for b in range(a + 1, per_group):
                s = mem[a] + mem[b]
                gs = s if gs is None else jnp.maximum(gs, s)
        if best is None:
            best, gidx = gs, jnp.zeros_like(gs, dtype=jnp.int32)
        else:
            better = gs > best
            gidx = jnp.where(better, g, gidx)
            best = jnp.where(better, gs, best)
    masked = [jnp.where(gidx == (e // per_group), row[e], NEG_INF) for e in range(n_experts)]
    m1, i1 = masked[0], jnp.zeros_like(gidx)
    for e in range(1, n_experts):
        better = masked[e] > m1
        i1 = jnp.where(better, e, i1)
        m1 = jnp.where(better, masked[e], m1)
    m2 = jnp.full_like(m1, -jnp.inf)
    i2 = jnp.zeros_like(gidx)
    for e in range(n_experts):
        cand = jnp.where(i1 == e, -jnp.inf, masked[e])
        better = cand > m2
        i2 = jnp.where(better, e, i2)
        m2 = jnp.where(better, cand, m2)
    w1 = functools.reduce(jnp.add, [jnp.where(i1 == e, srow[e], 0.0) for e in range(n_experts)])
    w2 = functools.reduce(jnp.add, [jnp.where(i2 == e, srow[e], 0.0) for e in range(n_experts)])
    tot = w1 + w2
    return i1, i2, w1 / tot * ROUTED_SCALE, w2 / tot * ROUTED_SCALE


META_ROWS = 8


def _norm_router_kernel(x_ref, gain_ref, shift_ref, scale_ref, wr_ref, bias_ref,
                        h_ref, meta_ref, cnt_ref, carry_ref, *, tr, n_ctx, n_experts):
    first = jnp.logical_and(pl.program_id(0) == 0, pl.program_id(1) == 0)

    @pl.when(first)
    def _():
        carry_ref[...] = jnp.zeros_like(carry_ref)

    x = x_ref[0]
    var = jnp.mean(x * x, axis=-1, keepdims=True)
    y = x * lax.rsqrt(var + NORM_EPS) * gain_ref[...]
    row = pl.program_id(1) * tr + lax.broadcasted_iota(jnp.int32, (tr, 1), 0)
    is_ctx = row < n_ctx
    shift = jnp.where(is_ctx, shift_ref[0, 0:1, :], shift_ref[0, 1:2, :])
    scale = jnp.where(is_ctx, scale_ref[0, 0:1, :], scale_ref[0, 1:2, :])
    h = y * (1.0 + scale) + shift
    h_ref[0] = h

    logits = _dot_nt(wr_ref[...], h.astype(BF16))
    i1, i2, w1, w2 = _route(logits, bias_ref[...], n_experts)
    erow = lax.broadcasted_iota(jnp.int32, (n_experts, tr), 0)
    hit1 = erow == i1
    hit2 = erow == i2
    onehot = jnp.where(jnp.logical_or(hit1, hit2), 1.0, 0.0)
    before = (lax.broadcasted_iota(jnp.int32, (tr, tr), 0)
              < lax.broadcasted_iota(jnp.int32, (tr, tr), 1))
    prefix = _dot(onehot.astype(BF16), jnp.where(before, 1.0, 0.0).astype(BF16))
    seen = prefix + carry_ref[:, 0:1]
    r1 = jnp.sum(jnp.where(hit1, seen, 0.0), axis=0, keepdims=True)
    r2 = jnp.sum(jnp.where(hit2, seen, 0.0), axis=0, keepdims=True)
    carry_ref[...] = carry_ref[...] + jnp.sum(onehot, axis=1, keepdims=True)
    zero = jnp.zeros_like(w1)
    meta_ref[...] = jnp.concatenate(
        [i1.astype(F32), i2.astype(F32), w1, w2, r1, r2, zero, zero], axis=0)
    cnt_ref[...] = carry_ref[...]


def _norm_router(xa, gain, shift, scale, wr_t, bias, n_ctx):
    B, R, D = xa.shape
    E = wr_t.shape[0]
    tr = _pick_tile(R, (256, 128))
    nt = R // tr
    return pl.pallas_call(
        functools.partial(_norm_router_kernel, tr=tr, n_ctx=n_ctx, n_experts=E),
        out_shape=(jax.ShapeDtypeStruct((B, R, D), F32),
                   jax.ShapeDtypeStruct((META_ROWS, B * R), F32),
                   jax.ShapeDtypeStruct((E, HEAD_DIM), F32)),
        grid=(B, nt),
        in_specs=[
            pl.BlockSpec((1, tr, D), lambda b, i: (b, i, 0)),
            pl.BlockSpec((1, D), lambda b, i: (0, 0)),
            pl.BlockSpec((1, 2, D), lambda b, i: (b, 0, 0)),
            pl.BlockSpec((1, 2, D), lambda b, i: (b, 0, 0)),
            pl.BlockSpec((E, D), lambda b, i: (0, 0)),
            pl.BlockSpec((E, 1), lambda b, i: (0, 0)),
        ],
        out_specs=(pl.BlockSpec((1, tr, D), lambda b, i: (b, i, 0)),
                   pl.BlockSpec((META_ROWS, tr), lambda b, i: (0, b * nt + i)),
                   pl.BlockSpec((E, HEAD_DIM), lambda b, i: (0, 0))),
        scratch_shapes=[pltpu.VMEM((E, HEAD_DIM), F32)],
        compiler_params=_cparams(2),
        name="norm_router",
    )(xa, gain.reshape(1, D), shift, scale, wr_t, bias.reshape(E, 1).astype(F32))


SUBLANES = 8


def _row_gather(idx_vmem_ref, idx_smem, isem, src_hbm, buf, sem, slot, n_rows, both_queues=False):
    cp = pltpu.make_async_copy(idx_vmem_ref.at[0, 0], idx_smem.at[pl.ds(slot * n_rows, n_rows)], isem)
    cp.start()
    cp.wait()

    def body(g, carry):
        base = slot * n_rows + g * SUBLANES
        for s in range(SUBLANES):
            t = idx_smem[base + s]
            pltpu.make_async_copy(src_hbm.at[pl.ds(t, 1), :], buf.at[slot, g, pl.ds(s, 1), :],
                                  sem.at[slot]).start(priority=s % 2 if both_queues else 0)
        return carry

    lax.fori_loop(0, n_rows // SUBLANES, body, 0)


def _row_gather_wait(src_hbm, buf, sem, slot, n_rows):
    def body(g, carry):
        pltpu.make_async_copy(src_hbm.at[pl.ds(0, SUBLANES), :], buf.at[slot, g], sem.at[slot]).wait()
        return carry

    lax.fori_loop(0, n_rows // SUBLANES, body, 0)


def _gathered(buf, slot):
    rows = buf[slot]
    return rows.reshape(rows.shape[0] * SUBLANES, rows.shape[2])


def _expert_kernel(te_ref, nv_ref, tok_ref, tok_next_ref, h_hbm, wg_ref, wu_ref, wd_ref,
                   o_ref, buf, idx_smem, sem, isem, *, tm):
    j = pl.program_id(0)
    n_valid = nv_ref[0]
    slot = j % 2

    @pl.when(j == 0)
    def _():
        _row_gather(tok_ref, idx_smem, isem, h_hbm, buf, sem, 0, tm)

    @pl.when(j + 1 < n_valid)
    def _():
        _row_gather(tok_next_ref, idx_smem, isem, h_hbm, buf, sem, 1 - slot, tm)

    @pl.when(j < n_valid)
    def _():
        _row_gather_wait(h_hbm, buf, sem, slot, tm)
        h = _gathered(buf, slot).astype(BF16)
        g = _dot(h, wg_ref[0, 0])
        u = _dot(h, wu_ref[0, 0])
        a = (g * _sigmoid(g) * u).astype(BF16)
        o_ref[...] = _dot(a, wd_ref[0, 0])

    @pl.when(j >= n_valid)
    def _():
        o_ref[...] = jnp.zeros_like(o_ref)


def _experts(h2d, tok3, tile_expert, n_valid, wg, wu, wd, layer, tm):
    M, D = h2d.shape
    F = wg.shape[-1]
    n_tiles = tok3.shape[0]
    P = n_tiles * tm
    grid_spec = pltpu.PrefetchScalarGridSpec(
        num_scalar_prefetch=2,
        grid=(n_tiles,),
        in_specs=[
            pl.BlockSpec((1, 1, tm), lambda j, te, nv: (j, 0, 0)),
            pl.BlockSpec((1, 1, tm), lambda j, te, nv: (jnp.minimum(j + 1, n_tiles - 1), 0, 0)),
            pl.BlockSpec(memory_space=pl.ANY),
            pl.BlockSpec((1, 1, D, F), lambda j, te, nv: (layer, te[j], 0, 0)),
            pl.BlockSpec((1, 1, D, F), lambda j, te, nv: (layer, te[j], 0, 0)),
            pl.BlockSpec((1, 1, F, D), lambda j, te, nv: (layer, te[j], 0, 0)),
        ],
        out_specs=pl.BlockSpec((tm, D), lambda j, te, nv: (j, 0)),
        scratch_shapes=[pltpu.VMEM((2, tm // SUBLANES, SUBLANES, D), F32), pltpu.SMEM((2 * tm,), jnp.int32),
                        pltpu.SemaphoreType.DMA((2,)), pltpu.SemaphoreType.DMA],
    )
    return pl.pallas_call(
        functools.partial(_expert_kernel, tm=tm),
        out_shape=jax.ShapeDtypeStruct((P, D), F32),
        grid_spec=grid_spec,
        compiler_params=_cparams(1),
        name="moe_experts",
    )(tile_expert, n_valid, tok3, tok3, h2d, wg, wu, wd)


def _combine_kernel(p1_ref, p1n_ref, p2_ref, p2n_ref, y_hbm, x_ref, w_ref, gate_ref, *rest,
                    tm, tiles_per_batch, n_ctx, with_norm):
    if with_norm == "next":
        gain_ref, shift_ref, scale_ref, o_ref, h_ref = rest[:5]
    else:
        gain_ref, o_ref = rest[:2]
    buf1, buf2, idx1, idx2, sem1, sem2, isem = rest[-7:]
    j = pl.program_id(0)
    slot = j % 2

    @pl.when(j == 0)
    def _():
        _row_gather(p1_ref, idx1, isem, y_hbm, buf1, sem1, 0, tm, both_queues=True)
        _row_gather(p2_ref, idx2, isem, y_hbm, buf2, sem2, 0, tm, both_queues=True)

    @pl.when(j + 1 < pl.num_programs(0))
    def _():
        _row_gather(p1n_ref, idx1, isem, y_hbm, buf1, sem1, 1 - slot, tm, both_queues=True)
        _row_gather(p2n_ref, idx2, isem, y_hbm, buf2, sem2, 1 - slot, tm, both_queues=True)

    _row_gather_wait(y_hbm, buf1, sem1, slot, tm)
    _row_gather_wait(y_hbm, buf2, sem2, slot, tm)
    step_rows = 2 * SUBLANES
    ctx_tile = (j % tiles_per_batch) * tm < n_ctx
    pick = lambda ref: jnp.where(ctx_tile, ref[0, 0:1, :], ref[0, 1:2, :])
    gate = pick(gate_ref)
    if with_norm == "next":
        gain_scale = gain_ref[...] * (1.0 + pick(scale_ref))
        shift = pick(shift_ref)

    def rows_step(i, carry):
        rows = pl.ds(pl.multiple_of(i * step_rows, step_rows), step_rows)
        b1 = jnp.concatenate([buf1[slot, 2 * i], buf1[slot, 2 * i + 1]], axis=0)
        b2 = jnp.concatenate([buf2[slot, 2 * i], buf2[slot, 2 * i + 1]], axis=0)
        w = w_ref[rows, :]
        x = x_ref[rows, :] + gate * (w[:, 0:1] * b1 + w[:, 1:2] * b2)
        normed = x * lax.rsqrt(jnp.mean(x * x, axis=-1, keepdims=True) + NORM_EPS)
        if with_norm == "next":
            o_ref[rows, :] = x
            h_ref[rows, :] = (normed * gain_scale + shift).astype(h_ref.dtype)
        else:
            o_ref[0, rows, :] = normed * gain_ref[...]
        return carry

    lax.fori_loop(0, tm // step_rows, rows_step, 0, unroll=4)


def _combine(x2d, y_sorted, pos1, pos2, w12, gate, dims, next_norm=None, final_gain=None):
    M, D = x2d.shape
    tm = dims["tm_comb"]
    n_ctx = dims["n_ctx"]
    assert n_ctx % tm == 0
    n_tiles = M // tm
    tpb = dims["rows"] // tm
    p1 = pos1.reshape(n_tiles, 1, tm)
    p2 = pos2.reshape(n_tiles, 1, tm)
    cur = pl.BlockSpec((1, 1, tm), lambda j: (j, 0, 0))
    nxt = pl.BlockSpec((1, 1, tm), lambda j: (jnp.minimum(j + 1, n_tiles - 1), 0, 0))
    rows = pl.BlockSpec((tm, D), lambda j: (j, 0))
    per_batch = pl.BlockSpec((1, 2, D), lambda j: (j // tpb, 0, 0))
    one_row = pl.BlockSpec((1, D), lambda j: (0, 0))
    if next_norm is not None:
        with_norm = "next"
        gain, shift, scale = next_norm
        extra_in = [one_row, per_batch, per_batch]
        extra_args = [gain.reshape(1, D), shift, scale]
        out_shape = (jax.ShapeDtypeStruct((M, D), F32), jax.ShapeDtypeStruct((M, D), BF16))
        out_specs = (rows, rows)
        aliases = {5: 0}
    else:
        with_norm = "final"
        extra_in = [one_row]
        extra_args = [final_gain.reshape(1, D)]
        ctx_tiles = n_ctx // tm
        out_shape = jax.ShapeDtypeStruct((M // dims["rows"], dims["rows"] - n_ctx, D), F32)
        out_specs = pl.BlockSpec(
            (1, tm, D), lambda j: (j // tpb, jnp.maximum(j % tpb - ctx_tiles, 0), 0))
        aliases = {}
    return pl.pallas_call(
        functools.partial(_combine_kernel, tm=tm, tiles_per_batch=tpb, n_ctx=dims["n_ctx"],
                          with_norm=with_norm),
        out_shape=out_shape,
        grid=(n_tiles,),
        in_specs=[cur, nxt, cur, nxt,
                  pl.BlockSpec(memory_space=pl.ANY),
                  rows,
                  pl.BlockSpec((tm, 2), lambda j: (j, 0)),
                  per_batch] + extra_in,
        out_specs=out_specs,
        scratch_shapes=[pltpu.VMEM((2, tm // SUBLANES, SUBLANES, D), F32),
                        pltpu.VMEM((2, tm // SUBLANES, SUBLANES, D), F32),
                        pltpu.SMEM((2 * tm,), jnp.int32), pltpu.SMEM((2 * tm,), jnp.int32),
                        pltpu.SemaphoreType.DMA((2,)), pltpu.SemaphoreType.DMA((2,)),
                        pltpu.SemaphoreType.DMA],
        input_output_aliases=aliases,
        compiler_params=_cparams(1),
        name="moe_combine",
    )(p1, p1, p2, p2, y_sorted, x2d, w12, gate, *extra_args)


def _dispatch_plan(meta, counts, tm, n_tiles):
    M = meta.shape[1]
    i1 = meta[0].astype(jnp.int32)
    i2 = meta[1].astype(jnp.int32)
    r1 = meta[4].astype(jnp.int32)
    r2 = meta[5].astype(jnp.int32)
    cnt = counts[:, 0].astype(jnp.int32)
    padded = ((cnt + tm - 1) // tm) * tm
    seg_end = jnp.cumsum(padded)
    seg_start = seg_end - padded
    pos1 = seg_start[i1] + r1
    pos2 = seg_start[i2] + r2
    n_valid = seg_end[-1] // tm
    tile_start = jnp.arange(n_tiles, dtype=jnp.int32) * tm
    probe = jnp.minimum(tile_start, seg_end[-1] - tm)
    tile_expert = jnp.sum(probe[:, None] >= seg_end[None, :], axis=1).astype(jnp.int32)
    tok = jnp.arange(M, dtype=jnp.int32)
    tok_sorted = jnp.zeros((n_tiles * tm,), jnp.int32).at[jnp.concatenate([pos1, pos2])].set(
        jnp.concatenate([tok, tok]), unique_indices=True)
    w12 = jnp.stack([meta[2], meta[3]], axis=1)
    return (pos1, pos2, w12, tok_sorted.reshape(n_tiles, 1, tm), tile_expert,
            n_valid.reshape(1).astype(jnp.int32))


def _rope_tables(seq, n_ctx):
    rows = seq // GRID_W
    row = jnp.repeat(jnp.arange(rows), GRID_W).astype(F32)
    col = jnp.tile(jnp.arange(GRID_W), rows).astype(F32)
    inv = ROPE_THETA ** (-(jnp.arange(0, ROPE_AXIS_DIM, 2, dtype=F32) / ROPE_AXIS_DIM))
    ang_r = row[:, None] * inv
    ang_c = col[:, None] * inv
    ang = jnp.concatenate([ang_r, ang_r, ang_c, ang_c], axis=-1)
    lane = jnp.arange(HEAD_DIM)
    sign = jnp.where((lane % ROPE_AXIS_DIM) < (ROPE_AXIS_DIM // 2), -1.0, 1.0).astype(F32)
    cos = jnp.concatenate([jnp.ones((n_ctx, HEAD_DIM), F32), jnp.cos(ang)], axis=0)
    sin = jnp.concatenate([jnp.zeros((n_ctx, HEAD_DIM), F32), jnp.sin(ang) * sign], axis=0)
    return cos, sin


def _dims(D, S, C, d_expert):
    ha = D // 256
    hkv = ha // 4
    hb = D // 512
    widths = [("ka", hkv * HEAD_DIM), ("va", hkv * HEAD_DIM), ("kb", hb * 2 * HEAD_DIM),
              ("vb", hb * 2 * HEAD_DIM), ("qa", ha * HEAD_DIM), ("qb", hb * 2 * HEAD_DIM),
              ("gate", 2 * D)]
    seg, off = {}, 0
    for name, wdt in widths:
        seg[name] = (off, off + wdt)
        off += wdt
    R = C + S
    tm = _pick_tile(R, (768, 512, 384, 256, 128))
    tn = _pick_tile(D, (512, 256, 128))
    tn_in = _pick_tile(off, (1024, 512, 256, 128))
    tm_in = _pick_tile(R, (1056, 768, 512, 384, 256, 128))
    tm_comb = _pick_tile(R, (256, 128))
    return dict(ha=ha, hkv=hkv, group=ha // hkv, hb=hb, seg=seg, n_cols=off, rows=R, n_ctx=C,
                tm=tm, tn=tn, tn_in=tn_in, tm_in=tm_in, tm_exp=256, tm_comb=tm_comb)


def kernel(x, c, ctx, c_ctx, ada_down, ada_up, ada_bias, norm_mix, norm_ffn, w_in, sink_logit,
           lam_q1, lam_k1, lam_q2, lam_k2, subln_gain, w_branch_a, w_branch_b, w_out,
           w_router, b_router, w_exp_gate, w_exp_up, w_exp_down, norm_final):
    B, S, D = x.shape
    C = ctx.shape[1]
    depth = w_in.shape[0]
    dims = _dims(D, S, C, w_exp_gate.shape[-1])
    R = dims["rows"]
    M = B * R

    cond = jnp.concatenate([c, c_ctx[None], jnp.zeros((8 - (B + 1) % 8 if (B + 1) % 8 else 0, D), F32)])
    mods = _adaln(cond, ada_down, ada_up, ada_bias)
    mods = mods.reshape(depth, mods.shape[1], N_MOD, D)
    lat = mods[:, :B]
    cx = jnp.broadcast_to(mods[:, B:B + 1], lat.shape)
    mod = jnp.stack([cx, lat], axis=2)

    cos_tab, sin_tab = _rope_tables(S, C)
    wr_t = w_router.T.astype(BF16)
    xa = jnp.concatenate([ctx, x], axis=1)
    w_in_b, wa_b, wb_b, wo_b = (w.astype(BF16) for w in (w_in, w_branch_a, w_branch_b, w_out))
    wg_b, wu_b, wd_b = (w.astype(BF16) for w in (w_exp_gate, w_exp_up, w_exp_down))
    n_experts = w_router.shape[1]
    tm_exp = dims["tm_exp"]
    n_exp_tiles = -(-(2 * M + n_experts * (tm_exp - 1)) // tm_exp)

    mods_of = lambda l: [mod[l, :, :, k] for k in range(N_MOD)]
    h_mix = _norm_mod(xa, norm_mix[0], mods_of(0)[0], mods_of(0)[1], C).reshape(M, D)
    for l in range(depth):
        lam_init = 0.8 - 0.6 * math.exp(-0.3 * l)
        m = mods_of(l)
        px = _inproj(h_mix, w_in_b, l, cos_tab, sin_tab, dims)
        px3 = px.reshape(B, R, -1)
        ya = _win_attn(px3, sink_logit[l], dims)
        yb = _diff_attn(px3, lam_q1[l], lam_k1[l], lam_q2[l], lam_k2[l], subln_gain[l], lam_init, dims)
        mg = _merge(ya.reshape(M, -1), yb.reshape(M, -1), wa_b, wb_b, l, px, dims)
        x2d = _outproj(mg, wo_b, l, xa.reshape(M, D), m[2], dims)
        xa = x2d.reshape(B, R, D)
        h, meta, counts = _norm_router(xa, norm_ffn[l], m[3], m[4], wr_t, b_router, C)
        pos1, pos2, w12, tok3, tile_expert, n_valid = _dispatch_plan(meta, counts, tm_exp, n_exp_tiles)
        y_sorted = _experts(h.reshape(M, D), tok3, tile_expert, n_valid, wg_b, wu_b, wd_b, l, tm_exp)
        if l + 1 < depth:
            nm = mods_of(l + 1)
            x2d, h_mix = _combine(x2d, y_sorted, pos1, pos2, w12, m[5], dims,
                                  next_norm=(norm_mix[l + 1], nm[0], nm[1]))
            xa = x2d.reshape(B, R, D)
        else:
            return _combine(x2d, y_sorted, pos1, pos2, w12, m[5], dims, final_gain=norm_final)
```

```python
import functools
import math

import jax
import jax.numpy as jnp
from jax import lax
from jax.experimental import pallas as pl
from jax.experimental.pallas import tpu as pltpu

HEAD_DIM = 128
GRID_W = 64
ROPE_AXIS_DIM = HEAD_DIM // 2
ROPE_THETA = 10000.0
BLOCK = 128
N_GROUPS = 4
ROUTED_SCALE = 1.0
NORM_EPS = 1e-6
NEG_INF = -1e30
N_MOD = 6
Q_SCALE = HEAD_DIM ** -0.5
LOG2E = math.log2(math.e)
MIN_SOFTMAX_DENOM = 2.0 ** -80

F32 = jnp.float32
BF16 = jnp.bfloat16

VMEM_LIMIT_BYTES = 56 * 1024 * 1024


def _cparams(n_axes):
    return pltpu.CompilerParams(dimension_semantics=("arbitrary",) * n_axes,
                                vmem_limit_bytes=VMEM_LIMIT_BYTES)


def _pick_tile(total, candidates):
    for t in candidates:
        if total % t == 0:
            return t
    raise ValueError(f"no tile in {candidates} divides {total}")


def _dot(a, b):
    return jnp.dot(a, b, preferred_element_type=F32)


def _dot_nt(a, b):
    return lax.dot_general(a, b, (((1,), (1,)), ((), ())), preferred_element_type=F32)


def _sigmoid(x):
    return 1.0 / (1.0 + jnp.exp(-x))


def _adaln_kernel(cond_ref, down_ref, up_ref, bias_ref, o_ref):
    cond = cond_ref[...]
    s = (cond * _sigmoid(cond)).astype(BF16)
    t = _dot(s, down_ref[0].astype(BF16)).astype(BF16)
    o_ref[0] = _dot(t, up_ref[0].astype(BF16)) + bias_ref[0]


def _adaln(cond, down, up, bias):
    L, D, rank = down.shape
    n_out = up.shape[-1]
    rows = cond.shape[0]
    tn = _pick_tile(n_out, (2048, 1024, 512, 256, 128))
    return pl.pallas_call(
        _adaln_kernel,
        out_shape=jax.ShapeDtypeStruct((L, rows, n_out), F32),
        grid=(L, n_out // tn),
        in_specs=[
            pl.BlockSpec((rows, D), lambda l, j: (0, 0)),
            pl.BlockSpec((1, D, rank), lambda l, j: (l, 0, 0)),
            pl.BlockSpec((1, rank, tn), lambda l, j: (l, 0, j)),
            pl.BlockSpec((1, 1, tn), lambda l, j: (l, 0, j)),
        ],
        out_specs=pl.BlockSpec((1, rows, tn), lambda l, j: (l, 0, j)),
        compiler_params=_cparams(2),
        name="adaln",
    )(cond, down, up, bias.reshape(L, 1, n_out))


def _norm_mod_kernel(x_ref, gain_ref, shift_ref, scale_ref, o_ref, *, tr, n_ctx):
    x = x_ref[0]
    var = jnp.mean(x * x, axis=-1, keepdims=True)
    y = x * lax.rsqrt(var + NORM_EPS) * gain_ref[...]
    row = pl.program_id(1) * tr + lax.broadcasted_iota(jnp.int32, (tr, 1), 0)
    is_ctx = row < n_ctx
    shift = jnp.where(is_ctx, shift_ref[0, 0:1, :], shift_ref[0, 1:2, :])
    scale = jnp.where(is_ctx, scale_ref[0, 0:1, :], scale_ref[0, 1:2, :])
    o_ref[0] = (y * (1.0 + scale) + shift).astype(o_ref.dtype)


def _norm_mod(xa, gain, shift, scale, n_ctx):
    B, R, D = xa.shape
    tr = _pick_tile(R, (256, 128))
    return pl.pallas_call(
        functools.partial(_norm_mod_kernel, tr=tr, n_ctx=n_ctx),
        out_shape=jax.ShapeDtypeStruct((B, R, D), BF16),
        grid=(B, R // tr),
        in_specs=[
            pl.BlockSpec((1, tr, D), lambda b, i: (b, i, 0)),
            pl.BlockSpec((1, D), lambda b, i: (0, 0)),
            pl.BlockSpec((1, 2, D), lambda b, i: (b, 0, 0)),
            pl.BlockSpec((1, 2, D), lambda b, i: (b, 0, 0)),
        ],
        out_specs=pl.BlockSpec((1, tr, D), lambda b, i: (b, i, 0)),
        compiler_params=_cparams(2),
        name="norm_mod",
    )(xa, gain.reshape(1, D), shift, scale)


def _rope_cols(a, cos, sin_signed, lane_lo):
    rot = jnp.where(lane_lo, pltpu.roll(a, HEAD_DIM - 32, 1), pltpu.roll(a, 32, 1))
    return a * cos + rot * sin_signed


def _in_ranges(x, ranges):
    return functools.reduce(jnp.logical_or, [(x >= lo) & (x < hi) for lo, hi in ranges])


def _inproj_kernel(h_ref, w_ref, cos_ref, sin_ref, o_ref, *, tn, sub, rope_ranges, qa_range, qb_range):
    j = pl.program_id(1)
    chunks = tn // HEAD_DIM
    first = j * chunks
    any_roped = functools.reduce(
        jnp.logical_or, [_in_ranges(first + c, rope_ranges) for c in range(chunks)])

    @pl.when(any_roped)
    def _():
        h = h_ref[...]
        cos = cos_ref[...]
        sin_signed = sin_ref[...]
        lane = lax.broadcasted_iota(jnp.int32, (1, HEAD_DIM), 1)
        lane_lo = (lane % ROPE_AXIS_DIM) < (ROPE_AXIS_DIM // 2)
        for c0 in range(0, tn, sub):
            acc = _dot(h, w_ref[0, :, c0:c0 + sub])
            for c in range(sub // HEAD_DIM):
                g = first + (c0 // HEAD_DIM + c)
                roped = _in_ranges(g, rope_ranges)
                mult = jnp.where(_in_ranges(g, (qa_range, qb_range)), Q_SCALE * LOG2E, 1.0).astype(F32)
                cos_g = jnp.where(roped, cos, 1.0) * mult
                sin_g = jnp.where(roped, sin_signed, 0.0) * mult
                a = acc[:, c * HEAD_DIM:(c + 1) * HEAD_DIM]
                col = c0 + c * HEAD_DIM
                o_ref[:, col:col + HEAD_DIM] = _rope_cols(a, cos_g, sin_g, lane_lo).astype(o_ref.dtype)

    @pl.when(jnp.logical_not(any_roped))
    def _():
        h = h_ref[...]
        for c0 in range(0, tn, sub):
            o_ref[:, c0:c0 + sub] = _dot(h, w_ref[0, :, c0:c0 + sub]).astype(o_ref.dtype)


def _inproj(h2d, w_all, layer, cos_tab, sin_tab, dims):
    M, D = h2d.shape
    N = w_all.shape[2]
    R = cos_tab.shape[0]
    tm = dims["tm_in"]
    tn = dims["tn_in"]
    tiles_per_batch = R // tm
    seg = dims["seg"]
    rng = lambda k: (seg[k][0] // HEAD_DIM, seg[k][1] // HEAD_DIM)
    rope_ranges = tuple(rng(k) for k in ("ka", "kb", "qa", "qb"))
    return pl.pallas_call(
        functools.partial(_inproj_kernel, tn=tn, sub=min(tn, 2 * HEAD_DIM), rope_ranges=rope_ranges,
                          qa_range=rng("qa"), qb_range=rng("qb")),
        out_shape=jax.ShapeDtypeStruct((M, N), BF16),
        grid=(M // tm, N // tn),
        in_specs=[
            pl.BlockSpec((tm, D), lambda i, j: (i, 0)),
            pl.BlockSpec((1, D, tn), lambda i, j: (layer, 0, j)),
            pl.BlockSpec((tm, HEAD_DIM), lambda i, j: (i % tiles_per_batch, 0)),
            pl.BlockSpec((tm, HEAD_DIM), lambda i, j: (i % tiles_per_batch, 0)),
        ],
        out_specs=pl.BlockSpec((tm, tn), lambda i, j: (i, j)),
        compiler_params=_cparams(2),
        name="inproj_rope",
    )(h2d, w_all, cos_tab, sin_tab)


def _win_attn_kernel(sink_ref, *refs, hkv, group, n_ctx_blocks, n_blocks):
    q_refs = refs[:hkv]
    kc_ref, vc_ref, kp_ref, ko_ref, kn_ref, vp_ref, vo_ref, vn_ref, o_ref = refs[hkv:]
    n = pl.program_id(1)
    rows = group * BLOCK
    qi = lax.broadcasted_iota(jnp.int32, (rows, BLOCK), 0) % BLOCK
    kj = lax.broadcasted_iota(jnp.int32, (rows, BLOCK), 1)
    is_lat = n >= n_ctx_blocks
    valid_p = jnp.logical_and(is_lat, n - 1 >= n_ctx_blocks)
    valid_n = jnp.logical_and(is_lat, n + 1 < n_blocks)
    mask = jnp.concatenate([
        jnp.logical_and(kj >= qi, valid_p),
        jnp.logical_and(kj >= 0, is_lat),
        jnp.logical_and(kj <= qi, valid_n)], axis=1)
    for h in range(hkv):
        cs = slice(h * HEAD_DIM, (h + 1) * HEAD_DIM)
        q = jnp.concatenate(
            [q_refs[h][0, :, g * HEAD_DIM:(g + 1) * HEAD_DIM] for g in range(group)], axis=0)
        sink = jnp.concatenate(
            [jnp.full((BLOCK, 1), sink_ref[h * group + g] * LOG2E, F32) for g in range(group)], axis=0)
        k_band = jnp.concatenate([kp_ref[0, :, cs], ko_ref[0, :, cs], kn_ref[0, :, cs]], axis=0)
        v_band = jnp.concatenate([vp_ref[0, :, cs], vo_ref[0, :, cs], vn_ref[0, :, cs]], axis=0)
        s_c = _dot_nt(q, kc_ref[0, :, cs])
        s_b = jnp.where(mask, _dot_nt(q, k_band), NEG_INF)
        m = jnp.maximum(jnp.maximum(jnp.max(s_c, axis=-1, keepdims=True),
                                    jnp.max(s_b, axis=-1, keepdims=True)), sink)
        e_c = jnp.exp2(s_c - m)
        e_b = jnp.exp2(s_b - m)
        denom = (jnp.sum(e_c, axis=-1, keepdims=True) + jnp.sum(e_b, axis=-1, keepdims=True)
                 + jnp.exp2(sink - m))
        out = (_dot(e_b.astype(BF16), v_band) + _dot(e_c.astype(BF16), vc_ref[0, :, cs])) / denom
        for g in range(group):
            col = (h * group + g) * HEAD_DIM
            o_ref[0, :, col:col + HEAD_DIM] = out[g * BLOCK:(g + 1) * BLOCK].astype(o_ref.dtype)


def _win_attn(px, sink, dims):
    B, R, _ = px.shape
    seg = dims["seg"]
    hkv, group, n_ctx = dims["hkv"], dims["group"], dims["n_ctx"]
    nb = R // BLOCK
    ncb = n_ctx // BLOCK
    qw = group * HEAD_DIM
    kw = hkv * HEAD_DIM
    k0 = seg["ka"][0] // kw
    v0 = seg["va"][0] // kw
    q0 = seg["qa"][0] // qw
    assert seg["qa"][0] % qw == 0 and seg["ka"][0] % kw == 0 and seg["va"][0] % kw == 0

    def q_spec(h):
        return pl.BlockSpec((1, BLOCK, qw), lambda b, n: (b, n, q0 + h))

    def band_spec(col, shift):
        return pl.BlockSpec((1, BLOCK, kw), lambda b, n: (b, jnp.clip(n + shift, 0, nb - 1), col))

    return pl.pallas_call(
        functools.partial(_win_attn_kernel, hkv=hkv, group=group, n_ctx_blocks=ncb, n_blocks=nb),
        out_shape=jax.ShapeDtypeStruct((B, R, hkv * qw), BF16),
        grid=(B, nb),
        in_specs=[pl.BlockSpec(memory_space=pltpu.SMEM)]
        + [q_spec(h) for h in range(hkv)]
        + [pl.BlockSpec((1, n_ctx, kw), lambda b, n: (b, 0, k0)),
           pl.BlockSpec((1, n_ctx, kw), lambda b, n: (b, 0, v0)),
           band_spec(k0, -1), band_spec(k0, 0), band_spec(k0, 1),
           band_spec(v0, -1), band_spec(v0, 0), band_spec(v0, 1)],
        out_specs=pl.BlockSpec((1, BLOCK, hkv * qw), lambda b, n: (b, n, 0)),
        compiler_params=_cparams(2),
        name="window_attn",
    )(sink, *([px] * (hkv + 8)))


def _diff_attn_kernel(lq1_ref, lk1_ref, lq2_ref, lk2_ref, gain_ref, q_ref, k_ref, v_ref, o_ref,
                      acc1_ref, acc2_ref, s_ref, *, tq, tk, n_ctx, n_rows, lam_init):
    lam = (jnp.exp(jnp.sum(lq1_ref[...] * lk1_ref[...], axis=-1, keepdims=True))
           - jnp.exp(jnp.sum(lq2_ref[...] * lk2_ref[...], axis=-1, keepdims=True)) + lam_init)
    gain = gain_ref[...] * (1.0 - lam_init)

    def finish(q_start, nq, l1, l2):
        o = acc1_ref[pl.ds(0, nq), :] / l1 - lam * (acc2_ref[pl.ds(0, nq), :] / l2)
        var = jnp.mean(o * o, axis=-1, keepdims=True)
        y = o * lax.rsqrt(var + NORM_EPS) * gain
        o_ref[0, pl.ds(q_start, nq), :] = y.astype(o_ref.dtype)

    def attend_online(q_start, nq, kv_len, chunk):
        q1 = q_ref[0, pl.ds(q_start, nq), :HEAD_DIM]
        q2 = q_ref[0, pl.ds(q_start, nq), HEAD_DIM:]
        accs = (acc1_ref.at[pl.ds(0, nq), :], acc2_ref.at[pl.ds(0, nq), :])

        def step(t, carry):
            start = pl.multiple_of(t * chunk, chunk)
            k = k_ref[0, pl.ds(start, chunk), :]
            v = v_ref[0, pl.ds(start, chunk), :]
            out = []
            for which, (q, acc) in enumerate(zip((q1, q2), accs)):
                m, l = carry[2 * which], carry[2 * which + 1]
                s = _dot_nt(q, k[:, which * HEAD_DIM:(which + 1) * HEAD_DIM])
                n = jnp.maximum(m, jnp.max(s, axis=-1, keepdims=True))
                alpha = jnp.exp2(m - n)
                p = jnp.exp2(s - n)
                acc[...] = alpha * acc[...] + _dot(p.astype(BF16), v)
                out += [n, alpha * l + jnp.sum(p, axis=-1, keepdims=True)]
            return tuple(out)

        for acc in accs:
            acc[...] = jnp.zeros((nq, 2 * HEAD_DIM), F32)
        carry = (jnp.full((nq, 1), -jnp.inf, F32), jnp.zeros((nq, 1), F32)) * 2
        if kv_len == chunk:
            carry = step(0, carry)
        else:
            carry = lax.fori_loop(0, kv_len // chunk, step, carry)
        finish(q_start, nq, carry[1], carry[3])

    def attend_bounded(q_start, nq, kv_len, chunk, key_norm):
        qs = tuple(q_ref[0, pl.ds(q_start, nq), w * HEAD_DIM:(w + 1) * HEAD_DIM] for w in range(2))
        accs = (acc1_ref.at[pl.ds(0, nq), :], acc2_ref.at[pl.ds(0, nq), :])
        bound = []
        for q, kn in zip(qs, key_norm):
            qf = q.astype(F32)
            bound.append(jnp.sqrt(jnp.sum(qf * qf, axis=-1, keepdims=True)) * kn)

        def scores(t, slot):
            k = k_ref[0, pl.ds(pl.multiple_of(t * chunk, chunk), chunk), :]
            for which in range(2):
                s_ref[slot, which, pl.ds(0, nq), pl.ds(0, chunk)] = _dot_nt(
                    qs[which], k[:, which * HEAD_DIM:(which + 1) * HEAD_DIM])

        def softmax_pv(t, slot, ls):
            v = v_ref[0, pl.ds(pl.multiple_of(t * chunk, chunk), chunk), :]
            out = []
            for which in range(2):
                p = jnp.exp2(s_ref[slot, which, pl.ds(0, nq), pl.ds(0, chunk)] - bound[which])
                accs[which][...] += _dot(p.astype(BF16), v)
                out.append(ls[which] + jnp.sum(p, axis=-1, keepdims=True))
            return tuple(out)

        for acc in accs:
            acc[...] = jnp.zeros((nq, 2 * HEAD_DIM), F32)
        ls = (jnp.zeros((nq, 1), F32),) * 2
        n_chunks = kv_len // chunk

        def pair(u, c):
            scores(2 * u + 1, 1)
            c = softmax_pv(2 * u, 0, c)
            scores(2 * u + 2, 0)
            return softmax_pv(2 * u + 1, 1, c)

        scores(0, 0)
        n_pairs = (n_chunks - 1) // 2
        if n_pairs > 0:
            ls = lax.fori_loop(0, n_pairs, pair, ls)
        if n_chunks % 2 == 0:
            scores(n_chunks - 1, 1)
            ls = softmax_pv(n_chunks - 2, 0, ls)
            ls = softmax_pv(n_chunks - 1, 1, ls)
        else:
            ls = softmax_pv(n_chunks - 1, 0, ls)
        return ls

    attend_online(0, n_ctx, n_ctx, n_ctx)

    def max_norm(t, mx):
        k = k_ref[0, pl.ds(pl.multiple_of(t * tk, tk), tk), :].astype(F32)
        sq = k * k
        return tuple(
            jnp.maximum(mx[w], jnp.max(jnp.sum(sq[:, w * HEAD_DIM:(w + 1) * HEAD_DIM], axis=-1, keepdims=True),
                                       axis=0, keepdims=True))
            for w in range(2))

    key_sq = lax.fori_loop(0, n_rows // tk, max_norm, (jnp.zeros((1, 1), F32),) * 2)
    key_norm = tuple(jnp.sqrt(x) for x in key_sq)
    align = math.gcd(n_ctx, tq)

    def latent_tile(i, _):
        q_start = pl.multiple_of(n_ctx + i * tq, align)
        l1, l2 = attend_bounded(q_start, tq, n_rows, tk, key_norm)
        smallest = jnp.min(jnp.minimum(l1, l2), axis=0, keepdims=True)[0, 0]
        safe = smallest >= MIN_SOFTMAX_DENOM

        @pl.when(safe)
        def _():
            finish(q_start, tq, l1, l2)

        @pl.when(jnp.logical_not(safe))
        def _():
            attend_online(q_start, tq, n_rows, tk)

        return 0

    lax.fori_loop(0, (n_rows - n_ctx) // tq, latent_tile, 0)


def _diff_attn(px, lq1, lk1, lq2, lk2, gain, lam_init, dims):
    B, R, _ = px.shape
    seg = dims["seg"]
    hb, n_ctx = dims["hb"], dims["n_ctx"]
    w = 2 * HEAD_DIM
    tq = _pick_tile(R - n_ctx, (512, 256, 128))
    tk = _pick_tile(R, (768, 512, 384, 256, 128))
    assert n_ctx <= tq and n_ctx % 16 == 0
    q0, k0, v0 = seg["qb"][0] // w, seg["kb"][0] // w, seg["vb"][0] // w
    vec = lambda a: a.reshape(1, -1)
    small = lambda n: pl.BlockSpec((1, n), lambda b, h: (0, 0))
    head = lambda col: pl.BlockSpec((1, R, w), lambda b, h: (b, 0, col + h))
    return pl.pallas_call(
        functools.partial(_diff_attn_kernel, tq=tq, tk=tk, n_ctx=n_ctx, n_rows=R, lam_init=lam_init),
        out_shape=jax.ShapeDtypeStruct((B, R, hb * w), BF16),
        grid=(B, hb),
        in_specs=[
            small(HEAD_DIM), small(HEAD_DIM), small(HEAD_DIM), small(HEAD_DIM), small(w),
            head(q0), head(k0), head(v0),
        ],
        out_specs=head(0),
        scratch_shapes=[pltpu.VMEM((tq, w), F32), pltpu.VMEM((tq, w), F32),
                        pltpu.VMEM((2, 2, tq, tk), F32)],
        compiler_params=_cparams(2),
        name="diff_attn",
    )(vec(lq1), vec(lk1), vec(lq2), vec(lk2), vec(gain), px, px, px)


def _merge_kernel(ya_ref, yb_ref, wa_ref, wb_ref, ga_ref, gb_ref, o_ref):
    pa = _dot(ya_ref[...], wa_ref[0])
    pb = _dot(yb_ref[...], wb_ref[0])
    m = _sigmoid(ga_ref[...].astype(F32)) * pa + _sigmoid(gb_ref[...].astype(F32)) * pb
    o_ref[...] = m.astype(o_ref.dtype)


def _merge(ya, yb, wa, wb, layer, px2d, dims):
    M = ya.shape[0]
    D = wa.shape[2]
    tm, tn = dims["tm"], dims["tn"]
    g0 = dims["seg"]["gate"][0] // tn
    g1 = g0 + D // tn
    return pl.pallas_call(
        _merge_kernel,
        out_shape=jax.ShapeDtypeStruct((M, D), BF16),
        grid=(M // tm, D // tn),
        in_specs=[
            pl.BlockSpec((tm, ya.shape[1]), lambda i, j: (i, 0)),
            pl.BlockSpec((tm, yb.shape[1]), lambda i, j: (i, 0)),
            pl.BlockSpec((1, wa.shape[1], tn), lambda i, j: (layer, 0, j)),
            pl.BlockSpec((1, wb.shape[1], tn), lambda i, j: (layer, 0, j)),
            pl.BlockSpec((tm, tn), lambda i, j: (i, g0 + j)),
            pl.BlockSpec((tm, tn), lambda i, j: (i, g1 + j)),
        ],
        out_specs=pl.BlockSpec((tm, tn), lambda i, j: (i, j)),
        compiler_params=_cparams(2),
        name="merge_branches",
    )(ya, yb, wa, wb, px2d, px2d)


def _row_gate(gate_ref, tile_in_batch, tm, n_ctx):
    row = tile_in_batch * tm + lax.broadcasted_iota(jnp.int32, (tm, 1), 0)
    return jnp.where(row < n_ctx, gate_ref[0, 0:1, :], gate_ref[0, 1:2, :])


def _outproj_kernel(m_ref, w_ref, x_ref, gate_ref, o_ref, *, tm, tiles_per_batch, n_ctx):
    gate = _row_gate(gate_ref, pl.program_id(0) % tiles_per_batch, tm, n_ctx)
    o_ref[...] = x_ref[...] + gate * _dot(m_ref[...], w_ref[0])


def _outproj(m, w, layer, x2d, gate, dims):
    M, D = x2d.shape
    tm, tn = dims["tm"], dims["tn"]
    tpb = dims["rows"] // tm
    return pl.pallas_call(
        functools.partial(_outproj_kernel, tm=tm, tiles_per_batch=tpb, n_ctx=dims["n_ctx"]),
        out_shape=jax.ShapeDtypeStruct((M, D), F32),
        grid=(M // tm, D // tn),
        in_specs=[
            pl.BlockSpec((tm, D), lambda i, j: (i, 0)),
            pl.BlockSpec((1, D, tn), lambda i, j: (layer, 0, j)),
            pl.BlockSpec((tm, tn), lambda i, j: (i, j)),
            pl.BlockSpec((1, 2, tn), lambda i, j: (i // tpb, 0, j)),
        ],
        out_specs=pl.BlockSpec((tm, tn), lambda i, j: (i, j)),
        input_output_aliases={2: 0},
        compiler_params=_cparams(2),
        name="outproj_residual",
    )(m, w, x2d, gate)


def _route(logits, bias, n_experts):
    per_group = n_experts // N_GROUPS
    scores = _sigmoid(logits)
    sel = scores + bias
    srow = [scores[e:e + 1, :] for e in range(n_experts)]
    row = [sel[e:e + 1, :] for e in range(n_experts)]
    best, gidx = None, None
    for g in range(N_GROUPS):
        mem = row[g * per_group:(g + 1) * per_group]
        gs = None
        for a in range(per_group):
            for b in range(a + 1, per_group):
                s = mem[a] + mem[b]
                gs = s if gs is None else jnp.maximum(gs, s)
        if best is None:
            best, gidx = gs, jnp.zeros_like(gs, dtype=jnp.int32)
        else:
            better = gs > best
            gidx = jnp.where(better, g, gidx)
            best = jnp.where(better, gs, best)
    masked = [jnp.where(gidx == (e // per_group), row[e], NEG_INF) for e in range(n_experts)]
    m1, i1 = masked[0], jnp.zeros_like(gidx)
    for e in range(1, n_experts):
        better = masked[e] > m1
        i1 = jnp.where(better, e, i1)
        m1 = jnp.where(better, masked[e], m1)
    m2 = jnp.full_like(m1, -jnp.inf)
    i2 = jnp.zeros_like(gidx)
    for e in range(n_experts):
        cand = jnp.where(i1 == e, -jnp.inf, masked[e])
        better = cand > m2
        i2 = jnp.where(better, e, i2)
        m2 = jnp.where(better, cand, m2)
    w1 = functools.reduce(jnp.add, [jnp.where(i1 == e, srow[e], 0.0) for e in range(n_experts)])
    w2 = functools.reduce(jnp.add, [jnp.where(i2 == e, srow[e], 0.0) for e in range(n_experts)])
    tot = w1 + w2
    return i1, i2, w1 / tot * ROUTED_SCALE, w2 / tot * ROUTED_SCALE


META_ROWS = 8


def _norm_router_kernel(x_ref, gain_ref, shift_ref, scale_ref, wr_ref, bias_ref,
                        h_ref, meta_ref, cnt_ref, carry_ref, *, tr, n_ctx, n_experts):
    first = jnp.logical_and(pl.program_id(0) == 0, pl.program_id(1) == 0)

    @pl.when(first)
    def _():
        carry_ref[...] = jnp.zeros_like(carry_ref)

    x = x_ref[0]
    var = jnp.mean(x * x, axis=-1, keepdims=True)
    y = x * lax.rsqrt(var + NORM_EPS) * gain_ref[...]
    row = pl.program_id(1) * tr + lax.broadcasted_iota(jnp.int32, (tr, 1), 0)
    is_ctx = row < n_ctx
    shift = jnp.where(is_ctx, shift_ref[0, 0:1, :], shift_ref[0, 1:2, :])
    scale = jnp.where(is_ctx, scale_ref[0, 0:1, :], scale_ref[0, 1:2, :])
    h = y * (1.0 + scale) + shift
    h_ref[0] = h

    logits = _dot_nt(wr_ref[...], h.astype(BF16))
    i1, i2, w1, w2 = _route(logits, bias_ref[...], n_experts)
    erow = lax.broadcasted_iota(jnp.int32, (n_experts, tr), 0)
    hit1 = erow == i1
    hit2 = erow == i2
    onehot = jnp.where(jnp.logical_or(hit1, hit2), 1.0, 0.0)
    before = (lax.broadcasted_iota(jnp.int32, (tr, tr), 0)
              < lax.broadcasted_iota(jnp.int32, (tr, tr), 1))
    prefix = _dot(onehot.astype(BF16), jnp.where(before, 1.0, 0.0).astype(BF16))
    seen = prefix + carry_ref[:, 0:1]
    r1 = jnp.sum(jnp.where(hit1, seen, 0.0), axis=0, keepdims=True)
    r2 = jnp.sum(jnp.where(hit2, seen, 0.0), axis=0, keepdims=True)
    carry_ref[...] = carry_ref[...] + jnp.sum(onehot, axis=1, keepdims=True)
    zero = jnp.zeros_like(w1)
    meta_ref[...] = jnp.concatenate(
        [i1.astype(F32), i2.astype(F32), w1, w2, r1, r2, zero, zero], axis=0)
    cnt_ref[...] = carry_ref[...]


def _norm_router(xa, gain, shift, scale, wr_t, bias, n_ctx):
    B, R, D = xa.shape
    E = wr_t.shape[0]
    tr = _pick_tile(R, (256, 128))
    nt = R // tr
    return pl.pallas_call(
        functools.partial(_norm_router_kernel, tr=tr, n_ctx=n_ctx, n_experts=E),
        out_shape=(jax.ShapeDtypeStruct((B, R, D), F32),
                   jax.ShapeDtypeStruct((META_ROWS, B * R), F32),
                   jax.ShapeDtypeStruct((E, HEAD_DIM), F32)),
        grid=(B, nt),
        in_specs=[
            pl.BlockSpec((1, tr, D), lambda b, i: (b, i, 0)),
            pl.BlockSpec((1, D), lambda b, i: (0, 0)),
            pl.BlockSpec((1, 2, D), lambda b, i: (b, 0, 0)),
            pl.BlockSpec((1, 2, D), lambda b, i: (b, 0, 0)),
            pl.BlockSpec((E, D), lambda b, i: (0, 0)),
            pl.BlockSpec((E, 1), lambda b, i: (0, 0)),
        ],
        out_specs=(pl.BlockSpec((1, tr, D), lambda b, i: (b, i, 0)),
                   pl.BlockSpec((META_ROWS, tr), lambda b, i: (0, b * nt + i)),
                   pl.BlockSpec((E, HEAD_DIM), lambda b, i: (0, 0))),
        scratch_shapes=[pltpu.VMEM((E, HEAD_DIM), F32)],
        compiler_params=_cparams(2),
        name="norm_router",
    )(xa, gain.reshape(1, D), shift, scale, wr_t, bias.reshape(E, 1).astype(F32))


SUBLANES = 8


def _row_gather(idx_vmem_ref, idx_smem, isem, src_hbm, buf, sem, slot, n_rows):
    cp = pltpu.make_async_copy(idx_vmem_ref.at[0, 0], idx_smem.at[pl.ds(slot * n_rows, n_rows)], isem)
    cp.start()
    cp.wait()

    def body(g, carry):
        base = slot * n_rows + g * SUBLANES
        for s in range(SUBLANES):
            t = idx_smem[base + s]
            pltpu.make_async_copy(src_hbm.at[pl.ds(t, 1), :], buf.at[slot, g, pl.ds(s, 1), :],
                                  sem.at[slot]).start()
        return carry

    lax.fori_loop(0, n_rows // SUBLANES, body, 0)


def _row_gather_wait(src_hbm, buf, sem, slot, n_rows):
    def body(g, carry):
        pltpu.make_async_copy(src_hbm.at[pl.ds(0, SUBLANES), :], buf.at[slot, g], sem.at[slot]).wait()
        return carry

    lax.fori_loop(0, n_rows // SUBLANES, body, 0)


def _gathered(buf, slot):
    rows = buf[slot]
    return rows.reshape(rows.shape[0] * SUBLANES, rows.shape[2])


def _expert_kernel(te_ref, nv_ref, tok_ref, tok_next_ref, h_hbm, wg_ref, wu_ref, wd_ref,
                   o_ref, buf, idx_smem, sem, isem, *, tm):
    j = pl.program_id(0)
    n_valid = nv_ref[0]
    slot = j % 2

    @pl.when(j == 0)
    def _():
        _row_gather(tok_ref, idx_smem, isem, h_hbm, buf, sem, 0, tm)

    @pl.when(j + 1 < n_valid)
    def _():
        _row_gather(tok_next_ref, idx_smem, isem, h_hbm, buf, sem, 1 - slot, tm)

    @pl.when(j < n_valid)
    def _():
        _row_gather_wait(h_hbm, buf, sem, slot, tm)
        h = _gathered(buf, slot).astype(BF16)
        g = _dot(h, wg_ref[0, 0])
        u = _dot(h, wu_ref[0, 0])
        a = (g * _sigmoid(g) * u).astype(BF16)
        o_ref[...] = _dot(a, wd_ref[0, 0])

    @pl.when(j >= n_valid)
    def _():
        o_ref[...] = jnp.zeros_like(o_ref)


def _experts(h2d, tok3, tile_expert, n_valid, wg, wu, wd, layer, tm):
    M, D = h2d.shape
    F = wg.shape[-1]
    n_tiles = tok3.shape[0]
    P = n_tiles * tm
    grid_spec = pltpu.PrefetchScalarGridSpec(
        num_scalar_prefetch=2,
        grid=(n_tiles,),
        in_specs=[
            pl.BlockSpec((1, 1, tm), lambda j, te, nv: (j, 0, 0)),
            pl.BlockSpec((1, 1, tm), lambda j, te, nv: (jnp.minimum(j + 1, n_tiles - 1), 0, 0)),
            pl.BlockSpec(memory_space=pl.ANY),
            pl.BlockSpec((1, 1, D, F), lambda j, te, nv: (layer, te[j], 0, 0)),
            pl.BlockSpec((1, 1, D, F), lambda j, te, nv: (layer, te[j], 0, 0)),
            pl.BlockSpec((1, 1, F, D), lambda j, te, nv: (layer, te[j], 0, 0)),
        ],
        out_specs=pl.BlockSpec((tm, D), lambda j, te, nv: (j, 0)),
        scratch_shapes=[pltpu.VMEM((2, tm // SUBLANES, SUBLANES, D), F32), pltpu.SMEM((2 * tm,), jnp.int32),
                        pltpu.SemaphoreType.DMA((2,)), pltpu.SemaphoreType.DMA],
    )
    return pl.pallas_call(
        functools.partial(_expert_kernel, tm=tm),
        out_shape=jax.ShapeDtypeStruct((P, D), F32),
        grid_spec=grid_spec,
        compiler_params=_cparams(1),
        name="moe_experts",
    )(tile_expert, n_valid, tok3, tok3, h2d, wg, wu, wd)


def _combine_kernel(p1_ref, p1n_ref, p2_ref, p2n_ref, y_hbm, x_ref, w_ref, gate_ref, *rest,
                    tm, tiles_per_batch, n_ctx, with_norm):
    if with_norm == "next":
        gain_ref, shift_ref, scale_ref, o_ref, h_ref = rest[:5]
    else:
        gain_ref, o_ref = rest[:2]
    buf1, buf2, idx1, idx2, sem1, sem2, isem = rest[-7:]
    j = pl.program_id(0)
    slot = j % 2

    @pl.when(j == 0)
    def _():
        _row_gather(p1_ref, idx1, isem, y_hbm, buf1, sem1, 0, tm)
        _row_gather(p2_ref, idx2, isem, y_hbm, buf2, sem2, 0, tm)

    @pl.when(j + 1 < pl.num_programs(0))
    def _():
        _row_gather(p1n_ref, idx1, isem, y_hbm, buf1, sem1, 1 - slot, tm)
        _row_gather(p2n_ref, idx2, isem, y_hbm, buf2, sem2, 1 - slot, tm)

    _row_gather_wait(y_hbm, buf1, sem1, slot, tm)
    _row_gather_wait(y_hbm, buf2, sem2, slot, tm)
    step_rows = 2 * SUBLANES
    ctx_tile = (j % tiles_per_batch) * tm < n_ctx
    pick = lambda ref: jnp.where(ctx_tile, ref[0, 0:1, :], ref[0, 1:2, :])
    gate = pick(gate_ref)
    if with_norm == "next":
        gain_scale = gain_ref[...] * (1.0 + pick(scale_ref))
        shift = pick(shift_ref)

    def rows_step(i, carry):
        rows = pl.ds(pl.multiple_of(i * step_rows, step_rows), step_rows)
        b1 = jnp.concatenate([buf1[slot, 2 * i], buf1[slot, 2 * i + 1]], axis=0)
        b2 = jnp.concatenate([buf2[slot, 2 * i], buf2[slot, 2 * i + 1]], axis=0)
        w = w_ref[rows, :]
        x = x_ref[rows, :] + gate * (w[:, 0:1] * b1 + w[:, 1:2] * b2)
        normed = x * lax.rsqrt(jnp.mean(x * x, axis=-1, keepdims=True) + NORM_EPS)
        if with_norm == "next":
            o_ref[rows, :] = x
            h_ref[rows, :] = (normed * gain_scale + shift).astype(h_ref.dtype)
        else:
            o_ref[0, rows, :] = normed * gain_ref[...]
        return carry

    lax.fori_loop(0, tm // step_rows, rows_step, 0, unroll=4)


def _combine(x2d, y_sorted, pos1, pos2, w12, gate, dims, next_norm=None, final_gain=None):
    M, D = x2d.shape
    tm = dims["tm_comb"]
    n_ctx = dims["n_ctx"]
    assert n_ctx % tm == 0
    n_tiles = M // tm
    tpb = dims["rows"] // tm
    p1 = pos1.reshape(n_tiles, 1, tm)
    p2 = pos2.reshape(n_tiles, 1, tm)
    cur = pl.BlockSpec((1, 1, tm), lambda j: (j, 0, 0))
    nxt = pl.BlockSpec((1, 1, tm), lambda j: (jnp.minimum(j + 1, n_tiles - 1), 0, 0))
    rows = pl.BlockSpec((tm, D), lambda j: (j, 0))
    per_batch = pl.BlockSpec((1, 2, D), lambda j: (j // tpb, 0, 0))
    one_row = pl.BlockSpec((1, D), lambda j: (0, 0))
    if next_norm is not None:
        with_norm = "next"
        gain, shift, scale = next_norm
        extra_in = [one_row, per_batch, per_batch]
        extra_args = [gain.reshape(1, D), shift, scale]
        out_shape = (jax.ShapeDtypeStruct((M, D), F32), jax.ShapeDtypeStruct((M, D), BF16))
        out_specs = (rows, rows)
        aliases = {5: 0}
    else:
        with_norm = "final"
        extra_in = [one_row]
        extra_args = [final_gain.reshape(1, D)]
        ctx_tiles = n_ctx // tm
        out_shape = jax.ShapeDtypeStruct((M // dims["rows"], dims["rows"] - n_ctx, D), F32)
        out_specs = pl.BlockSpec(
            (1, tm, D), lambda j: (j // tpb, jnp.maximum(j % tpb - ctx_tiles, 0), 0))
        aliases = {}
    return pl.pallas_call(
        functools.partial(_combine_kernel, tm=tm, tiles_per_batch=tpb, n_ctx=dims["n_ctx"],
                          with_norm=with_norm),
        out_shape=out_shape,
        grid=(n_tiles,),
        in_specs=[cur, nxt, cur, nxt,
                  pl.BlockSpec(memory_space=pl.ANY),
                  rows,
                  pl.BlockSpec((tm, 2), lambda j: (j, 0)),
                  per_batch] + extra_in,
        out_specs=out_specs,
        scratch_shapes=[pltpu.VMEM((2, tm // SUBLANES, SUBLANES, D), F32),
                        pltpu.VMEM((2, tm // SUBLANES, SUBLANES, D), F32),
                        pltpu.SMEM((2 * tm,), jnp.int32), pltpu.SMEM((2 * tm,), jnp.int32),
                        pltpu.SemaphoreType.DMA((2,)), pltpu.SemaphoreType.DMA((2,)),
                        pltpu.SemaphoreType.DMA],
        input_output_aliases=aliases,
        compiler_params=_cparams(1),
        name="moe_combine",
    )(p1, p1, p2, p2, y_sorted, x2d, w12, gate, *extra_args)


def _dispatch_plan(meta, counts, tm, n_tiles):
    M = meta.shape[1]
    i1 = meta[0].astype(jnp.int32)
    i2 = meta[1].astype(jnp.int32)
    r1 = meta[4].astype(jnp.int32)
    r2 = meta[5].astype(jnp.int32)
    cnt = counts[:, 0].astype(jnp.int32)
    padded = ((cnt + tm - 1) // tm) * tm
    seg_end = jnp.cumsum(padded)
    seg_start = seg_end - padded
    pos1 = seg_start[i1] + r1
    pos2 = seg_start[i2] + r2
    n_valid = seg_end[-1] // tm
    tile_start = jnp.arange(n_tiles, dtype=jnp.int32) * tm
    probe = jnp.minimum(tile_start, seg_end[-1] - tm)
    tile_expert = jnp.sum(probe[:, None] >= seg_end[None, :], axis=1).astype(jnp.int32)
    tok = jnp.arange(M, dtype=jnp.int32)
    tok_sorted = jnp.zeros((n_tiles * tm,), jnp.int32).at[jnp.concatenate([pos1, pos2])].set(
        jnp.concatenate([tok, tok]), unique_indices=True)
    w12 = jnp.stack([meta[2], meta[3]], axis=1)
    return (pos1, pos2, w12, tok_sorted.reshape(n_tiles, 1, tm), tile_expert,
            n_valid.reshape(1).astype(jnp.int32))


def _rope_tables(seq, n_ctx):
    rows = seq // GRID_W
    row = jnp.repeat(jnp.arange(rows), GRID_W).astype(F32)
    col = jnp.tile(jnp.arange(GRID_W), rows).astype(F32)
    inv = ROPE_THETA ** (-(jnp.arange(0, ROPE_AXIS_DIM, 2, dtype=F32) / ROPE_AXIS_DIM))
    ang_r = row[:, None] * inv
    ang_c = col[:, None] * inv
    ang = jnp.concatenate([ang_r, ang_r, ang_c, ang_c], axis=-1)
    lane = jnp.arange(HEAD_DIM)
    sign = jnp.where((lane % ROPE_AXIS_DIM) < (ROPE_AXIS_DIM // 2), -1.0, 1.0).astype(F32)
    cos = jnp.concatenate([jnp.ones((n_ctx, HEAD_DIM), F32), jnp.cos(ang)], axis=0)
    sin = jnp.concatenate([jnp.zeros((n_ctx, HEAD_DIM), F32), jnp.sin(ang) * sign], axis=0)
    return cos, sin


def _dims(D, S, C, d_expert):
    ha = D // 256
    hkv = ha // 4
    hb = D // 512
    widths = [("ka", hkv * HEAD_DIM), ("va", hkv * HEAD_DIM), ("kb", hb * 2 * HEAD_DIM),
              ("vb", hb * 2 * HEAD_DIM), ("qa", ha * HEAD_DIM), ("qb", hb * 2 * HEAD_DIM),
              ("gate", 2 * D)]
    seg, off = {}, 0
    for name, wdt in widths:
        seg[name] = (off, off + wdt)
        off += wdt
    R = C + S
    tm = _pick_tile(R, (768, 512, 384, 256, 128))
    tn = _pick_tile(math.gcd(D, seg["gate"][0]), (1024, 512, 256, 128))
    tn_in = _pick_tile(off, (1024, 512, 256, 128))
    tm_in = _pick_tile(R, (1056, 768, 512, 384, 256, 128))
    tm_comb = _pick_tile(R, (256, 128))
    return dict(ha=ha, hkv=hkv, group=ha // hkv, hb=hb, seg=seg, n_cols=off, rows=R, n_ctx=C,
                tm=tm, tn=tn, tn_in=tn_in, tm_in=tm_in, tm_exp=256, tm_comb=tm_comb)


def kernel(x, c, ctx, c_ctx, ada_down, ada_up, ada_bias, norm_mix, norm_ffn, w_in, sink_logit,
           lam_q1, lam_k1, lam_q2, lam_k2, subln_gain, w_branch_a, w_branch_b, w_out,
           w_router, b_router, w_exp_gate, w_exp_up, w_exp_down, norm_final):
    B, S, D = x.shape
    C = ctx.shape[1]
    depth = w_in.shape[0]
    dims = _dims(D, S, C, w_exp_gate.shape[-1])
    R = dims["rows"]
    M = B * R

    cond = jnp.concatenate([c, c_ctx[None], jnp.zeros((8 - (B + 1) % 8 if (B + 1) % 8 else 0, D), F32)])
    mods = _adaln(cond, ada_down, ada_up, ada_bias)
    mods = mods.reshape(depth, mods.shape[1], N_MOD, D)
    lat = mods[:, :B]
    cx = jnp.broadcast_to(mods[:, B:B + 1], lat.shape)
    mod = jnp.stack([cx, lat], axis=2)

    cos_tab, sin_tab = _rope_tables(S, C)
    wr_t = w_router.T.astype(BF16)
    xa = jnp.concatenate([ctx, x], axis=1)
    w_in_b, wa_b, wb_b, wo_b = (w.astype(BF16) for w in (w_in, w_branch_a, w_branch_b, w_out))
    wg_b, wu_b, wd_b = (w.astype(BF16) for w in (w_exp_gate, w_exp_up, w_exp_down))
    n_experts = w_router.shape[1]
    tm_exp = dims["tm_exp"]
    n_exp_tiles = -(-(2 * M + n_experts * (tm_exp - 1)) // tm_exp)

    mods_of = lambda l: [mod[l, :, :, k] for k in range(N_MOD)]
    h_mix = _norm_mod(xa, norm_mix[0], mods_of(0)[0], mods_of(0)[1], C).reshape(M, D)
    for l in range(depth):
        lam_init = 0.8 - 0.6 * math.exp(-0.3 * l)
        m = mods_of(l)
        px = _inproj(h_mix, w_in_b, l, cos_tab, sin_tab, dims)
        px3 = px.reshape(B, R, -1)
        ya = _win_attn(px3, sink_logit[l], dims)
        yb = _diff_attn(px3, lam_q1[l], lam_k1[l], lam_q2[l], lam_k2[l], subln_gain[l], lam_init, dims)
        mg = _merge(ya.reshape(M, -1), yb.reshape(M, -1), wa_b, wb_b, l, px, dims)
        x2d = _outproj(mg, wo_b, l, xa.reshape(M, D), m[2], dims)
        xa = x2d.reshape(B, R, D)
        h, meta, counts = _norm_router(xa, norm_ffn[l], m[3], m[4], wr_t, b_router, C)
        pos1, pos2, w12, tok3, tile_expert, n_valid = _dispatch_plan(meta, counts, tm_exp, n_exp_tiles)
        y_sorted = _experts(h.reshape(M, D), tok3, tile_expert, n_valid, wg_b, wu_b, wd_b, l, tm_exp)
        if l + 1 < depth:
            nm = mods_of(l + 1)
            x2d, h_mix = _combine(x2d, y_sorted, pos1, pos2, w12, m[5], dims,
                                  next_norm=(norm_mix[l + 1], nm[0], nm[1]))
            xa = x2d.reshape(B, R, D)
        else:
            return _combine(x2d, y_sorted, pos1, pos2, w12, m[5], dims, final_gain=norm_final)
```
